```python
import math
import jax, jax.numpy as jnp
from jax import lax
import numpy as np

D_MODEL = 2048
BATCH = 4
SEQ = 2048
DEPTH = 1

HEAD_DIM = 64
DIL_PATTERNS = ((128, 1), (512, 4), (2048, 16))
N_GROUPS_A = len(DIL_PATTERNS)
HEADS_PER_GROUP_A = 8
N_HEADS_A = N_GROUPS_A * HEADS_PER_GROUP_A
WIDTH_A = N_HEADS_A * HEAD_DIM
OUT_WIDTH_A = HEADS_PER_GROUP_A * HEAD_DIM
BLOCK = 128
N_HEADS_B = D_MODEL // (2 * HEAD_DIM)
WIDTH_B = N_HEADS_B * 2 * HEAD_DIM
NUM_BUCKETS = 32
MAX_DISTANCE = 2048
N_BIAS_HEADS = N_HEADS_A + N_HEADS_B
D_FF = -(-8 * D_MODEL // (3 * 256)) * 256
SPLIT_SIZES = (WIDTH_A, WIDTH_A, WIDTH_A, WIDTH_B, WIDTH_B, WIDTH_B, D_MODEL, D_MODEL)
D_IN = sum(SPLIT_SIZES)
NORM_EPS = 1e-6
NEG_INF = -1e30

kernel_name = "hybrid_dilated_diff_gated_block"


def rms_norm(x, g):
    xf = x.astype(jnp.float32)
    y = xf * lax.rsqrt(jnp.mean(xf * xf, axis=-1, keepdims=True) + NORM_EPS)
    return (y * g.astype(jnp.float32)).astype(x.dtype)


def rel_bucket(dist):
    n = jnp.maximum(dist, 0)
    max_exact = NUM_BUCKETS // 2
    nf = jnp.maximum(n, 1).astype(jnp.float32)
    large = max_exact + (jnp.log(nf / max_exact) / math.log(MAX_DISTANCE / max_exact)
                         * (NUM_BUCKETS - max_exact)).astype(jnp.int32)
    large = jnp.minimum(large, NUM_BUCKETS - 1)
    return jnp.where(n < max_exact, n, large)


def dilated_group_attention(q, k, v, table_g, window, dilation):
    B, S, H, E = q.shape
    L = S // dilation
    nb = -(-L // BLOCK)
    Lp = nb * BLOCK
    sub_w = window // dilation

    def to_sub(t):
        t = t.reshape(B, L, dilation, H, E).transpose(0, 2, 3, 1, 4)
        return jnp.pad(t, ((0, 0), (0, 0), (0, 0), (0, Lp - L), (0, 0)))

    qs, ks, vs = to_sub(q), to_sub(k), to_sub(v)
    qb = qs.reshape(B, dilation, H, nb, BLOCK, E)

    def windows(t):
        t = jnp.pad(t, ((0, 0), (0, 0), (0, 0), (BLOCK, 0), (0, 0)))
        t = t.reshape(B, dilation, H, nb + 1, BLOCK, E)
        return jnp.concatenate([t[:, :, :, :-1], t[:, :, :, 1:]], axis=4)

    kw, vw = windows(ks), windows(vs)
    ql = jnp.arange(BLOCK)[:, None]
    kl = jnp.arange(2 * BLOCK)[None, :]
    rel = BLOCK + ql - kl
    bias = table_g[rel_bucket(rel * dilation)]
    bias = bias.transpose(2, 0, 1).astype(jnp.float32)
    band = (rel >= 0) & (rel <= sub_w)
    blk = jnp.arange(nb)[:, None, None]
    valid = band[None] & (blk * BLOCK + kl[None] - BLOCK >= 0)

    s = jnp.einsum('bdhnqe,bdhnke->bdhnqk', qb, kw).astype(jnp.float32) * (E ** -0.5)
    s = s + bias[None, None, :, None]
    s = jnp.where(valid[None, None, None], s, NEG_INF)
    m = jnp.max(s, axis=-1, keepdims=True)
    p = jnp.exp(s - m)
    den = jnp.sum(p, axis=-1, keepdims=True)
    o = jnp.einsum('bdhnqk,bdhnke->bdhnqe', p, vw.astype(jnp.float32)) / den
    lse = (m + jnp.log(den))[..., 0]
    o = o.reshape(B, dilation, H, Lp, E)[:, :, :, :L].transpose(0, 3, 1, 2, 4).reshape(B, S, H, E)
    lse = lse.reshape(B, dilation, H, Lp)[..., :L].transpose(0, 3, 1, 2).reshape(B, S, H)
    return o, lse


def dilated_mixture_attention(qa, ka, va, table_a):
    B, S = qa.shape[:2]
    outs, lses = [], []
    for g, (window, dilation) in enumerate(DIL_PATTERNS):
        tg = table_a[:, g * HEADS_PER_GROUP_A:(g + 1) * HEADS_PER_GROUP_A]
        o, lse = dilated_group_attention(qa[:, :, g], ka[:, :, g], va[:, :, g], tg, window, dilation)
        outs.append(o)
        lses.append(lse)
    o = jnp.stack(outs, axis=-2)
    alpha = jax.nn.softmax(jnp.stack(lses, axis=-1), axis=-1)
    o = jnp.sum(alpha[..., None] * o, axis=-2)
    return o.reshape(B, S, OUT_WIDTH_A)


def diff_attention(q, k, v, table_b, lam, lam_init, subln_g):
    B, S, H, _, E = q.shape
    nb = S // BLOCK
    qb = q.reshape(B, nb, BLOCK, H, 2, E).transpose(1, 0, 3, 4, 2, 5)
    kt = k.transpose(0, 2, 3, 1, 4)
    vt = v.transpose(0, 2, 1, 3).astype(jnp.float32)
    kpos = jnp.arange(S)

    def block_fn(args):
        qblk, start = args
        s = jnp.einsum('bhcqe,bhcke->bhcqk', qblk, kt).astype(jnp.float32) * (E ** -0.5)
        rel = start + jnp.arange(BLOCK)[:, None] - kpos[None, :]
        bias = table_b[rel_bucket(rel)].transpose(2, 0, 1).astype(jnp.float32)
        s = s + bias[None, :, None]
        s = jnp.where((rel >= 0)[None, None, None], s, NEG_INF)
        p = jax.nn.softmax(s, axis=-1)
        w = p[:, :, 0] - lam * p[:, :, 1]
        return jnp.einsum('bhqk,bhke->bhqe', w, vt)

    starts = jnp.arange(nb) * BLOCK
    o = lax.map(block_fn, (qb, starts))
    o = o.transpose(1, 0, 3, 2, 4).reshape(B, S, H, 2 * E)
    o = rms_norm(o, subln_g) * (1.0 - lam_init)
    return o.reshape(B, S, H * 2 * E)


def setup_inputs(seed: int = 0) -> dict:
    key = jax.random.key(seed)
    ks = jax.random.split(key, 18)
    f32 = jnp.float32

    def nrm(k, shape, scale):
        return jax.random.normal(k, shape, f32) * scale

    return {
        "x": nrm(ks[0], (BATCH, SEQ, D_MODEL), 1.0),
        "norm_attn_g": 1.0 + nrm(ks[1], (DEPTH, D_MODEL), 0.02),
        "w_in": nrm(ks[2], (DEPTH, D_MODEL, D_IN), D_MODEL ** -0.5),
        "w_proj_a": nrm(ks[3], (DEPTH, OUT_WIDTH_A, D_MODEL), OUT_WIDTH_A ** -0.5),
        "w_proj_b": nrm(ks[4], (DEPTH, WIDTH_B, D_MODEL), WIDTH_B ** -0.5),
        "w_out": nrm(ks[5], (DEPTH, D_MODEL, D_MODEL), D_MODEL ** -0.5),
        "rel_bias_table": nrm(ks[6], (NUM_BUCKETS, N_BIAS_HEADS), 0.5),
        "diff_lambda_q1": nrm(ks[7], (DEPTH, HEAD_DIM), 0.1),
        "diff_lambda_k1": nrm(ks[8], (DEPTH, HEAD_DIM), 0.1),
        "diff_lambda_q2": nrm(ks[9], (DEPTH, HEAD_DIM), 0.1),
        "diff_lambda_k2": nrm(ks[10], (DEPTH, HEAD_DIM), 0.1),
        "diff_subln_g": 1.0 + nrm(ks[11], (DEPTH, 2 * HEAD_DIM), 0.02),
        "norm_ffn_g": 1.0 + nrm(ks[12], (DEPTH, D_MODEL), 0.02),
        "w_ffn_gate": nrm(ks[13], (DEPTH, D_MODEL, D_FF), D_MODEL ** -0.5),
        "w_ffn_up": nrm(ks[14], (DEPTH, D_MODEL, D_FF), D_MODEL ** -0.5),
        "w_ffn_down": nrm(ks[15], (DEPTH, D_FF, D_MODEL), D_FF ** -0.5),
        "norm_final_g": 1.0 + nrm(ks[16], (D_MODEL,), 0.02),
    }


def reference(x, norm_attn_g, w_in, w_proj_a, w_proj_b, w_out, rel_bias_table,
              diff_lambda_q1, diff_lambda_k1, diff_lambda_q2, diff_lambda_k2, diff_subln_g,
              norm_ffn_g, w_ffn_gate, w_ffn_up, w_ffn_down, norm_final_g):
    B, S, _ = x.shape
    split_at = [int(v) for v in np.cumsum(SPLIT_SIZES)[:-1]]
    table_a = rel_bias_table[:, :N_HEADS_A]
    table_b = rel_bias_table[:, N_HEADS_A:]
    for l in range(DEPTH):
        h = rms_norm(x, norm_attn_g[l])
        proj = jnp.einsum('bsd,de->bse', h, w_in[l])
        qa, ka, va, qb, kb, vb, ga, gb = jnp.split(proj, split_at, axis=-1)
        shape_a = (B, S, N_GROUPS_A, HEADS_PER_GROUP_A, HEAD_DIM)
        ya = dilated_mixture_attention(qa.reshape(shape_a), ka.reshape(shape_a),
                                       va.reshape(shape_a), table_a)

        lam_init = 0.8 - 0.6 * math.exp(-0.3 * l)
        lq1, lk1 = diff_lambda_q1[l].astype(jnp.float32), diff_lambda_k1[l].astype(jnp.float32)
        lq2, lk2 = diff_lambda_q2[l].astype(jnp.float32), diff_lambda_k2[l].astype(jnp.float32)
        lam = jnp.exp(jnp.sum(lq1 * lk1)) - jnp.exp(jnp.sum(lq2 * lk2)) + lam_init
        shape_b = (B, S, N_HEADS_B, 2, HEAD_DIM)
        yb = diff_attention(qb.reshape(shape_b), kb.reshape(shape_b),
                            vb.reshape(B, S, N_HEADS_B, 2 * HEAD_DIM),
                            table_b, lam, lam_init, diff_subln_g[l])

        ya = jnp.einsum('bse,ed->bsd', ya.astype(x.dtype), w_proj_a[l])
        yb = jnp.einsum('bse,ed->bsd', yb.astype(x.dtype), w_proj_b[l])
        merged = jax.nn.sigmoid(ga) * ya + jax.nn.sigmoid(gb) * yb
        x = x + jnp.einsum('bsd,de->bse', merged, w_out[l])

        h2 = rms_norm(x, norm_ffn_g[l])
        u = jax.nn.silu(jnp.einsum('bsd,df->bsf', h2, w_ffn_gate[l])) * jnp.einsum('bsd,df->bsf', h2, w_ffn_up[l])
        x = x + jnp.einsum('bsf,fd->bsd', u, w_ffn_down[l])
    return rms_norm(x, norm_final_g)
```

```python
import functools
import math

import numpy as np
import jax
import jax.numpy as jnp
from jax import lax
from jax.experimental import pallas as pl
from jax.experimental.pallas import tpu as pltpu

D_MODEL = 2048
HEAD_DIM = 64
DIL_PATTERNS = ((128, 1), (512, 4), (2048, 16))
N_GROUPS_A = len(DIL_PATTERNS)
HEADS_PER_GROUP_A = 8
N_HEADS_A = N_GROUPS_A * HEADS_PER_GROUP_A
WIDTH_A = N_HEADS_A * HEAD_DIM
OUT_WIDTH_A = HEADS_PER_GROUP_A * HEAD_DIM
BLOCK = 128
N_HEADS_B = D_MODEL // (2 * HEAD_DIM)
WIDTH_B = N_HEADS_B * 2 * HEAD_DIM
NUM_BUCKETS = 32
MAX_DISTANCE = 2048
D_FF = -(-8 * D_MODEL // (3 * 256)) * 256
D_IN = 3 * WIDTH_A + 3 * WIDTH_B + 2 * D_MODEL
NORM_EPS = 1e-6
NEG_INF = -1e30
SCALE = HEAD_DIM ** -0.5

OFF_QA, OFF_KA, OFF_VA = 0, WIDTH_A, 2 * WIDTH_A
OFF_QB = 3 * WIDTH_A
OFF_KB = OFF_QB + WIDTH_B
OFF_VB = OFF_KB + WIDTH_B
OFF_GA = OFF_VB + WIDTH_B
OFF_GB = OFF_GA + D_MODEL

LANES = 128
VMEM_LIMIT = 56 * 1024 * 1024

BF16 = jnp.bfloat16
F32 = jnp.float32


def _rel_bucket_np(dist):
    n = np.maximum(dist, 0)
    max_exact = NUM_BUCKETS // 2
    nf = np.maximum(n, 1).astype(np.float32)
    large = max_exact + (np.log(nf / np.float32(max_exact)) / np.float32(math.log(MAX_DISTANCE / max_exact))
                         * np.float32(NUM_BUCKETS - max_exact)).astype(np.int32)
    large = np.minimum(large, NUM_BUCKETS - 1)
    return np.where(n < max_exact, n, large).astype(np.int32)


def _rms(x, g):
    ms = jnp.mean(x * x, axis=-1, keepdims=True)
    return x * lax.rsqrt(ms + NORM_EPS) * g


def _params(sem, vmem=VMEM_LIMIT):
    return pltpu.CompilerParams(dimension_semantics=sem, vmem_limit_bytes=vmem)


def _in_proj_kernel(x_ref, g_ref, w_ref, o_ref, h_ref):
    @pl.when(pl.program_id(1) == 0)
    def _():
        h_ref[...] = _rms(x_ref[...], g_ref[...]).astype(BF16)

    o_ref[...] = jnp.dot(h_ref[...], w_ref[...], preferred_element_type=F32).astype(o_ref.dtype)


def _in_proj(x2, g, w, tm=1024, tn=512):
    T, D = x2.shape
    N = w.shape[1]
    return pl.pallas_call(
        _in_proj_kernel,
        grid=(T // tm, N // tn),
        in_specs=[pl.BlockSpec((tm, D), lambda i, j: (i, 0)),
                  pl.BlockSpec((1, D), lambda i, j: (0, 0)),
                  pl.BlockSpec((D, tn), lambda i, j: (0, j))],
        out_specs=pl.BlockSpec((tm, tn), lambda i, j: (i, j)),
        out_shape=jax.ShapeDtypeStruct((T, N), BF16),
        scratch_shapes=[pltpu.VMEM((tm, D), BF16)],
        compiler_params=_params(("parallel", "arbitrary")),
        name="in_proj",
    )(x2, g, w)


def _mixer_a_kernel(q_ref, k_ref, v_ref, bias_ref, o_ref, lse_ref, *, nb):
    lane = lax.broadcasted_iota(jnp.int32, (BLOCK, LANES), 1)
    lo = lane < HEAD_DIM
    row = lax.broadcasted_iota(jnp.int32, (2 * BLOCK, 2 * BLOCK), 0)
    col = lax.broadcasted_iota(jnp.int32, (2 * BLOCK, 2 * BLOCK), 1)
    rel = BLOCK + (row & (BLOCK - 1)) - col
    band = (rel >= 0) & (rel <= BLOCK)
    band_first = band & (col >= BLOCK)

    def block(q_rows, kw_of, vw_of, valid):
        for hp in range(OUT_WIDTH_A // LANES):
            cs = pl.ds(hp * LANES, LANES)
            q = q_ref[0, q_rows, cs]
            zero = jnp.zeros_like(q)
            qz = jnp.concatenate([jnp.where(lo, q, zero), jnp.where(lo, zero, q)], axis=0)
            kw = kw_of(cs)
            vw = vw_of(cs)
            s = lax.dot_general(qz, kw, (((1,), (1,)), ((), ())), preferred_element_type=F32) * SCALE
            s = s + bias_ref[hp]
            s = jnp.where(valid, s, NEG_INF)
            m = jnp.max(s, axis=-1, keepdims=True)
            p = jnp.exp(s - m)
            den = jnp.sum(p, axis=-1, keepdims=True)
            acc = jnp.dot(p.astype(BF16), vw, preferred_element_type=F32)
            o = acc / den
            lse = jnp.broadcast_to(m + jnp.log(den), (2 * BLOCK, LANES))
            o_ref[0, q_rows, cs] = jnp.where(lo, o[:BLOCK], o[BLOCK:])
            lse_ref[0, q_rows, cs] = jnp.where(lo, lse[:BLOCK], lse[BLOCK:])

    first = pl.ds(0, BLOCK)
    block(first,
          lambda cs: jnp.concatenate([k_ref[0, first, cs], k_ref[0, first, cs]], axis=0),
          lambda cs: jnp.concatenate([v_ref[0, first, cs], v_ref[0, first, cs]], axis=0),
          band_first)

    if nb > 1:
        def body(n, carry):
            q_rows = pl.ds(pl.multiple_of(n * BLOCK, BLOCK), BLOCK)
            win = pl.ds(pl.multiple_of((n - 1) * BLOCK, BLOCK), 2 * BLOCK)
            block(q_rows, lambda cs: k_ref[0, win, cs], lambda cs: v_ref[0, win, cs], band)
            return carry

        lax.fori_loop(1, nb, body, 0)


def _mixer_a_group(proj, bias_g, g, dilation):
    B, S, _ = proj.shape
    L = S // dilation
    nb = L // BLOCK
    W = OUT_WIDTH_A
    pv = proj.reshape(B, L, dilation * D_IN)
    per = D_IN // W

    def col(off):
        return lambda b, r: (b, 0, r * per + off // W + g)

    blk = pl.BlockSpec((1, L, W), lambda b, r: (b, 0, r))
    o, lse = pl.pallas_call(
        functools.partial(_mixer_a_kernel, nb=nb),
        grid=(B, dilation),
        in_specs=[pl.BlockSpec((1, L, W), col(OFF_QA)),
                  pl.BlockSpec((1, L, W), col(OFF_KA)),
                  pl.BlockSpec((1, L, W), col(OFF_VA)),
                  pl.BlockSpec((W // LANES, 2 * BLOCK, 2 * BLOCK), lambda b, r: (0, 0, 0))],
        out_specs=[blk, blk],
        out_shape=[jax.ShapeDtypeStruct((B, L, dilation * W), F32)] * 2,
        compiler_params=_params(("parallel", "parallel")),
        name=f"mixer_a{g}",
    )(pv, pv, pv, bias_g)
    return o.reshape(B, S, W), lse.reshape(B, S, W)


def _mixer_b_kernel(q_ref, k_ref, v_ref, rev_ref, lam_ref, g_ref, o_ref,
                    toep_ref, s_ref, mpart_ref, lpart_ref, acc_ref, *, tq, seq, lam_init):
    qi = pl.program_id(2)
    nchunk = seq // tq
    nsub = tq // LANES

    @pl.when(qi == 0)
    def _():
        x = jnp.broadcast_to(rev_ref[0], (tq, seq + tq))
        rolled = pltpu.roll(x, 0, 1, stride=1, stride_axis=0)
        for c in range(nchunk + 1):
            toep_ref[c] = rolled[:, c * tq:(c + 1) * tq]

    lane = lax.broadcasted_iota(jnp.int32, (tq, LANES), 1)
    lo = lane < HEAD_DIM
    q = q_ref[0]
    zero = jnp.zeros_like(q)
    qz = jnp.concatenate([jnp.where(lo, q, zero), jnp.where(lo, zero, q)], axis=0)

    def scores(j, bias):
        kc = k_ref[0, pl.ds(pl.multiple_of(j * tq, tq), tq), :]
        s = lax.dot_general(qz, kc, (((1,), (1,)), ((), ())), preferred_element_type=F32) * SCALE
        return s + jnp.concatenate([bias, bias], axis=0)

    def fold_max(s):
        m = mpart_ref[...]
        for c in range(nsub):
            m = jnp.maximum(m, s[:, c * LANES:(c + 1) * LANES])
        mpart_ref[...] = m

    mpart_ref[...] = jnp.full(mpart_ref.shape, -jnp.inf, F32)
    row = lax.broadcasted_iota(jnp.int32, (2 * tq, tq), 0)
    col = lax.broadcasted_iota(jnp.int32, (2 * tq, tq), 1)
    s = scores(qi, toep_ref[nchunk])
    s = jnp.where(col <= (row & (tq - 1)), s, NEG_INF)
    s_ref[qi] = s
    fold_max(s)

    def pass1(j, carry):
        s = scores(j, toep_ref[nchunk - qi + j])
        s_ref[j] = s
        fold_max(s)
        return carry

    lax.fori_loop(0, qi, pass1, 0)
    m = jnp.max(mpart_ref[...], axis=-1, keepdims=True)

    lpart_ref[...] = jnp.zeros(lpart_ref.shape, F32)
    acc_ref[...] = jnp.zeros(acc_ref.shape, F32)

    def pass2(j, carry):
        p = jnp.exp(s_ref[j] - m)
        l = lpart_ref[...]
        for c in range(nsub):
            l = l + p[:, c * LANES:(c + 1) * LANES]
        lpart_ref[...] = l
        vc = v_ref[0, pl.ds(pl.multiple_of(j * tq, tq), tq), :]
        acc_ref[...] += jnp.dot(p.astype(BF16), vc, preferred_element_type=F32)
        return carry

    lax.fori_loop(0, qi + 1, pass2, 0)

    den = jnp.sum(lpart_ref[...], axis=-1, keepdims=True)
    o = acc_ref[...] / den
    lv = lam_ref[...]
    lam = (jnp.exp(jnp.sum(lv[0:1] * lv[1:2], axis=-1, keepdims=True))
           - jnp.exp(jnp.sum(lv[2:3] * lv[3:4], axis=-1, keepdims=True)) + lam_init)
    y = o[:tq] - lam * o[tq:]
    y = _rms(y, g_ref[...]) * (1.0 - lam_init)
    o_ref[0] = y.astype(o_ref.dtype)


def _mixer_b(proj, rev_b, lam_vecs, subln_g, lam_init, tq=256):
    B, S, _ = proj.shape
    H = N_HEADS_B
    nq = S // tq
    kern = functools.partial(_mixer_b_kernel, tq=tq, seq=S, lam_init=lam_init)
    return pl.pallas_call(
        kern,
        grid=(B, H, nq),
        in_specs=[pl.BlockSpec((1, tq, LANES), lambda b, h, i: (b, i, OFF_QB // LANES + h)),
                  pl.BlockSpec((1, S, LANES), lambda b, h, i: (b, 0, OFF_KB // LANES + h)),
                  pl.BlockSpec((1, S, LANES), lambda b, h, i: (b, 0, OFF_VB // LANES + h)),
                  pl.BlockSpec((1, 1, S + tq), lambda b, h, i: (h, 0, 0)),
                  pl.BlockSpec((4, HEAD_DIM), lambda b, h, i: (0, 0)),
                  pl.BlockSpec((1, 2 * HEAD_DIM), lambda b, h, i: (0, 0))],
        out_specs=pl.BlockSpec((1, tq, LANES), lambda b, h, i: (b, i, h)),
        out_shape=jax.ShapeDtypeStruct((B, S, WIDTH_B), BF16),
        scratch_shapes=[pltpu.VMEM((nq + 1, tq, tq), F32),
                        pltpu.VMEM((nq, 2 * tq, tq), F32),
                        pltpu.VMEM((2 * tq, LANES), F32),
                        pltpu.VMEM((2 * tq, LANES), F32),
                        pltpu.VMEM((2 * tq, LANES), F32)],
        compiler_params=_params(("parallel", "parallel", "arbitrary")),
        name="mixer_b",
    )(proj, proj, proj, rev_b, lam_vecs, subln_g)


def _gate_merge_kernel(o0_ref, o1_ref, o2_ref, l0_ref, l1_ref, l2_ref, yb_ref, ga_ref, gb_ref,
                       wa_ref, wb_ref, out_ref, ya_ref):
    @pl.when(pl.program_id(1) == 0)
    def _():
        l0, l1, l2 = l0_ref[...], l1_ref[...], l2_ref[...]
        m = jnp.maximum(jnp.maximum(l0, l1), l2)
        e0, e1, e2 = jnp.exp(l0 - m), jnp.exp(l1 - m), jnp.exp(l2 - m)
        ya = (e0 * o0_ref[...] + e1 * o1_ref[...] + e2 * o2_ref[...]) / (e0 + e1 + e2)
        ya_ref[...] = ya.astype(BF16)

    pa = jnp.dot(ya_ref[...], wa_ref[...], preferred_element_type=F32)
    pb = jnp.dot(yb_ref[...], wb_ref[...], preferred_element_type=F32)
    ga = jax.nn.sigmoid(ga_ref[...].astype(F32))
    gb = jax.nn.sigmoid(gb_ref[...].astype(F32))
    out_ref[...] = (ga * pa + gb * pb).astype(out_ref.dtype)


def _gate_merge(oa, lsea, yb, proj2, wa, wb, tm=1024, tn=512):
    T = yb.shape[0]
    D = D_MODEL
    row_a = pl.BlockSpec((tm, OUT_WIDTH_A), lambda i, j: (i, 0))
    return pl.pallas_call(
        _gate_merge_kernel,
        grid=(T // tm, D // tn),
        in_specs=[row_a] * 6 + [
            pl.BlockSpec((tm, WIDTH_B), lambda i, j: (i, 0)),
            pl.BlockSpec((tm, tn), lambda i, j: (i, OFF_GA // tn + j)),
            pl.BlockSpec((tm, tn), lambda i, j: (i, OFF_GB // tn + j)),
            pl.BlockSpec((OUT_WIDTH_A, tn), lambda i, j: (0, j)),
            pl.BlockSpec((WIDTH_B, tn), lambda i, j: (0, j))],
        out_specs=pl.BlockSpec((tm, tn), lambda i, j: (i, j)),
        out_shape=jax.ShapeDtypeStruct((T, D), BF16),
        scratch_shapes=[pltpu.VMEM((tm, OUT_WIDTH_A), BF16)],
        compiler_params=_params(("parallel", "arbitrary")),
        name="gate_merge",
    )(*oa, *lsea, yb, proj2, proj2, wa, wb)


def _out_proj_kernel(m_ref, w_ref, x_ref, o_ref):
    o_ref[...] = x_ref[...] + jnp.dot(m_ref[...], w_ref[...], preferred_element_type=F32)


def _out_proj(merged, w, x2, tm=1024, tn=512):
    T, D = x2.shape
    return pl.pallas_call(
        _out_proj_kernel,
        grid=(T // tm, D // tn),
        in_specs=[pl.BlockSpec((tm, D), lambda i, j: (i, 0)),
                  pl.BlockSpec((D, tn), lambda i, j: (0, j)),
                  pl.BlockSpec((tm, tn), lambda i, j: (i, j))],
        out_specs=pl.BlockSpec((tm, tn), lambda i, j: (i, j)),
        out_shape=jax.ShapeDtypeStruct((T, D), F32),
        compiler_params=_params(("parallel", "arbitrary")),
        name="out_proj",
    )(merged, w, x2)


def _ffn_kernel(x_ref, g_ref, wg_ref, wu_ref, wd_ref, gf_ref, o_ref, h_ref, acc_ref):
    f = pl.program_id(1)

    @pl.when(f == 0)
    def _():
        h_ref[...] = _rms(x_ref[...], g_ref[...]).astype(BF16)
        acc_ref[...] = jnp.zeros(acc_ref.shape, F32)

    h = h_ref[...]
    a = jnp.dot(h, wg_ref[...], preferred_element_type=F32)
    b = jnp.dot(h, wu_ref[...], preferred_element_type=F32)
    u = (a * jax.nn.sigmoid(a)) * b
    acc_ref[...] += jnp.dot(u.astype(BF16), wd_ref[...], preferred_element_type=F32)

    @pl.when(f == pl.num_programs(1) - 1)
    def _():
        o_ref[...] = _rms(x_ref[...] + acc_ref[...], gf_ref[...])


def _ffn(x1, g, wg, wu, wd, gf, tm=512, tf=512):
    T, D = x1.shape
    F = wg.shape[1]
    return pl.pallas_call(
        _ffn_kernel,
        grid=(T // tm, F // tf),
        in_specs=[pl.BlockSpec((tm, D), lambda i, f: (i, 0)),
                  pl.BlockSpec((1, D), lambda i, f: (0, 0)),
                  pl.BlockSpec((D, tf), lambda i, f: (0, f)),
                  pl.BlockSpec((D, tf), lambda i, f: (0, f)),
                  pl.BlockSpec((tf, D), lambda i, f: (f, 0)),
                  pl.BlockSpec((1, D), lambda i, f: (0, 0))],
        out_specs=pl.BlockSpec((tm, D), lambda i, f: (i, 0)),
        out_shape=jax.ShapeDtypeStruct((T, D), F32),
        scratch_shapes=[pltpu.VMEM((tm, D), BF16), pltpu.VMEM((tm, D), F32)],
        compiler_params=_params(("parallel", "arbitrary")),
        name="ffn",
    )(x1, g, wg, wu, wd, gf)


def _bias_a_index():
    ql = np.arange(BLOCK)[:, None]
    kl = np.arange(2 * BLOCK)[None, :]
    rel = BLOCK + ql - kl
    return np.stack([_rel_bucket_np(rel * d) for _, d in DIL_PATTERNS])


def _rev_b_index(seq, tq):
    c = np.arange(seq + tq)
    return _rel_bucket_np(np.clip(seq - c, 0, seq - 1))


def kernel(x, norm_attn_g, w_in, w_proj_a, w_proj_b, w_out, rel_bias_table, diff_lambda_q1, diff_lambda_k1, diff_lambda_q2, diff_lambda_k2, diff_subln_g, norm_ffn_g, w_ffn_gate, w_ffn_up, w_ffn_down, norm_final_g):
    B, S, D = x.shape
    T = B * S
    depth = w_in.shape[0]
    assert depth == 1, "the final RMSNorm is fused into the FFN epilogue of a single layer"
    table_a = rel_bias_table[:, :N_HEADS_A]
    table_b = rel_bias_table[:, N_HEADS_A:]
    tq = 256

    idx_a = _bias_a_index()
    bias_a = []
    for g in range(N_GROUPS_A):
        tg = table_a[:, g * HEADS_PER_GROUP_A:(g + 1) * HEADS_PER_GROUP_A]
        bg = jnp.transpose(tg[idx_a[g]], (2, 0, 1)).astype(F32)
        bias_a.append(bg.reshape(OUT_WIDTH_A // LANES, 2 * BLOCK, 2 * BLOCK))
    rev_b = jnp.transpose(table_b[_rev_b_index(S, tq)], (1, 0)).astype(F32)[:, None, :]

    x2 = x.reshape(T, D)
    for l in range(depth):
        lam_init = 0.8 - 0.6 * math.exp(-0.3 * l)
        proj2 = _in_proj(x2, norm_attn_g[l][None, :], w_in[l].astype(BF16))
        proj = proj2.reshape(B, S, D_IN)

        oa, lsea = [], []
        for g, (_, dilation) in enumerate(DIL_PATTERNS):
            o, lse = _mixer_a_group(proj, bias_a[g], g, dilation)
            oa.append(o.reshape(T, OUT_WIDTH_A))
            lsea.append(lse.reshape(T, OUT_WIDTH_A))

        lam_vecs = jnp.stack([diff_lambda_q1[l], diff_lambda_k1[l],
                              diff_lambda_q2[l], diff_lambda_k2[l]]).astype(F32)
        yb = _mixer_b(proj, rev_b, lam_vecs, diff_subln_g[l][None, :].astype(F32), lam_init, tq=tq)

        merged = _gate_merge(oa, lsea, yb.reshape(T, WIDTH_B), proj2,
                             w_proj_a[l].astype(BF16), w_proj_b[l].astype(BF16))
        x2 = _out_proj(merged, w_out[l].astype(BF16), x2)

        x2 = _ffn(x2, norm_ffn_g[l][None, :], w_ffn_gate[l].astype(BF16), w_ffn_up[l].astype(BF16),
                  w_ffn_down[l].astype(BF16), norm_final_g[None, :])
    return x2.reshape(B, S, D)
```

```python
import functools
import math

import numpy as np
import jax
import jax.numpy as jnp
from jax import lax
from jax.experimental import pallas as pl
from jax.experimental.pallas import tpu as pltpu

D_MODEL = 2048
HEAD_DIM = 64
DIL_PATTERNS = ((128, 1), (512, 4), (2048, 16))
N_GROUPS_A = len(DIL_PATTERNS)
HEADS_PER_GROUP_A = 8
N_HEADS_A = N_GROUPS_A * HEADS_PER_GROUP_A
WIDTH_A = N_HEADS_A * HEAD_DIM
OUT_WIDTH_A = HEADS_PER_GROUP_A * HEAD_DIM
BLOCK = 128
N_HEADS_B = D_MODEL // (2 * HEAD_DIM)
WIDTH_B = N_HEADS_B * 2 * HEAD_DIM
NUM_BUCKETS = 32
MAX_DISTANCE = 2048
D_FF = -(-8 * D_MODEL // (3 * 256)) * 256
PROJ_A = 3 * WIDTH_A
PROJ_R = 3 * WIDTH_B + 2 * D_MODEL
D_IN = PROJ_A + PROJ_R
NORM_EPS = 1e-6
NEG_INF = -1e30
SCALE = HEAD_DIM ** -0.5

OFF_QB = 0
OFF_KB = WIDTH_B
OFF_VB = 2 * WIDTH_B
OFF_GA = 3 * WIDTH_B
OFF_GB = OFF_GA + D_MODEL

LANES = 128
VMEM_LIMIT = 56 * 1024 * 1024

BF16 = jnp.bfloat16
F32 = jnp.float32


def _rel_bucket_np(dist):
    n = np.maximum(dist, 0)
    max_exact = NUM_BUCKETS // 2
    nf = np.maximum(n, 1).astype(np.float32)
    large = max_exact + (np.log(nf / np.float32(max_exact)) / np.float32(math.log(MAX_DISTANCE / max_exact))
                         * np.float32(NUM_BUCKETS - max_exact)).astype(np.int32)
    large = np.minimum(large, NUM_BUCKETS - 1)
    return np.where(n < max_exact, n, large).astype(np.int32)


def _rms(x, g):
    ms = jnp.mean(x * x, axis=-1, keepdims=True)
    return x * lax.rsqrt(ms + NORM_EPS) * g


def _params(sem, vmem=VMEM_LIMIT):
    return pltpu.CompilerParams(dimension_semantics=sem, vmem_limit_bytes=vmem)


def _in_proj_kernel(x_ref, g_ref, w_ref, oa_ref, or_ref, h_ref, *, na):
    j = pl.program_id(1)

    @pl.when(j == 0)
    def _():
        h_ref[...] = _rms(x_ref[...], g_ref[...]).astype(BF16)

    res = jnp.dot(h_ref[...], w_ref[...].astype(BF16), preferred_element_type=F32)

    @pl.when(j < na)
    def _():
        oa_ref[...] = res

    @pl.when(j >= na)
    def _():
        or_ref[...] = res.astype(BF16)


def _in_proj(x2, g, w, tm=1024, tn=512):
    T, D = x2.shape
    na = PROJ_A // tn
    return pl.pallas_call(
        functools.partial(_in_proj_kernel, na=na),
        grid=(T // tm, D_IN // tn),
        in_specs=[pl.BlockSpec((tm, D), lambda i, j: (i, 0)),
                  pl.BlockSpec((1, D), lambda i, j: (0, 0)),
                  pl.BlockSpec((D, tn), lambda i, j: (0, j))],
        out_specs=[pl.BlockSpec((tm, tn), lambda i, j: (i, jnp.minimum(j, na - 1))),
                   pl.BlockSpec((tm, tn), lambda i, j: (i, jnp.maximum(j - na, 0)))],
        out_shape=[jax.ShapeDtypeStruct((T, PROJ_A), F32), jax.ShapeDtypeStruct((T, PROJ_R), BF16)],
        scratch_shapes=[pltpu.VMEM((tm, D), BF16)],
        compiler_params=_params(("parallel", "arbitrary")),
        name="in_proj",
    )(x2, g, w)


def _mixer_a_kernel(q0_ref, k0_ref, v0_ref, q1_ref, k1_ref, v1_ref, q2_ref, k2_ref, v2_ref, rev_ref,
                    o_ref, bias_ref, m_ref, l_ref, acc_ref, *, seq):
    qkv = ((q0_ref, k0_ref, v0_ref), (q1_ref, k1_ref, v1_ref), (q2_ref, k2_ref, v2_ref))
    lane = lax.broadcasted_iota(jnp.int32, (BLOCK, LANES), 1)
    lo = lane < HEAD_DIM

    for g in range(N_GROUPS_A):
        for hh in range(2):
            x = jnp.broadcast_to(rev_ref[0, g * 2 + hh:g * 2 + hh + 1, :], (BLOCK, 4 * BLOCK))
            rolled = pltpu.roll(x, 0, 1, stride=1, stride_axis=0)
            bias_ref[g, hh * BLOCK:(hh + 1) * BLOCK, :] = rolled[:, BLOCK:3 * BLOCK]

    row = lax.broadcasted_iota(jnp.int32, (2 * BLOCK, 2 * BLOCK), 0)
    col = lax.broadcasted_iota(jnp.int32, (2 * BLOCK, 2 * BLOCK), 1)
    rel = BLOCK + (row & (BLOCK - 1)) - col
    band = (rel >= 0) & (rel <= BLOCK)
    band_first = band & (col >= BLOCK)

    def attend(g, rows, q, kw, vw, valid):
        zero = jnp.zeros_like(q)
        qz = jnp.concatenate([jnp.where(lo, q, zero), jnp.where(lo, zero, q)], axis=0)
        s = lax.dot_general(qz, kw, (((1,), (1,)), ((), ())), preferred_element_type=F32) * SCALE
        s = jnp.where(valid, s + bias_ref[g], NEG_INF)
        m = jnp.max(s, axis=-1, keepdims=True)
        p = jnp.exp(s - m)
        den = jnp.sum(p, axis=-1, keepdims=True)
        acc = jnp.dot(p.astype(BF16), vw, preferred_element_type=F32)
        mb = jnp.broadcast_to(m, (2 * BLOCK, LANES))
        lb = jnp.broadcast_to(den, (2 * BLOCK, LANES))
        m_ref[g, rows, :] = jnp.where(lo, mb[:BLOCK], mb[BLOCK:])
        l_ref[g, rows, :] = jnp.where(lo, lb[:BLOCK], lb[BLOCK:])
        acc_ref[g, rows, :] = jnp.where(lo, acc[:BLOCK], acc[BLOCK:])

    q_ref, k_ref, v_ref = qkv[0]
    first = pl.ds(0, BLOCK)
    k_first = k_ref[0, first, :].astype(BF16)
    v_first = v_ref[0, first, :].astype(BF16)
    attend(0, first, q_ref[0, first, :].astype(BF16),
           jnp.concatenate([k_first, k_first], axis=0), jnp.concatenate([v_first, v_first], axis=0), band_first)

    def body(n, carry):
        rows = pl.ds(pl.multiple_of(n * BLOCK, BLOCK), BLOCK)
        win = pl.ds(pl.multiple_of((n - 1) * BLOCK, BLOCK), 2 * BLOCK)
        attend(0, rows, q_ref[0, rows, :].astype(BF16), k_ref[0, win, :].astype(BF16),
               v_ref[0, win, :].astype(BF16), band)
        return carry

    lax.fori_loop(1, seq // BLOCK, body, 0)

    for g in range(1, N_GROUPS_A):
        d = DIL_PATTERNS[g][1]
        sub_len = seq // d
        q_ref, k_ref, v_ref = qkv[g]
        for r in range(d):
            sub = pl.ds(r, sub_len, stride=d)
            q = q_ref[0, sub, :].astype(BF16)
            k = k_ref[0, sub, :].astype(BF16)
            v = v_ref[0, sub, :].astype(BF16)
            for n in range(sub_len // BLOCK):
                rows = pl.ds(n * BLOCK * d + r, BLOCK, stride=d)
                qb = q[n * BLOCK:(n + 1) * BLOCK]
                if n == 0:
                    attend(g, rows, qb, jnp.concatenate([k[:BLOCK], k[:BLOCK]], axis=0),
                           jnp.concatenate([v[:BLOCK], v[:BLOCK]], axis=0), band_first)
                else:
                    attend(g, rows, qb, k[(n - 1) * BLOCK:(n + 1) * BLOCK],
                           v[(n - 1) * BLOCK:(n + 1) * BLOCK], band)

    def merge(i, carry):
        rows = pl.ds(pl.multiple_of(i * 2 * BLOCK, 2 * BLOCK), 2 * BLOCK)
        ms = [m_ref[g, rows, :] for g in range(N_GROUPS_A)]
        mx = jnp.maximum(jnp.maximum(ms[0], ms[1]), ms[2])
        num = jnp.zeros((2 * BLOCK, LANES), F32)
        den = jnp.zeros((2 * BLOCK, LANES), F32)
        for g in range(N_GROUPS_A):
            w = jnp.exp(ms[g] - mx)
            num = num + w * acc_ref[g, rows, :]
            den = den + w * l_ref[g, rows, :]
        o_ref[0, rows, :] = (num / den).astype(o_ref.dtype)
        return carry

    lax.fori_loop(0, seq // (2 * BLOCK), merge, 0)


def _mixer_a(proj_a, rev_a):
    B, S, _ = proj_a.shape
    npair = OUT_WIDTH_A // LANES

    def col(which, g):
        base = (which * WIDTH_A + g * OUT_WIDTH_A) // LANES
        return pl.BlockSpec((1, S, LANES), lambda b, hp: (b, 0, base + hp))

    in_specs = [col(which, g) for g in range(N_GROUPS_A) for which in range(3)]
    in_specs.append(pl.BlockSpec((1, 2 * N_GROUPS_A, 4 * BLOCK), lambda b, hp: (hp, 0, 0)))
    return pl.pallas_call(
        functools.partial(_mixer_a_kernel, seq=S),
        grid=(B, npair),
        in_specs=in_specs,
        out_specs=pl.BlockSpec((1, S, LANES), lambda b, hp: (b, 0, hp)),
        out_shape=jax.ShapeDtypeStruct((B, S, OUT_WIDTH_A), BF16),
        scratch_shapes=[pltpu.VMEM((N_GROUPS_A, 2 * BLOCK, 2 * BLOCK), F32),
                        pltpu.VMEM((N_GROUPS_A, S, LANES), F32),
                        pltpu.VMEM((N_GROUPS_A, S, LANES), F32),
                        pltpu.VMEM((N_GROUPS_A, S, LANES), F32)],
        compiler_params=_params(("parallel", "parallel")),
        name="mixer_a",
    )(*([proj_a] * 9), rev_a)


def _mixer_b_kernel(q_ref, k_ref, v_ref, rev_ref, lam_ref, g_ref, o_ref,
                    toep_ref, s_ref, mpart_ref, lpart_ref, acc_ref, *, tq, seq, lam_init):
    qi = pl.program_id(2)
    nchunk = seq // tq
    nsub = tq // LANES

    @pl.when(qi == 0)
    def _():
        x = jnp.broadcast_to(rev_ref[0], (tq, seq + tq))
        rolled = pltpu.roll(x, 0, 1, stride=1, stride_axis=0)
        for c in range(nchunk + 1):
            toep_ref[c] = rolled[:, c * tq:(c + 1) * tq]

    lane = lax.broadcasted_iota(jnp.int32, (tq, LANES), 1)
    lo = lane < HEAD_DIM
    q = q_ref[0]
    zero = jnp.zeros_like(q)
    qz = jnp.concatenate([jnp.where(lo, q, zero), jnp.where(lo, zero, q)], axis=0)

    def scores(j, bias):
        kc = k_ref[0, pl.ds(pl.multiple_of(j * tq, tq), tq), :]
        s = lax.dot_general(qz, kc, (((1,), (1,)), ((), ())), preferred_element_type=F32) * SCALE
        return s + jnp.concatenate([bias, bias], axis=0)

    def fold_max(s):
        m = mpart_ref[...]
        for c in range(nsub):
            m = jnp.maximum(m, s[:, c * LANES:(c + 1) * LANES])
        mpart_ref[...] = m

    mpart_ref[...] = jnp.full(mpart_ref.shape, -jnp.inf, F32)
    row = lax.broadcasted_iota(jnp.int32, (2 * tq, tq), 0)
    col = lax.broadcasted_iota(jnp.int32, (2 * tq, tq), 1)
    s = scores(qi, toep_ref[nchunk])
    s = jnp.where(col <= (row & (tq - 1)), s, NEG_INF)
    s_ref[qi] = s
    fold_max(s)

    def pass1(j, carry):
        s = scores(j, toep_ref[nchunk - qi + j])
        s_ref[j] = s
        fold_max(s)
        return carry

    lax.fori_loop(0, qi, pass1, 0)
    m = jnp.max(mpart_ref[...], axis=-1, keepdims=True)

    lpart_ref[...] = jnp.zeros(lpart_ref.shape, F32)
    acc_ref[...] = jnp.zeros(acc_ref.shape, F32)

    def pass2(j, carry):
        p = jnp.exp(s_ref[j] - m)
        l = lpart_ref[...]
        for c in range(nsub):
            l = l + p[:, c * LANES:(c + 1) * LANES]
        lpart_ref[...] = l
        vc = v_ref[0, pl.ds(pl.multiple_of(j * tq, tq), tq), :]
        acc_ref[...] += jnp.dot(p.astype(BF16), vc, preferred_element_type=F32)
        return carry

    lax.fori_loop(0, qi + 1, pass2, 0)

    den = jnp.sum(lpart_ref[...], axis=-1, keepdims=True)
    o = acc_ref[...] / den
    lv = lam_ref[...]
    lam = (jnp.exp(jnp.sum(lv[0:1] * lv[1:2], axis=-1, keepdims=True))
           - jnp.exp(jnp.sum(lv[2:3] * lv[3:4], axis=-1, keepdims=True)) + lam_init)
    y = o[:tq] - lam * o[tq:]
    y = _rms(y, g_ref[...]) * (1.0 - lam_init)
    o_ref[0] = y.astype(o_ref.dtype)


def _mixer_b(proj, rev_b, lam_vecs, subln_g, lam_init, tq=256):
    B, S, _ = proj.shape
    H = N_HEADS_B
    nq = S // tq
    kern = functools.partial(_mixer_b_kernel, tq=tq, seq=S, lam_init=lam_init)
    return pl.pallas_call(
        kern,
        grid=(B, H, nq),
        in_specs=[pl.BlockSpec((1, tq, LANES), lambda b, h, i: (b, i, OFF_QB // LANES + h)),
                  pl.BlockSpec((1, S, LANES), lambda b, h, i: (b, 0, OFF_KB // LANES + h)),
                  pl.BlockSpec((1, S, LANES), lambda b, h, i: (b, 0, OFF_VB // LANES + h)),
                  pl.BlockSpec((1, 1, S + tq), lambda b, h, i: (h, 0, 0)),
                  pl.BlockSpec((4, HEAD_DIM), lambda b, h, i: (0, 0)),
                  pl.BlockSpec((1, 2 * HEAD_DIM), lambda b, h, i: (0, 0))],
        out_specs=pl.BlockSpec((1, tq, LANES), lambda b, h, i: (b, i, h)),
        out_shape=jax.ShapeDtypeStruct((B, S, WIDTH_B), BF16),
        scratch_shapes=[pltpu.VMEM((nq + 1, tq, tq), F32),
                        pltpu.VMEM((nq, 2 * tq, tq), F32),
                        pltpu.VMEM((2 * tq, LANES), F32),
                        pltpu.VMEM((2 * tq, LANES), F32),
                        pltpu.VMEM((2 * tq, LANES), F32)],
        compiler_params=_params(("parallel", "parallel", "arbitrary")),
        name="mixer_b",
    )(proj, proj, proj, rev_b, lam_vecs, subln_g)


def _gate_merge_kernel(ya_ref, yb_ref, ga_ref, gb_ref, wa_ref, wb_ref, out_ref):
    pa = jnp.dot(ya_ref[...], wa_ref[...].astype(BF16), preferred_element_type=F32)
    pb = jnp.dot(yb_ref[...], wb_ref[...].astype(BF16), preferred_element_type=F32)
    ga = jax.nn.sigmoid(ga_ref[...].astype(F32))
    gb = jax.nn.sigmoid(gb_ref[...].astype(F32))
    out_ref[...] = (ga * pa + gb * pb).astype(out_ref.dtype)


def _gate_merge(ya, yb, proj_r, wa, wb, tm=1024, tn=512):
    T = yb.shape[0]
    D = D_MODEL
    return pl.pallas_call(
        _gate_merge_kernel,
        grid=(T // tm, D // tn),
        in_specs=[pl.BlockSpec((tm, OUT_WIDTH_A), lambda i, j: (i, 0)),
                  pl.BlockSpec((tm, WIDTH_B), lambda i, j: (i, 0)),
                  pl.BlockSpec((tm, tn), lambda i, j: (i, OFF_GA // tn + j)),
                  pl.BlockSpec((tm, tn), lambda i, j: (i, OFF_GB // tn + j)),
                  pl.BlockSpec((OUT_WIDTH_A, tn), lambda i, j: (0, j)),
                  pl.BlockSpec((WIDTH_B, tn), lambda i, j: (0, j))],
        out_specs=pl.BlockSpec((tm, tn), lambda i, j: (i, j)),
        out_shape=jax.ShapeDtypeStruct((T, D), BF16),
        compiler_params=_params(("parallel", "arbitrary")),
        name="gate_merge",
    )(ya, yb, proj_r, proj_r, wa, wb)


def _out_proj_kernel(m_ref, w_ref, x_ref, o_ref):
    o_ref[...] = x_ref[...] + jnp.dot(m_ref[...], w_ref[...].astype(BF16), preferred_element_type=F32)


def _out_proj(merged, w, x2, tm=1024, tn=512):
    T, D = x2.shape
    return pl.pallas_call(
        _out_proj_kernel,
        grid=(T // tm, D // tn),
        in_specs=[pl.BlockSpec((tm, D), lambda i, j: (i, 0)),
                  pl.BlockSpec((D, tn), lambda i, j: (0, j)),
                  pl.BlockSpec((tm, tn), lambda i, j: (i, j))],
        out_specs=pl.BlockSpec((tm, tn), lambda i, j: (i, j)),
        out_shape=jax.ShapeDtypeStruct((T, D), F32),
        compiler_params=_params(("parallel", "arbitrary")),
        name="out_proj",
    )(merged, w, x2)


def _ffn_kernel(x_ref, g_ref, wg_ref, wu_ref, wd_ref, gf_ref, o_ref, h_ref):
    f = pl.program_id(1)

    @pl.when(f == 0)
    def _():
        h_ref[...] = _rms(x_ref[...], g_ref[...]).astype(BF16)

    h = h_ref[...]
    a = jnp.dot(h, wg_ref[...].astype(BF16), preferred_element_type=F32)
    b = jnp.dot(h, wu_ref[...].astype(BF16), preferred_element_type=F32)
    u = (a * jax.nn.sigmoid(a)) * b
    d = jnp.dot(u.astype(BF16), wd_ref[...].astype(BF16), preferred_element_type=F32)

    @pl.when(f == 0)
    def _():
        o_ref[...] = x_ref[...] + d

    @pl.when(f > 0)
    def _():
        o_ref[...] += d

    @pl.when(f == pl.num_programs(1) - 1)
    def _():
        o_ref[...] = _rms(o_ref[...], gf_ref[...])


def _ffn(x1, g, wg, wu, wd, gf, tm=1024, tf=256):
    T, D = x1.shape
    F = wg.shape[1]
    return pl.pallas_call(
        _ffn_kernel,
        grid=(T // tm, F // tf),
        in_specs=[pl.BlockSpec((tm, D), lambda i, f: (i, 0), pipeline_mode=pl.Buffered(1)),
                  pl.BlockSpec((1, D), lambda i, f: (0, 0)),
                  pl.BlockSpec((D, tf), lambda i, f: (0, f)),
                  pl.BlockSpec((D, tf), lambda i, f: (0, f)),
                  pl.BlockSpec((tf, D), lambda i, f: (f, 0)),
                  pl.BlockSpec((1, D), lambda i, f: (0, 0))],
        out_specs=pl.BlockSpec((tm, D), lambda i, f: (i, 0)),
        out_shape=jax.ShapeDtypeStruct((T, D), F32),
        scratch_shapes=[pltpu.VMEM((tm, D), BF16)],
        compiler_params=_params(("parallel", "arbitrary")),
        name="ffn",
    )(x1, g, wg, wu, wd, gf)


def _rev_a_index():
    u = np.arange(4 * BLOCK)
    rel = np.clip(2 * BLOCK - u, 0, None)
    return np.stack([_rel_bucket_np(rel * d) for _, d in DIL_PATTERNS])


def _rev_b_index(seq, tq):
    c = np.arange(seq + tq)
    return _rel_bucket_np(np.clip(seq - c, 0, seq - 1))


def kernel(x, norm_attn_g, w_in, w_proj_a, w_proj_b, w_out, rel_bias_table, diff_lambda_q1, diff_lambda_k1, diff_lambda_q2, diff_lambda_k2, diff_subln_g, norm_ffn_g, w_ffn_gate, w_ffn_up, w_ffn_down, norm_final_g):
    B, S, D = x.shape
    T = B * S
    depth = w_in.shape[0]
    assert depth == 1, "the final RMSNorm is fused into the FFN epilogue of a single layer"
    table_a = rel_bias_table[:, :N_HEADS_A].astype(F32)
    table_b = rel_bias_table[:, N_HEADS_A:].astype(F32)
    tq = 256

    idx_a = _rev_a_index()
    rev_a = jnp.stack([table_a[idx_a[g], g * HEADS_PER_GROUP_A:(g + 1) * HEADS_PER_GROUP_A]
                       for g in range(N_GROUPS_A)])
    npair = OUT_WIDTH_A // LANES
    rev_a = jnp.transpose(rev_a.reshape(N_GROUPS_A, 4 * BLOCK, npair, 2), (2, 0, 3, 1))
    rev_a = rev_a.reshape(npair, 2 * N_GROUPS_A, 4 * BLOCK)
    rev_b = jnp.transpose(table_b[_rev_b_index(S, tq)], (1, 0))[:, None, :]

    x2 = x.reshape(T, D)
    l = 0
    lam_init = 0.8 - 0.6 * math.exp(-0.3 * l)
    proj_a, proj_r = _in_proj(x2, norm_attn_g[l][None, :], w_in[l])
    ya = _mixer_a(proj_a.reshape(B, S, PROJ_A), rev_a)

    lam_vecs = jnp.stack([diff_lambda_q1[l], diff_lambda_k1[l],
                          diff_lambda_q2[l], diff_lambda_k2[l]]).astype(F32)
    yb = _mixer_b(proj_r.reshape(B, S, PROJ_R), rev_b, lam_vecs, diff_subln_g[l][None, :].astype(F32),
                  lam_init, tq=tq)

    merged = _gate_merge(ya.reshape(T, OUT_WIDTH_A), yb.reshape(T, WIDTH_B), proj_r, w_proj_a[l], w_proj_b[l])
    x2 = _out_proj(merged, w_out[l], x2)
    x2 = _ffn(x2, norm_ffn_g[l][None, :], w_ffn_gate[l], w_ffn_up[l], w_ffn_down[l], norm_final_g[None, :])
    return x2.reshape(B, S, D)
```

```python
import functools
import math

import numpy as np
import jax
import jax.numpy as jnp
from jax import lax
from jax.experimental import pallas as pl
from jax.experimental.pallas import tpu as pltpu

D_MODEL = 2048
HEAD_DIM = 64
DIL_PATTERNS = ((128, 1), (512, 4), (2048, 16))
N_GROUPS_A = len(DIL_PATTERNS)
HEADS_PER_GROUP_A = 8
N_HEADS_A = N_GROUPS_A * HEADS_PER_GROUP_A
WIDTH_A = N_HEADS_A * HEAD_DIM
OUT_WIDTH_A = HEADS_PER_GROUP_A * HEAD_DIM
BLOCK = 128
N_HEADS_B = D_MODEL // (2 * HEAD_DIM)
WIDTH_B = N_HEADS_B * 2 * HEAD_DIM
NUM_BUCKETS = 32
MAX_DISTANCE = 2048
D_FF = -(-8 * D_MODEL // (3 * 256)) * 256
PROJ_A = 3 * WIDTH_A
PROJ_R = 3 * WIDTH_B + 2 * D_MODEL
D_IN = PROJ_A + PROJ_R
NORM_EPS = 1e-6
NEG_INF = -1e30
SCALE = HEAD_DIM ** -0.5
LOG2E = math.log2(math.e)

OFF_QB = 0
OFF_KB = WIDTH_B
OFF_VB = 2 * WIDTH_B
OFF_GA = 3 * WIDTH_B
OFF_GB = OFF_GA + D_MODEL

LANES = 128
VMEM_LIMIT = 56 * 1024 * 1024

BF16 = jnp.bfloat16
F32 = jnp.float32


def _rel_bucket_np(dist):
    n = np.maximum(dist, 0)
    max_exact = NUM_BUCKETS // 2
    nf = np.maximum(n, 1).astype(np.float32)
    large = max_exact + (np.log(nf / np.float32(max_exact)) / np.float32(math.log(MAX_DISTANCE / max_exact))
                         * np.float32(NUM_BUCKETS - max_exact)).astype(np.int32)
    large = np.minimum(large, NUM_BUCKETS - 1)
    return np.where(n < max_exact, n, large).astype(np.int32)


def _rms(x, g):
    ms = jnp.mean(x * x, axis=-1, keepdims=True)
    return x * lax.rsqrt(ms + NORM_EPS) * g


def _params(sem, vmem=VMEM_LIMIT):
    return pltpu.CompilerParams(dimension_semantics=sem, vmem_limit_bytes=vmem)


def _in_proj_kernel(x_ref, g_ref, w_ref, oa_ref, or_ref, h_ref, *, na):
    j = pl.program_id(1)

    @pl.when(j == 0)
    def _():
        h_ref[...] = _rms(x_ref[...], g_ref[...]).astype(BF16)

    @pl.when(j < na)
    def _():
        oa_ref[...] = jnp.dot(h_ref[...], w_ref[...].astype(BF16), preferred_element_type=F32)

    @pl.when(j >= na)
    def _():
        or_ref[...] = jnp.dot(h_ref[...], w_ref[...].astype(BF16), preferred_element_type=F32).astype(BF16)


def _in_proj(x2, g, w, tm=1024, tn=512):
    T, D = x2.shape
    na = PROJ_A // tn
    return pl.pallas_call(
        functools.partial(_in_proj_kernel, na=na),
        grid=(T // tm, D_IN // tn),
        in_specs=[pl.BlockSpec((tm, D), lambda i, j: (i, 0)),
                  pl.BlockSpec((1, D), lambda i, j: (0, 0)),
                  pl.BlockSpec((D, tn), lambda i, j: (0, j))],
        out_specs=[pl.BlockSpec((tm, tn), lambda i, j: (i, jnp.minimum(j, na - 1))),
                   pl.BlockSpec((tm, tn), lambda i, j: (i, jnp.maximum(j - na, 0)))],
        out_shape=[jax.ShapeDtypeStruct((T, PROJ_A), F32), jax.ShapeDtypeStruct((T, PROJ_R), BF16)],
        scratch_shapes=[pltpu.VMEM((tm, D), BF16)],
        compiler_params=_params(("parallel", "arbitrary")),
        name="in_proj",
    )(x2, g, w)


def _mixer_a_kernel(q0_ref, k0_ref, v0_ref, q1_ref, k1_ref, v1_ref, q2_ref, k2_ref, v2_ref, rev_ref,
                    o_ref, bias_ref, m_ref, l_ref, acc_ref, *, seq):
    qkv = ((q0_ref, k0_ref, v0_ref), (q1_ref, k1_ref, v1_ref), (q2_ref, k2_ref, v2_ref))
    lane = lax.broadcasted_iota(jnp.int32, (BLOCK, LANES), 1)
    lo = lane < HEAD_DIM

    for g in range(N_GROUPS_A):
        for hh in range(2):
            x = jnp.broadcast_to(rev_ref[0, g * 2 + hh:g * 2 + hh + 1, :], (BLOCK, 4 * BLOCK))
            rolled = pltpu.roll(x, 0, 1, stride=1, stride_axis=0)
            bias_ref[g, hh * BLOCK:(hh + 1) * BLOCK, :] = rolled[:, BLOCK:3 * BLOCK]

    row = lax.broadcasted_iota(jnp.int32, (2 * BLOCK, 2 * BLOCK), 0)
    col = lax.broadcasted_iota(jnp.int32, (2 * BLOCK, 2 * BLOCK), 1)
    rel = BLOCK + (row & (BLOCK - 1)) - col
    band = (rel >= 0) & (rel <= BLOCK)
    band_first = band & (col >= BLOCK)

    def attend(g, rows, q, kw, vw, valid):
        zero = jnp.zeros_like(q)
        qz = jnp.concatenate([jnp.where(lo, q, zero), jnp.where(lo, zero, q)], axis=0)
        s = lax.dot_general(qz, kw, (((1,), (1,)), ((), ())), preferred_element_type=F32) * SCALE
        s = jnp.where(valid, s + bias_ref[g], NEG_INF)
        m = jnp.max(s, axis=-1, keepdims=True)
        p = jnp.exp(s - m)
        den = jnp.sum(p, axis=-1, keepdims=True)
        acc = jnp.dot(p.astype(BF16), vw, preferred_element_type=F32)
        mb = jnp.broadcast_to(m, (2 * BLOCK, LANES))
        lb = jnp.broadcast_to(den, (2 * BLOCK, LANES))
        m_ref[g, rows, :] = jnp.where(lo, mb[:BLOCK], mb[BLOCK:])
        l_ref[g, rows, :] = jnp.where(lo, lb[:BLOCK], lb[BLOCK:])
        acc_ref[g, rows, :] = jnp.where(lo, acc[:BLOCK], acc[BLOCK:])

    q_ref, k_ref, v_ref = qkv[0]
    first = pl.ds(0, BLOCK)
    k_first = k_ref[0, first, :].astype(BF16)
    v_first = v_ref[0, first, :].astype(BF16)
    attend(0, first, q_ref[0, first, :].astype(BF16),
           jnp.concatenate([k_first, k_first], axis=0), jnp.concatenate([v_first, v_first], axis=0), band_first)

    def body(n, carry):
        rows = pl.ds(pl.multiple_of(n * BLOCK, BLOCK), BLOCK)
        win = pl.ds(pl.multiple_of((n - 1) * BLOCK, BLOCK), 2 * BLOCK)
        attend(0, rows, q_ref[0, rows, :].astype(BF16), k_ref[0, win, :].astype(BF16),
               v_ref[0, win, :].astype(BF16), band)
        return carry

    lax.fori_loop(1, seq // BLOCK, body, 0)

    for g in range(1, N_GROUPS_A):
        d = DIL_PATTERNS[g][1]
        sub_len = seq // d
        q_ref, k_ref, v_ref = qkv[g]
        for r in range(d):
            sub = pl.ds(r, sub_len, stride=d)
            q = q_ref[0, sub, :].astype(BF16)
            k = k_ref[0, sub, :].astype(BF16)
            v = v_ref[0, sub, :].astype(BF16)
            for n in range(sub_len // BLOCK):
                rows = pl.ds(n * BLOCK * d + r, BLOCK, stride=d)
                qb = q[n * BLOCK:(n + 1) * BLOCK]
                if n == 0:
                    attend(g, rows, qb, jnp.concatenate([k[:BLOCK], k[:BLOCK]], axis=0),
                           jnp.concatenate([v[:BLOCK], v[:BLOCK]], axis=0), band_first)
                else:
                    attend(g, rows, qb, k[(n - 1) * BLOCK:(n + 1) * BLOCK],
                           v[(n - 1) * BLOCK:(n + 1) * BLOCK], band)

    def merge(i, carry):
        rows = pl.ds(pl.multiple_of(i * 2 * BLOCK, 2 * BLOCK), 2 * BLOCK)
        ms = [m_ref[g, rows, :] for g in range(N_GROUPS_A)]
        mx = jnp.maximum(jnp.maximum(ms[0], ms[1]), ms[2])
        num = jnp.zeros((2 * BLOCK, LANES), F32)
        den = jnp.zeros((2 * BLOCK, LANES), F32)
        for g in range(N_GROUPS_A):
            w = jnp.exp(ms[g] - mx)
            num = num + w * acc_ref[g, rows, :]
            den = den + w * l_ref[g, rows, :]
        o_ref[0, rows, :] = (num / den).astype(o_ref.dtype)
        return carry

    lax.fori_loop(0, seq // (2 * BLOCK), merge, 0)


def _mixer_a(proj_a, rev_a):
    B, S, _ = proj_a.shape
    npair = OUT_WIDTH_A // LANES

    def col(which, g):
        base = (which * WIDTH_A + g * OUT_WIDTH_A) // LANES
        return pl.BlockSpec((1, S, LANES), lambda b, hp: (b, 0, base + hp))

    in_specs = [col(which, g) for g in range(N_GROUPS_A) for which in range(3)]
    in_specs.append(pl.BlockSpec((1, 2 * N_GROUPS_A, 4 * BLOCK), lambda b, hp: (hp, 0, 0)))
    return pl.pallas_call(
        functools.partial(_mixer_a_kernel, seq=S),
        grid=(B, npair),
        in_specs=in_specs,
        out_specs=pl.BlockSpec((1, S, LANES), lambda b, hp: (b, 0, hp)),
        out_shape=jax.ShapeDtypeStruct((B, S, OUT_WIDTH_A), BF16),
        scratch_shapes=[pltpu.VMEM((N_GROUPS_A, 2 * BLOCK, 2 * BLOCK), F32),
                        pltpu.VMEM((N_GROUPS_A, S, LANES), F32),
                        pltpu.VMEM((N_GROUPS_A, S, LANES), F32),
                        pltpu.VMEM((N_GROUPS_A, S, LANES), F32)],
        compiler_params=_params(("parallel", "parallel")),
        name="mixer_a",
    )(*([proj_a] * 9), rev_a)


def _mixer_b_kernel(q_ref, k_ref, v_ref, rev_ref, lam_ref, g_ref, o_ref,
                    toep_ref, s_ref, p_ref, l_ref, *, tq, seq, lam_init):
    nq = seq // tq
    rg_rows = 16
    n_rg = 2 * tq // rg_rows

    x = jnp.broadcast_to(rev_ref[0] * LOG2E, (tq, seq + tq))
    rolled = pltpu.roll(x, 0, 1, stride=1, stride_axis=0)
    for c in range(nq + 1):
        toep_ref[c] = rolled[:, c * tq:(c + 1) * tq]

    lane = lax.broadcasted_iota(jnp.int32, (tq, LANES), 1)
    lo = lane < HEAD_DIM
    row = lax.broadcasted_iota(jnp.int32, (2 * tq, tq), 0)
    col = lax.broadcasted_iota(jnp.int32, (2 * tq, tq), 1)
    causal = col <= (row & (tq - 1))
    lv = lam_ref[...]
    lam = (jnp.exp(jnp.sum(lv[0:1] * lv[1:2], axis=-1, keepdims=True))
           - jnp.exp(jnp.sum(lv[2:3] * lv[3:4], axis=-1, keepdims=True)) + lam_init)

    def score_chunks(qi):
        slot = qi % 2
        q = q_ref[0, qi * tq:(qi + 1) * tq, :]
        zero = jnp.zeros_like(q)
        qz = jnp.concatenate([jnp.where(lo, q, zero), jnp.where(lo, zero, q)], axis=0)

        def chunk(c):
            kc = k_ref[0, c * tq:(c + 1) * tq, :]
            s = lax.dot_general(qz, kc, (((1,), (1,)), ((), ())), preferred_element_type=F32)
            bias = toep_ref[nq - qi + c]
            s = s * (SCALE * LOG2E) + jnp.concatenate([bias, bias], axis=0)
            if c == qi:
                s = jnp.where(causal, s, NEG_INF)
            s_ref[slot, :, c * tq:(c + 1) * tq] = s

        return [functools.partial(chunk, c) for c in range(qi + 1)]

    def softmax_group(qi, rg):
        slot, width = qi % 2, (qi + 1) * tq
        rows = slice(rg * rg_rows, (rg + 1) * rg_rows)
        t = s_ref[slot, rows, :width]
        m = jnp.max(t, axis=-1, keepdims=True)
        p = jnp.exp2(t - m)
        l_ref[slot, rows, :] = jnp.broadcast_to(jnp.sum(p, axis=-1, keepdims=True), (rg_rows, LANES))
        p_ref[slot, rows, :width] = p.astype(BF16)

    def finish(qi):
        slot, width = qi % 2, (qi + 1) * tq
        acc = jnp.dot(p_ref[slot, :, :width], v_ref[0, :width, :], preferred_element_type=F32)
        o = acc / l_ref[slot]
        y = o[:tq] - lam * o[tq:]
        y = _rms(y, g_ref[...]) * (1.0 - lam_init)
        o_ref[0, qi * tq:(qi + 1) * tq, :] = y.astype(o_ref.dtype)

    for chunk in score_chunks(0):
        chunk()
    for qi in range(nq):
        chunks = score_chunks(qi + 1) if qi + 1 < nq else []
        per = -(-n_rg // (len(chunks) + 1))
        rg = 0
        for chunk in chunks:
            chunk()
            for _ in range(per):
                if rg < n_rg:
                    softmax_group(qi, rg)
                    rg += 1
        while rg < n_rg:
            softmax_group(qi, rg)
            rg += 1
        finish(qi)


def _mixer_b(proj, rev_b, lam_vecs, subln_g, lam_init, tq=256):
    B, S, _ = proj.shape
    H = N_HEADS_B
    nq = S // tq
    kern = functools.partial(_mixer_b_kernel, tq=tq, seq=S, lam_init=lam_init)

    def col(off):
        return pl.BlockSpec((1, S, LANES), lambda b, h: (b, 0, off // LANES + h))

    return pl.pallas_call(
        kern,
        grid=(B, H),
        in_specs=[col(OFF_QB), col(OFF_KB), col(OFF_VB),
                  pl.BlockSpec((1, 1, S + tq), lambda b, h: (h, 0, 0)),
                  pl.BlockSpec((4, HEAD_DIM), lambda b, h: (0, 0)),
                  pl.BlockSpec((1, 2 * HEAD_DIM), lambda b, h: (0, 0))],
        out_specs=pl.BlockSpec((1, S, LANES), lambda b, h: (b, 0, h)),
        out_shape=jax.ShapeDtypeStruct((B, S, WIDTH_B), BF16),
        scratch_shapes=[pltpu.VMEM((nq + 1, tq, tq), F32),
                        pltpu.VMEM((2, 2 * tq, S), F32),
                        pltpu.VMEM((2, 2 * tq, S), BF16),
                        pltpu.VMEM((2, 2 * tq, LANES), F32)],
        compiler_params=_params(("parallel", "parallel")),
        name="mixer_b",
    )(proj, proj, proj, rev_b, lam_vecs, subln_g)


def _gate_merge_kernel(ya_ref, yb_ref, ga_ref, gb_ref, wa_ref, wb_ref, out_ref):
    pa = jnp.dot(ya_ref[...], wa_ref[...].astype(BF16), preferred_element_type=F32)
    pb = jnp.dot(yb_ref[...], wb_ref[...].astype(BF16), preferred_element_type=F32)
    ga = jax.nn.sigmoid(ga_ref[...].astype(F32))
    gb = jax.nn.sigmoid(gb_ref[...].astype(F32))
    out_ref[...] = (ga * pa + gb * pb).astype(out_ref.dtype)


def _gate_merge(ya, yb, proj_r, wa, wb, tm=1024, tn=512):
    T = yb.shape[0]
    D = D_MODEL
    return pl.pallas_call(
        _gate_merge_kernel,
        grid=(T // tm, D // tn),
        in_specs=[pl.BlockSpec((tm, OUT_WIDTH_A), lambda i, j: (i, 0)),
                  pl.BlockSpec((tm, WIDTH_B), lambda i, j: (i, 0)),
                  pl.BlockSpec((tm, tn), lambda i, j: (i, OFF_GA // tn + j)),
                  pl.BlockSpec((tm, tn), lambda i, j: (i, OFF_GB // tn + j)),
                  pl.BlockSpec((OUT_WIDTH_A, tn), lambda i, j: (0, j)),
                  pl.BlockSpec((WIDTH_B, tn), lambda i, j: (0, j))],
        out_specs=pl.BlockSpec((tm, tn), lambda i, j: (i, j)),
        out_shape=jax.ShapeDtypeStruct((T, D), BF16),
        compiler_params=_params(("parallel", "arbitrary")),
        name="gate_merge",
    )(ya, yb, proj_r, proj_r, wa, wb)


def _out_proj_kernel(m_ref, w_ref, x_ref, o_ref):
    o_ref[...] = x_ref[...] + jnp.dot(m_ref[...], w_ref[...].astype(BF16), preferred_element_type=F32)


def _out_proj(merged, w, x2, tm=1024, tn=512):
    T, D = x2.shape
    return pl.pallas_call(
        _out_proj_kernel,
        grid=(T // tm, D // tn),
        in_specs=[pl.BlockSpec((tm, D), lambda i, j: (i, 0)),
                  pl.BlockSpec((D, tn), lambda i, j: (0, j)),
                  pl.BlockSpec((tm, tn), lambda i, j: (i, j))],
        out_specs=pl.BlockSpec((tm, tn), lambda i, j: (i, j)),
        out_shape=jax.ShapeDtypeStruct((T, D), F32),
        compiler_params=_params(("parallel", "arbitrary")),
        name="out_proj",
    )(merged, w, x2)


def _ffn_kernel(x_ref, g_ref, wg_ref, wu_ref, wd_ref, gf_ref, o_ref, h_ref):
    f = pl.program_id(1)

    @pl.when(f == 0)
    def _():
        h_ref[...] = _rms(x_ref[...], g_ref[...]).astype(BF16)
        o_ref[...] = x_ref[...]

    h = h_ref[...]
    a = jnp.dot(h, wg_ref[...].astype(BF16), preferred_element_type=F32)
    b = jnp.dot(h, wu_ref[...].astype(BF16), preferred_element_type=F32)
    u = (a * jax.nn.sigmoid(a)) * b
    o_ref[...] += jnp.dot(u.astype(BF16), wd_ref[...].astype(BF16), preferred_element_type=F32)

    @pl.when(f == pl.num_programs(1) - 1)
    def _():
        o_ref[...] = _rms(o_ref[...], gf_ref[...])


def _ffn(x1, g, wg, wu, wd, gf, tm=1024, tf=256):
    T, D = x1.shape
    F = wg.shape[1]
    return pl.pallas_call(
        _ffn_kernel,
        grid=(T // tm, F // tf),
        in_specs=[pl.BlockSpec((tm, D), lambda i, f: (i, 0), pipeline_mode=pl.Buffered(1)),
                  pl.BlockSpec((1, D), lambda i, f: (0, 0)),
                  pl.BlockSpec((D, tf), lambda i, f: (0, f)),
                  pl.BlockSpec((D, tf), lambda i, f: (0, f)),
                  pl.BlockSpec((tf, D), lambda i, f: (f, 0)),
                  pl.BlockSpec((1, D), lambda i, f: (0, 0))],
        out_specs=pl.BlockSpec((tm, D), lambda i, f: (i, 0)),
        out_shape=jax.ShapeDtypeStruct((T, D), F32),
        scratch_shapes=[pltpu.VMEM((tm, D), BF16)],
        compiler_params=_params(("parallel", "arbitrary")),
        name="ffn",
    )(x1, g, wg, wu, wd, gf)


def _rev_a_index():
    u = np.arange(4 * BLOCK)
    rel = np.clip(2 * BLOCK - u, 0, None)
    return np.stack([_rel_bucket_np(rel * d) for _, d in DIL_PATTERNS])


def _rev_b_index(seq, tq):
    c = np.arange(seq + tq)
    return _rel_bucket_np(np.clip(seq - c, 0, seq - 1))


def _lookup(table, idx):
    idx = jnp.asarray(idx)[None, :]
    out = jnp.zeros((table.shape[1], idx.shape[1]), F32)
    for b in range(NUM_BUCKETS):
        out = jnp.where(idx == b, table[b][:, None], out)
    return out


def kernel(x, norm_attn_g, w_in, w_proj_a, w_proj_b, w_out, rel_bias_table, diff_lambda_q1, diff_lambda_k1, diff_lambda_q2, diff_lambda_k2, diff_subln_g, norm_ffn_g, w_ffn_gate, w_ffn_up, w_ffn_down, norm_final_g):
    B, S, D = x.shape
    T = B * S
    depth = w_in.shape[0]
    assert depth == 1, "the final RMSNorm is fused into the FFN epilogue of a single layer"
    table_a = rel_bias_table[:, :N_HEADS_A].astype(F32)
    table_b = rel_bias_table[:, N_HEADS_A:].astype(F32)
    tq = 256

    idx_a = _rev_a_index()
    rev_a = jnp.stack([_lookup(table_a[:, g * HEADS_PER_GROUP_A:(g + 1) * HEADS_PER_GROUP_A], idx_a[g])
                       for g in range(N_GROUPS_A)])
    npair = OUT_WIDTH_A // LANES
    rev_a = jnp.transpose(rev_a.reshape(N_GROUPS_A, npair, 2, 4 * BLOCK), (1, 0, 2, 3))
    rev_a = rev_a.reshape(npair, 2 * N_GROUPS_A, 4 * BLOCK)
    rev_b = _lookup(table_b, _rev_b_index(S, tq))[:, None, :]

    x2 = x.reshape(T, D)
    l = 0
    lam_init = 0.8 - 0.6 * math.exp(-0.3 * l)
    proj_a, proj_r = _in_proj(x2, norm_attn_g[l][None, :], w_in[l])
    ya = _mixer_a(proj_a.reshape(B, S, PROJ_A), rev_a)

    lam_vecs = jnp.stack([diff_lambda_q1[l], diff_lambda_k1[l],
                          diff_lambda_q2[l], diff_lambda_k2[l]]).astype(F32)
    yb = _mixer_b(proj_r.reshape(B, S, PROJ_R), rev_b, lam_vecs, diff_subln_g[l][None, :].astype(F32),
                  lam_init, tq=tq)

    merged = _gate_merge(ya.reshape(T, OUT_WIDTH_A), yb.reshape(T, WIDTH_B), proj_r, w_proj_a[l], w_proj_b[l])
    x2 = _out_proj(merged, w_out[l], x2)
    x2 = _ffn(x2, norm_ffn_g[l][None, :], w_ffn_gate[l], w_ffn_up[l], w_ffn_down[l], norm_final_g[None, :])
    return x2.reshape(B, S, D)
```

```python
import functools
import math

import numpy as np
import jax
import jax.numpy as jnp
from jax import lax
from jax.experimental import pallas as pl
from jax.experimental.pallas import tpu as pltpu

D_MODEL = 2048
HEAD_DIM = 64
DIL_PATTERNS = ((128, 1), (512, 4), (2048, 16))
N_GROUPS_A = len(DIL_PATTERNS)
HEADS_PER_GROUP_A = 8
N_HEADS_A = N_GROUPS_A * HEADS_PER_GROUP_A
WIDTH_A = N_HEADS_A * HEAD_DIM
OUT_WIDTH_A = HEADS_PER_GROUP_A * HEAD_DIM
BLOCK = 128
N_HEADS_B = D_MODEL // (2 * HEAD_DIM)
WIDTH_B = N_HEADS_B * 2 * HEAD_DIM
NUM_BUCKETS = 32
MAX_DISTANCE = 2048
D_FF = -(-8 * D_MODEL // (3 * 256)) * 256
PROJ_A = 3 * WIDTH_A
PROJ_R = 3 * WIDTH_B + 2 * D_MODEL
D_IN = PROJ_A + PROJ_R
NORM_EPS = 1e-6
NEG_INF = -1e30
SCALE = HEAD_DIM ** -0.5
LOG2E = math.log2(math.e)

OFF_QB = 0
OFF_KB = WIDTH_B
OFF_VB = 2 * WIDTH_B
OFF_GA = 3 * WIDTH_B
OFF_GB = OFF_GA + D_MODEL

LANES = 128
VMEM_LIMIT = 56 * 1024 * 1024

BF16 = jnp.bfloat16
F32 = jnp.float32


def _rel_bucket_np(dist):
    n = np.maximum(dist, 0)
    max_exact = NUM_BUCKETS // 2
    nf = np.maximum(n, 1).astype(np.float32)
    large = max_exact + (np.log(nf / np.float32(max_exact)) / np.float32(math.log(MAX_DISTANCE / max_exact))
                         * np.float32(NUM_BUCKETS - max_exact)).astype(np.int32)
    large = np.minimum(large, NUM_BUCKETS - 1)
    return np.where(n < max_exact, n, large).astype(np.int32)


def _rms(x, g):
    ms = jnp.mean(x * x, axis=-1, keepdims=True)
    return x * lax.rsqrt(ms + NORM_EPS) * g


def _params(sem, vmem=VMEM_LIMIT):
    return pltpu.CompilerParams(dimension_semantics=sem, vmem_limit_bytes=vmem)


def _in_proj_kernel(x_ref, g_ref, w_ref, oa_ref, or_ref, h_ref, *, na):
    j = pl.program_id(1)

    @pl.when(j == 0)
    def _():
        h_ref[...] = _rms(x_ref[...], g_ref[...]).astype(BF16)

    @pl.when(j < na)
    def _():
        oa_ref[...] = jnp.dot(h_ref[...], w_ref[...].astype(BF16), preferred_element_type=F32)

    @pl.when(j >= na)
    def _():
        or_ref[...] = jnp.dot(h_ref[...], w_ref[...].astype(BF16), preferred_element_type=F32).astype(BF16)


def _in_proj(x2, g, w, tm=1024, tn=512):
    T, D = x2.shape
    na = PROJ_A // tn
    return pl.pallas_call(
        functools.partial(_in_proj_kernel, na=na),
        grid=(T // tm, D_IN // tn),
        in_specs=[pl.BlockSpec((tm, D), lambda i, j: (i, 0)),
                  pl.BlockSpec((1, D), lambda i, j: (0, 0)),
                  pl.BlockSpec((D, tn), lambda i, j: (0, j))],
        out_specs=[pl.BlockSpec((tm, tn), lambda i, j: (i, jnp.minimum(j, na - 1))),
                   pl.BlockSpec((tm, tn), lambda i, j: (i, jnp.maximum(j - na, 0)))],
        out_shape=[jax.ShapeDtypeStruct((T, PROJ_A), F32), jax.ShapeDtypeStruct((T, PROJ_R), BF16)],
        scratch_shapes=[pltpu.VMEM((tm, D), BF16)],
        compiler_params=_params(("parallel", "arbitrary")),
        name="in_proj",
    )(x2, g, w)


def _mixer_a_kernel(q0_ref, k0_ref, v0_ref, q1_ref, k1_ref, v1_ref, q2_ref, k2_ref, v2_ref, rev_ref,
                    o_ref, bias_ref, m_ref, l_ref, acc_ref, *, seq):
    qkv = ((q0_ref, k0_ref, v0_ref), (q1_ref, k1_ref, v1_ref), (q2_ref, k2_ref, v2_ref))
    lane = lax.broadcasted_iota(jnp.int32, (BLOCK, LANES), 1)
    lo = lane < HEAD_DIM

    for g in range(N_GROUPS_A):
        for hh in range(2):
            x = jnp.broadcast_to(rev_ref[0, g * 2 + hh:g * 2 + hh + 1, :], (BLOCK, 4 * BLOCK))
            rolled = pltpu.roll(x, 0, 1, stride=1, stride_axis=0)
            bias_ref[g, hh * BLOCK:(hh + 1) * BLOCK, :] = rolled[:, BLOCK:3 * BLOCK]

    row = lax.broadcasted_iota(jnp.int32, (2 * BLOCK, 2 * BLOCK), 0)
    col = lax.broadcasted_iota(jnp.int32, (2 * BLOCK, 2 * BLOCK), 1)
    rel = BLOCK + (row & (BLOCK - 1)) - col
    band = (rel >= 0) & (rel <= BLOCK)
    band_first = band & (col >= BLOCK)

    def attend(g, rows, q, kw, vw, valid):
        zero = jnp.zeros_like(q)
        qz = jnp.concatenate([jnp.where(lo, q, zero), jnp.where(lo, zero, q)], axis=0)
        s = lax.dot_general(qz, kw, (((1,), (1,)), ((), ())), preferred_element_type=F32) * SCALE
        s = jnp.where(valid, s + bias_ref[g], NEG_INF)
        m = jnp.max(s, axis=-1, keepdims=True)
        p = jnp.exp(s - m)
        den = jnp.sum(p, axis=-1, keepdims=True)
        acc = jnp.dot(p.astype(BF16), vw, preferred_element_type=F32)
        mb = jnp.broadcast_to(m, (2 * BLOCK, LANES))
        lb = jnp.broadcast_to(den, (2 * BLOCK, LANES))
        m_ref[g, rows, :] = jnp.where(lo, mb[:BLOCK], mb[BLOCK:])
        l_ref[g, rows, :] = jnp.where(lo, lb[:BLOCK], lb[BLOCK:])
        acc_ref[g, rows, :] = jnp.where(lo, acc[:BLOCK], acc[BLOCK:])

    q_ref, k_ref, v_ref = qkv[0]
    first = pl.ds(0, BLOCK)
    k_first = k_ref[0, first, :].astype(BF16)
    v_first = v_ref[0, first, :].astype(BF16)
    attend(0, first, q_ref[0, first, :].astype(BF16),
           jnp.concatenate([k_first, k_first], axis=0), jnp.concatenate([v_first, v_first], axis=0), band_first)

    for n in range(1, seq // BLOCK):
        rows = pl.ds(n * BLOCK, BLOCK)
        win = pl.ds((n - 1) * BLOCK, 2 * BLOCK)
        attend(0, rows, q_ref[0, rows, :].astype(BF16), k_ref[0, win, :].astype(BF16),
               v_ref[0, win, :].astype(BF16), band)

    for g in range(1, N_GROUPS_A):
        d = DIL_PATTERNS[g][1]
        sub_len = seq // d
        q_ref, k_ref, v_ref = qkv[g]
        for r in range(d):
            sub = pl.ds(r, sub_len, stride=d)
            q = q_ref[0, sub, :].astype(BF16)
            k = k_ref[0, sub, :].astype(BF16)
            v = v_ref[0, sub, :].astype(BF16)
            for n in range(sub_len // BLOCK):
                rows = pl.ds(n * BLOCK * d + r, BLOCK, stride=d)
                qb = q[n * BLOCK:(n + 1) * BLOCK]
                if n == 0:
                    attend(g, rows, qb, jnp.concatenate([k[:BLOCK], k[:BLOCK]], axis=0),
                           jnp.concatenate([v[:BLOCK], v[:BLOCK]], axis=0), band_first)
                else:
                    attend(g, rows, qb, k[(n - 1) * BLOCK:(n + 1) * BLOCK],
                           v[(n - 1) * BLOCK:(n + 1) * BLOCK], band)

    def merge(i, carry):
        rows = pl.ds(pl.multiple_of(i * 2 * BLOCK, 2 * BLOCK), 2 * BLOCK)
        ms = [m_ref[g, rows, :] for g in range(N_GROUPS_A)]
        mx = jnp.maximum(jnp.maximum(ms[0], ms[1]), ms[2])
        num = jnp.zeros((2 * BLOCK, LANES), F32)
        den = jnp.zeros((2 * BLOCK, LANES), F32)
        for g in range(N_GROUPS_A):
            w = jnp.exp(ms[g] - mx)
            num = num + w * acc_ref[g, rows, :]
            den = den + w * l_ref[g, rows, :]
        o_ref[0, rows, :] = (num / den).astype(o_ref.dtype)
        return carry

    lax.fori_loop(0, seq // (2 * BLOCK), merge, 0)


def _mixer_a(proj_a, rev_a):
    B, S, _ = proj_a.shape
    npair = OUT_WIDTH_A // LANES

    def col(which, g):
        base = (which * WIDTH_A + g * OUT_WIDTH_A) // LANES
        return pl.BlockSpec((1, S, LANES), lambda b, hp: (b, 0, base + hp))

    in_specs = [col(which, g) for g in range(N_GROUPS_A) for which in range(3)]
    in_specs.append(pl.BlockSpec((1, 2 * N_GROUPS_A, 4 * BLOCK), lambda b, hp: (hp, 0, 0)))
    return pl.pallas_call(
        functools.partial(_mixer_a_kernel, seq=S),
        grid=(B, npair),
        in_specs=in_specs,
        out_specs=pl.BlockSpec((1, S, LANES), lambda b, hp: (b, 0, hp)),
        out_shape=jax.ShapeDtypeStruct((B, S, OUT_WIDTH_A), BF16),
        scratch_shapes=[pltpu.VMEM((N_GROUPS_A, 2 * BLOCK, 2 * BLOCK), F32),
                        pltpu.VMEM((N_GROUPS_A, S, LANES), F32),
                        pltpu.VMEM((N_GROUPS_A, S, LANES), F32),
                        pltpu.VMEM((N_GROUPS_A, S, LANES), F32)],
        compiler_params=_params(("parallel", "parallel")),
        name="mixer_a",
    )(*([proj_a] * 9), rev_a)


def _mixer_b_kernel(q_ref, k_ref, v_ref, rev_ref, lam_ref, g_ref, o_ref,
                    toep_ref, s_ref, p_ref, vx_ref, *, tq, seq, lam_init):
    nq = seq // tq
    rg_rows = 16
    n_rg = 2 * tq // rg_rows

    x = jnp.broadcast_to(rev_ref[0] * LOG2E, (tq, seq + tq))
    rolled = pltpu.roll(x, 0, 1, stride=1, stride_axis=0)
    for c in range(nq + 1):
        toep_ref[c] = rolled[:, c * tq:(c + 1) * tq]

    vx_ref[:, :LANES] = v_ref[0]
    vx_ref[:, LANES:] = jnp.ones((seq, LANES), BF16)

    lane = lax.broadcasted_iota(jnp.int32, (tq, LANES), 1)
    lo = lane < HEAD_DIM
    row = lax.broadcasted_iota(jnp.int32, (2 * tq, tq), 0)
    col = lax.broadcasted_iota(jnp.int32, (2 * tq, tq), 1)
    causal = col <= (row & (tq - 1))
    lv = lam_ref[...]
    lam = (jnp.exp(jnp.sum(lv[0:1] * lv[1:2], axis=-1, keepdims=True))
           - jnp.exp(jnp.sum(lv[2:3] * lv[3:4], axis=-1, keepdims=True)) + lam_init)

    def score_chunks(qi):
        slot = qi % 2
        q = q_ref[0, qi * tq:(qi + 1) * tq, :]
        zero = jnp.zeros_like(q)
        qz = jnp.concatenate([jnp.where(lo, q, zero), jnp.where(lo, zero, q)], axis=0)

        def chunk(c):
            kc = k_ref[0, c * tq:(c + 1) * tq, :]
            s = lax.dot_general(qz, kc, (((1,), (1,)), ((), ())), preferred_element_type=F32)
            bias = toep_ref[nq - qi + c]
            s = s * (SCALE * LOG2E) + jnp.concatenate([bias, bias], axis=0)
            if c == qi:
                s = jnp.where(causal, s, NEG_INF)
            s_ref[slot, :, c * tq:(c + 1) * tq] = s

        return [functools.partial(chunk, c) for c in range(qi + 1)]

    def softmax_group(qi, rg):
        slot, width = qi % 2, (qi + 1) * tq
        rows = slice(rg * rg_rows, (rg + 1) * rg_rows)
        m = jnp.max(s_ref[slot, rows, :width], axis=-1, keepdims=True)
        p_ref[slot, rows, :width] = jnp.exp2(s_ref[slot, rows, :width] - m).astype(BF16)

    def finish(qi):
        slot, width = qi % 2, (qi + 1) * tq
        acc = jnp.dot(p_ref[slot, :, :width], vx_ref[:width, :], preferred_element_type=F32)
        o = acc[:, :LANES] / acc[:, LANES:]
        y = o[:tq] - lam * o[tq:]
        y = _rms(y, g_ref[...]) * (1.0 - lam_init)
        o_ref[0, qi * tq:(qi + 1) * tq, :] = y.astype(o_ref.dtype)

    for chunk in score_chunks(0):
        chunk()
    for qi in range(nq):
        chunks = score_chunks(qi + 1) if qi + 1 < nq else []
        per = -(-n_rg // (len(chunks) + 1))
        rg = 0
        for chunk in chunks:
            chunk()
            for _ in range(per):
                if rg < n_rg:
                    softmax_group(qi, rg)
                    rg += 1
        while rg < n_rg:
            softmax_group(qi, rg)
            rg += 1
        finish(qi)


def _mixer_b(proj, rev_b, lam_vecs, subln_g, lam_init, tq=256):
    B, S, _ = proj.shape
    H = N_HEADS_B
    nq = S // tq
    kern = functools.partial(_mixer_b_kernel, tq=tq, seq=S, lam_init=lam_init)

    def col(off):
        return pl.BlockSpec((1, S, LANES), lambda b, h: (b, 0, off // LANES + h))

    return pl.pallas_call(
        kern,
        grid=(B, H),
        in_specs=[col(OFF_QB), col(OFF_KB), col(OFF_VB),
                  pl.BlockSpec((1, 1, S + tq), lambda b, h: (h, 0, 0)),
                  pl.BlockSpec((4, HEAD_DIM), lambda b, h: (0, 0)),
                  pl.BlockSpec((1, 2 * HEAD_DIM), lambda b, h: (0, 0))],
        out_specs=pl.BlockSpec((1, S, LANES), lambda b, h: (b, 0, h)),
        out_shape=jax.ShapeDtypeStruct((B, S, WIDTH_B), BF16),
        scratch_shapes=[pltpu.VMEM((nq + 1, tq, tq), F32),
                        pltpu.VMEM((2, 2 * tq, S), F32),
                        pltpu.VMEM((2, 2 * tq, S), BF16),
                        pltpu.VMEM((S, 2 * LANES), BF16)],
        compiler_params=_params(("parallel", "parallel")),
        name="mixer_b",
    )(proj, proj, proj, rev_b, lam_vecs, subln_g)


def _gate_merge_kernel(ya_ref, yb_ref, ga_ref, gb_ref, wa_ref, wb_ref, out_ref):
    pa = jnp.dot(ya_ref[...], wa_ref[...].astype(BF16), preferred_element_type=F32)
    pb = jnp.dot(yb_ref[...], wb_ref[...].astype(BF16), preferred_element_type=F32)
    ga = jax.nn.sigmoid(ga_ref[...].astype(F32))
    gb = jax.nn.sigmoid(gb_ref[...].astype(F32))
    out_ref[...] = (ga * pa + gb * pb).astype(out_ref.dtype)


def _gate_merge(ya, yb, proj_r, wa, wb, tm=1024, tn=512):
    T = yb.shape[0]
    D = D_MODEL
    return pl.pallas_call(
        _gate_merge_kernel,
        grid=(T // tm, D // tn),
        in_specs=[pl.BlockSpec((tm, OUT_WIDTH_A), lambda i, j: (i, 0)),
                  pl.BlockSpec((tm, WIDTH_B), lambda i, j: (i, 0)),
                  pl.BlockSpec((tm, tn), lambda i, j: (i, OFF_GA // tn + j)),
                  pl.BlockSpec((tm, tn), lambda i, j: (i, OFF_GB // tn + j)),
                  pl.BlockSpec((OUT_WIDTH_A, tn), lambda i, j: (0, j)),
                  pl.BlockSpec((WIDTH_B, tn), lambda i, j: (0, j))],
        out_specs=pl.BlockSpec((tm, tn), lambda i, j: (i, j)),
        out_shape=jax.ShapeDtypeStruct((T, D), BF16),
        compiler_params=_params(("parallel", "arbitrary")),
        name="gate_merge",
    )(ya, yb, proj_r, proj_r, wa, wb)


def _out_proj_kernel(m_ref, w_ref, x_ref, o_ref):
    o_ref[...] = x_ref[...] + jnp.dot(m_ref[...], w_ref[...].astype(BF16), preferred_element_type=F32)


def _out_proj(merged, w, x2, tm=1024, tn=512):
    T, D = x2.shape
    return pl.pallas_call(
        _out_proj_kernel,
        grid=(T // tm, D // tn),
        in_specs=[pl.BlockSpec((tm, D), lambda i, j: (i, 0)),
                  pl.BlockSpec((D, tn), lambda i, j: (0, j)),
                  pl.BlockSpec((tm, tn), lambda i, j: (i, j))],
        out_specs=pl.BlockSpec((tm, tn), lambda i, j: (i, j)),
        out_shape=jax.ShapeDtypeStruct((T, D), F32),
        compiler_params=_params(("parallel", "arbitrary")),
        name="out_proj",
    )(merged, w, x2)


def _ffn_kernel(x_ref, g_ref, wg_ref, wu_ref, wd_ref, gf_ref, o_ref, h_ref):
    f = pl.program_id(1)

    @pl.when(f == 0)
    def _():
        h_ref[...] = _rms(x_ref[...], g_ref[...]).astype(BF16)
        o_ref[...] = x_ref[...]

    h = h_ref[...]
    a = jnp.dot(h, wg_ref[...].astype(BF16), preferred_element_type=F32)
    b = jnp.dot(h, wu_ref[...].astype(BF16), preferred_element_type=F32)
    u = (a * jax.nn.sigmoid(a)) * b
    o_ref[...] += jnp.dot(u.astype(BF16), wd_ref[...].astype(BF16), preferred_element_type=F32)

    @pl.when(f == pl.num_programs(1) - 1)
    def _():
        o_ref[...] = _rms(o_ref[...], gf_ref[...])


def _ffn(x1, g, wg, wu, wd, gf, tm=1024, tf=256):
    T, D = x1.shape
    F = wg.shape[1]
    return pl.pallas_call(
        _ffn_kernel,
        grid=(T // tm, F // tf),
        in_specs=[pl.BlockSpec((tm, D), lambda i, f: (i, 0), pipeline_mode=pl.Buffered(1)),
                  pl.BlockSpec((1, D), lambda i, f: (0, 0)),
                  pl.BlockSpec((D, tf), lambda i, f: (0, f)),
                  pl.BlockSpec((D, tf), lambda i, f: (0, f)),
                  pl.BlockSpec((tf, D), lambda i, f: (f, 0)),
                  pl.BlockSpec((1, D), lambda i, f: (0, 0))],
        out_specs=pl.BlockSpec((tm, D), lambda i, f: (i, 0)),
        out_shape=jax.ShapeDtypeStruct((T, D), F32),
        scratch_shapes=[pltpu.VMEM((tm, D), BF16)],
        compiler_params=_params(("parallel", "arbitrary")),
        name="ffn",
    )(x1, g, wg, wu, wd, gf)


def _rev_a_index():
    u = np.arange(4 * BLOCK)
    rel = np.clip(2 * BLOCK - u, 0, None)
    return np.stack([_rel_bucket_np(rel * d) for _, d in DIL_PATTERNS])


def _rev_b_index(seq, tq):
    c = np.arange(seq + tq)
    return _rel_bucket_np(np.clip(seq - c, 0, seq - 1))


def _lookup(table, idx):
    idx = jnp.asarray(idx)[None, :]
    out = jnp.zeros((table.shape[1], idx.shape[1]), F32)
    for b in range(NUM_BUCKETS):
        out = jnp.where(idx == b, table[b][:, None], out)
    return out


def kernel(x, norm_attn_g, w_in, w_proj_a, w_proj_b, w_out, rel_bias_table, diff_lambda_q1, diff_lambda_k1, diff_lambda_q2, diff_lambda_k2, diff_subln_g, norm_ffn_g, w_ffn_gate, w_ffn_up, w_ffn_down, norm_final_g):
    B, S, D = x.shape
    T = B * S
    depth = w_in.shape[0]
    assert depth == 1, "the final RMSNorm is fused into the FFN epilogue of a single layer"
    table_a = rel_bias_table[:, :N_HEADS_A].astype(F32)
    table_b = rel_bias_table[:, N_HEADS_A:].astype(F32)
    tq = 256

    idx_a = _rev_a_index()
    rev_a = jnp.stack([_lookup(table_a[:, g * HEADS_PER_GROUP_A:(g + 1) * HEADS_PER_GROUP_A], idx_a[g])
                       for g in range(N_GROUPS_A)])
    npair = OUT_WIDTH_A // LANES
    rev_a = jnp.transpose(rev_a.reshape(N_GROUPS_A, npair, 2, 4 * BLOCK), (1, 0, 2, 3))
    rev_a = rev_a.reshape(npair, 2 * N_GROUPS_A, 4 * BLOCK)
    rev_b = _lookup(table_b, _rev_b_index(S, tq))[:, None, :]

    x2 = x.reshape(T, D)
    l = 0
    lam_init = 0.8 - 0.6 * math.exp(-0.3 * l)
    proj_a, proj_r = _in_proj(x2, norm_attn_g[l][None, :], w_in[l])
    ya = _mixer_a(proj_a.reshape(B, S, PROJ_A), rev_a)

    lam_vecs = jnp.stack([diff_lambda_q1[l], diff_lambda_k1[l],
                          diff_lambda_q2[l], diff_lambda_k2[l]]).astype(F32)
    yb = _mixer_b(proj_r.reshape(B, S, PROJ_R), rev_b, lam_vecs, diff_subln_g[l][None, :].astype(F32),
                  lam_init, tq=tq)

    merged = _gate_merge(ya.reshape(T, OUT_WIDTH_A), yb.reshape(T, WIDTH_B), proj_r, w_proj_a[l], w_proj_b[l])
    x2 = _out_proj(merged, w_out[l], x2)
    x2 = _ffn(x2, norm_ffn_g[l][None, :], w_ffn_gate[l], w_ffn_up[l], w_ffn_down[l], norm_final_g[None, :])
    return x2.reshape(B, S, D)
```

```python
import functools
import math

import numpy as np
import jax
import jax.numpy as jnp
from jax import lax
from jax.experimental import pallas as pl
from jax.experimental.pallas import tpu as pltpu

D_MODEL = 2048
HEAD_DIM = 64
DIL_PATTERNS = ((128, 1), (512, 4), (2048, 16))
N_GROUPS_A = len(DIL_PATTERNS)
HEADS_PER_GROUP_A = 8
N_HEADS_A = N_GROUPS_A * HEADS_PER_GROUP_A
WIDTH_A = N_HEADS_A * HEAD_DIM
OUT_WIDTH_A = HEADS_PER_GROUP_A * HEAD_DIM
BLOCK = 128
N_HEADS_B = D_MODEL // (2 * HEAD_DIM)
WIDTH_B = N_HEADS_B * 2 * HEAD_DIM
NUM_BUCKETS = 32
MAX_DISTANCE = 2048
D_FF = -(-8 * D_MODEL // (3 * 256)) * 256
PROJ_A = 3 * WIDTH_A
PROJ_R = 3 * WIDTH_B + 2 * D_MODEL
D_IN = PROJ_A + PROJ_R
NORM_EPS = 1e-6
NEG_INF = -1e30
SCALE = HEAD_DIM ** -0.5
LOG2E = math.log2(math.e)

OFF_QB = 0
OFF_KB = WIDTH_B
OFF_VB = 2 * WIDTH_B
OFF_GA = 3 * WIDTH_B
OFF_GB = OFF_GA + D_MODEL

LANES = 128
VMEM_LIMIT = 56 * 1024 * 1024

BF16 = jnp.bfloat16
F32 = jnp.float32


def _rel_bucket_np(dist):
    n = np.maximum(dist, 0)
    max_exact = NUM_BUCKETS // 2
    nf = np.maximum(n, 1).astype(np.float32)
    large = max_exact + (np.log(nf / np.float32(max_exact)) / np.float32(math.log(MAX_DISTANCE / max_exact))
                         * np.float32(NUM_BUCKETS - max_exact)).astype(np.int32)
    large = np.minimum(large, NUM_BUCKETS - 1)
    return np.where(n < max_exact, n, large).astype(np.int32)


def _rms(x, g):
    ms = jnp.mean(x * x, axis=-1, keepdims=True)
    return x * lax.rsqrt(ms + NORM_EPS) * g


def _params(sem, vmem=VMEM_LIMIT):
    return pltpu.CompilerParams(dimension_semantics=sem, vmem_limit_bytes=vmem)


BF16_ROWS = 16


def _cast_specs(weights, grid):
    nsteps = math.prod(grid)
    in_specs, out_specs, out_shapes = [], [], []
    for w in weights:
        rows, cols = w.shape
        blk = next(r for r in range(BF16_ROWS, rows + 1, BF16_ROWS)
                   if rows % r == 0 and nsteps % (rows // r) == 0 and rows // r <= nsteps)
        per = nsteps // (rows // blk)

        def index(*ids, per=per):
            step = 0
            for i, n in zip(ids, grid):
                step = step * n + i
            return (step // per, 0)

        in_specs.append(pl.BlockSpec((blk, cols), index))
        out_specs.append(pl.BlockSpec((blk, cols), index))
        out_shapes.append(jax.ShapeDtypeStruct((rows, cols), BF16))
    return in_specs, out_specs, out_shapes


def _cast_blocks(in_refs, out_refs):
    for i_ref, o_ref in zip(in_refs, out_refs):
        o_ref[...] = i_ref[...].astype(BF16)


def _in_proj_kernel(x_ref, g_ref, w_ref, oa_ref, or_ref, h_ref, *, na):
    j = pl.program_id(1)

    @pl.when(j == 0)
    def _():
        h_ref[...] = _rms(x_ref[...], g_ref[...]).astype(BF16)

    @pl.when(j < na)
    def _():
        oa_ref[...] = jnp.dot(h_ref[...], w_ref[...].astype(BF16), preferred_element_type=F32)

    @pl.when(j >= na)
    def _():
        or_ref[...] = jnp.dot(h_ref[...], w_ref[...].astype(BF16), preferred_element_type=F32).astype(BF16)


def _in_proj(x2, g, w, tm=2048, tn=512):
    T, D = x2.shape
    na = PROJ_A // tn
    return pl.pallas_call(
        functools.partial(_in_proj_kernel, na=na),
        grid=(T // tm, D_IN // tn),
        in_specs=[pl.BlockSpec((tm, D), lambda i, j: (i, 0), pipeline_mode=pl.Buffered(1)),
                  pl.BlockSpec((1, D), lambda i, j: (0, 0)),
                  pl.BlockSpec((D, tn), lambda i, j: (0, j))],
        out_specs=[pl.BlockSpec((tm, tn), lambda i, j: (i, jnp.minimum(j, na - 1))),
                   pl.BlockSpec((tm, tn), lambda i, j: (i, jnp.maximum(j - na, 0)))],
        out_shape=[jax.ShapeDtypeStruct((T, PROJ_A), F32), jax.ShapeDtypeStruct((T, PROJ_R), BF16)],
        scratch_shapes=[pltpu.VMEM((tm, D), BF16)],
        compiler_params=_params(("parallel", "arbitrary")),
        name="in_proj",
    )(x2, g, w)


def _mixer_a_kernel(*refs, seq, n_cast):
    qkv = (refs[0:3], refs[3:6], refs[6:9])
    rev_ref = refs[9]
    o_ref = refs[10 + n_cast]
    bias_ref, m_ref, l_ref, acc_ref = refs[11 + 2 * n_cast:]
    _cast_blocks(refs[10:10 + n_cast], refs[11 + n_cast:11 + 2 * n_cast])
    lane = lax.broadcasted_iota(jnp.int32, (BLOCK, LANES), 1)
    lo = lane < HEAD_DIM

    for g in range(N_GROUPS_A):
        for hh in range(2):
            x = jnp.broadcast_to(rev_ref[0, g * 2 + hh:g * 2 + hh + 1, :], (BLOCK, 4 * BLOCK))
            rolled = pltpu.roll(x, 0, 1, stride=1, stride_axis=0)
            bias_ref[g, hh * BLOCK:(hh + 1) * BLOCK, :] = rolled[:, BLOCK:3 * BLOCK]

    row = lax.broadcasted_iota(jnp.int32, (2 * BLOCK, 2 * BLOCK), 0)
    col = lax.broadcasted_iota(jnp.int32, (2 * BLOCK, 2 * BLOCK), 1)
    rel = BLOCK + (row & (BLOCK - 1)) - col
    band = (rel >= 0) & (rel <= BLOCK)
    band_first = band & (col >= BLOCK)

    def attend(g, rows, q, kw, vw, valid):
        zero = jnp.zeros_like(q)
        qz = jnp.concatenate([jnp.where(lo, q, zero), jnp.where(lo, zero, q)], axis=0)
        s = lax.dot_general(qz, kw, (((1,), (1,)), ((), ())), preferred_element_type=F32) * SCALE
        s = jnp.where(valid, s + bias_ref[g], NEG_INF)
        m = jnp.max(s, axis=-1, keepdims=True)
        p = jnp.exp(s - m)
        den = jnp.sum(p, axis=-1, keepdims=True)
        acc = jnp.dot(p.astype(BF16), vw, preferred_element_type=F32)
        mb = jnp.broadcast_to(m, (2 * BLOCK, LANES))
        lb = jnp.broadcast_to(den, (2 * BLOCK, LANES))
        m_ref[g, rows, :] = jnp.where(lo, mb[:BLOCK], mb[BLOCK:])
        l_ref[g, rows, :] = jnp.where(lo, lb[:BLOCK], lb[BLOCK:])
        acc_ref[g, rows, :] = jnp.where(lo, acc[:BLOCK], acc[BLOCK:])

    q_ref, k_ref, v_ref = qkv[0]
    first = pl.ds(0, BLOCK)
    k_first = k_ref[0, first, :].astype(BF16)
    v_first = v_ref[0, first, :].astype(BF16)
    attend(0, first, q_ref[0, first, :].astype(BF16),
           jnp.concatenate([k_first, k_first], axis=0), jnp.concatenate([v_first, v_first], axis=0), band_first)

    for n in range(1, seq // BLOCK):
        rows = pl.ds(n * BLOCK, BLOCK)
        win = pl.ds((n - 1) * BLOCK, 2 * BLOCK)
        attend(0, rows, q_ref[0, rows, :].astype(BF16), k_ref[0, win, :].astype(BF16),
               v_ref[0, win, :].astype(BF16), band)

    for g in range(1, N_GROUPS_A):
        d = DIL_PATTERNS[g][1]
        sub_len = seq // d
        q_ref, k_ref, v_ref = qkv[g]
        for r in range(d):
            sub = pl.ds(r, sub_len, stride=d)
            q = q_ref[0, sub, :].astype(BF16)
            k = k_ref[0, sub, :].astype(BF16)
            v = v_ref[0, sub, :].astype(BF16)
            for n in range(sub_len // BLOCK):
                rows = pl.ds(n * BLOCK * d + r, BLOCK, stride=d)
                qb = q[n * BLOCK:(n + 1) * BLOCK]
                if n == 0:
                    attend(g, rows, qb, jnp.concatenate([k[:BLOCK], k[:BLOCK]], axis=0),
                           jnp.concatenate([v[:BLOCK], v[:BLOCK]], axis=0), band_first)
                else:
                    attend(g, rows, qb, k[(n - 1) * BLOCK:(n + 1) * BLOCK],
                           v[(n - 1) * BLOCK:(n + 1) * BLOCK], band)

    def merge(i, carry):
        rows = pl.ds(pl.multiple_of(i * 2 * BLOCK, 2 * BLOCK), 2 * BLOCK)
        ms = [m_ref[g, rows, :] for g in range(N_GROUPS_A)]
        mx = jnp.maximum(jnp.maximum(ms[0], ms[1]), ms[2])
        num = jnp.zeros((2 * BLOCK, LANES), F32)
        den = jnp.zeros((2 * BLOCK, LANES), F32)
        for g in range(N_GROUPS_A):
            w = jnp.exp(ms[g] - mx)
            num = num + w * acc_ref[g, rows, :]
            den = den + w * l_ref[g, rows, :]
        o_ref[0, rows, :] = (num / den).astype(o_ref.dtype)
        return carry

    lax.fori_loop(0, seq // (2 * BLOCK), merge, 0)


def _mixer_a(proj_a, rev_a, weights):
    B, S, _ = proj_a.shape
    npair = OUT_WIDTH_A // LANES
    grid = (B, npair)

    def col(which, g):
        base = (which * WIDTH_A + g * OUT_WIDTH_A) // LANES
        return pl.BlockSpec((1, S, LANES), lambda b, hp: (b, 0, base + hp))

    cast_in, cast_out, cast_shapes = _cast_specs(weights, grid)
    in_specs = [col(which, g) for g in range(N_GROUPS_A) for which in range(3)]
    in_specs.append(pl.BlockSpec((1, 2 * N_GROUPS_A, 4 * BLOCK), lambda b, hp: (hp, 0, 0)))
    return pl.pallas_call(
        functools.partial(_mixer_a_kernel, seq=S, n_cast=len(weights)),
        grid=grid,
        in_specs=in_specs + cast_in,
        out_specs=[pl.BlockSpec((1, S, LANES), lambda b, hp: (b, 0, hp))] + cast_out,
        out_shape=[jax.ShapeDtypeStruct((B, S, OUT_WIDTH_A), BF16)] + cast_shapes,
        scratch_shapes=[pltpu.VMEM((N_GROUPS_A, 2 * BLOCK, 2 * BLOCK), F32),
                        pltpu.VMEM((N_GROUPS_A, S, LANES), F32),
                        pltpu.VMEM((N_GROUPS_A, S, LANES), F32),
                        pltpu.VMEM((N_GROUPS_A, S, LANES), F32)],
        compiler_params=_params(("arbitrary", "arbitrary")),
        name="mixer_a",
    )(*([proj_a] * 9), rev_a, *weights)


def _mixer_b_kernel(*refs, tq, seq, lam_init, n_cast):
    q_ref, k_ref, v_ref, rev_ref, lam_ref, g_ref = refs[:6]
    o_ref = refs[6 + n_cast]
    toep_ref, s_ref, p_ref, vx_ref = refs[7 + 2 * n_cast:]
    _cast_blocks(refs[6:6 + n_cast], refs[7 + n_cast:7 + 2 * n_cast])
    nq = seq // tq
    rg_rows = BF16_ROWS
    n_rg = 2 * tq // rg_rows

    x = jnp.broadcast_to(rev_ref[0] * LOG2E, (tq, seq + tq))
    rolled = pltpu.roll(x, 0, 1, stride=1, stride_axis=0)
    for c in range(nq + 1):
        toep_ref[c] = rolled[:, c * tq:(c + 1) * tq]

    vx_ref[:, :LANES] = v_ref[0]
    vx_ref[:, LANES:] = jnp.ones((seq, LANES), BF16)

    lane = lax.broadcasted_iota(jnp.int32, (tq, LANES), 1)
    lo = lane < HEAD_DIM
    row = lax.broadcasted_iota(jnp.int32, (2 * tq, tq), 0)
    col = lax.broadcasted_iota(jnp.int32, (2 * tq, tq), 1)
    causal = col <= (row & (tq - 1))
    lv = lam_ref[...]
    lam = (jnp.exp(jnp.sum(lv[0:1] * lv[1:2], axis=-1, keepdims=True))
           - jnp.exp(jnp.sum(lv[2:3] * lv[3:4], axis=-1, keepdims=True)) + lam_init)

    def score_chunks(qi):
        slot = qi % 2
        q = q_ref[0, qi * tq:(qi + 1) * tq, :]
        zero = jnp.zeros_like(q)
        qz = jnp.concatenate([jnp.where(lo, q, zero), jnp.where(lo, zero, q)], axis=0)

        def chunk(c):
            kc = k_ref[0, c * tq:(c + 1) * tq, :]
            s = lax.dot_general(qz, kc, (((1,), (1,)), ((), ())), preferred_element_type=F32)
            bias = toep_ref[nq - qi + c]
            s = s * (SCALE * LOG2E) + jnp.concatenate([bias, bias], axis=0)
            if c == qi:
                s = jnp.where(causal, s, NEG_INF)
            s_ref[slot, :, c * tq:(c + 1) * tq] = s

        return [functools.partial(chunk, c) for c in range(qi + 1)]

    def softmax_group(qi, rg):
        slot, width = qi % 2, (qi + 1) * tq
        rows = slice(rg * rg_rows, (rg + 1) * rg_rows)
        m = jnp.max(s_ref[slot, rows, :width], axis=-1, keepdims=True)
        p_ref[slot, rows, :width] = jnp.exp2(s_ref[slot, rows, :width] - m).astype(BF16)

    def finish(qi):
        slot, width = qi % 2, (qi + 1) * tq
        acc = jnp.dot(p_ref[slot, :, :width], vx_ref[:width, :], preferred_element_type=F32)
        o = acc[:, :LANES] / acc[:, LANES:]
        y = o[:tq] - lam * o[tq:]
        y = _rms(y, g_ref[...]) * (1.0 - lam_init)
        o_ref[0, qi * tq:(qi + 1) * tq, :] = y.astype(o_ref.dtype)

    for chunk in score_chunks(0):
        chunk()
    for qi in range(nq):
        chunks = score_chunks(qi + 1) if qi + 1 < nq else []
        per = -(-n_rg // (len(chunks) + 1))
        rg = 0
        for chunk in chunks:
            chunk()
            for _ in range(per):
                if rg < n_rg:
                    softmax_group(qi, rg)
                    rg += 1
        while rg < n_rg:
            softmax_group(qi, rg)
            rg += 1
        finish(qi)


def _mixer_b(proj, rev_b, lam_vecs, subln_g, lam_init, weights, tq=256):
    B, S, _ = proj.shape
    H = N_HEADS_B
    nq = S // tq
    grid = (B, H)
    kern = functools.partial(_mixer_b_kernel, tq=tq, seq=S, lam_init=lam_init, n_cast=len(weights))

    def col(off):
        return pl.BlockSpec((1, S, LANES), lambda b, h: (b, 0, off // LANES + h))

    cast_in, cast_out, cast_shapes = _cast_specs(weights, grid)
    return pl.pallas_call(
        kern,
        grid=grid,
        in_specs=[col(OFF_QB), col(OFF_KB), col(OFF_VB),
                  pl.BlockSpec((1, 1, S + tq), lambda b, h: (h, 0, 0)),
                  pl.BlockSpec((4, HEAD_DIM), lambda b, h: (0, 0)),
                  pl.BlockSpec((1, 2 * HEAD_DIM), lambda b, h: (0, 0))] + cast_in,
        out_specs=[pl.BlockSpec((1, S, LANES), lambda b, h: (b, 0, h))] + cast_out,
        out_shape=[jax.ShapeDtypeStruct((B, S, WIDTH_B), BF16)] + cast_shapes,
        scratch_shapes=[pltpu.VMEM((nq + 1, tq, tq), F32),
                        pltpu.VMEM((2, 2 * tq, S), F32),
                        pltpu.VMEM((2, 2 * tq, S), BF16),
                        pltpu.VMEM((S, 2 * LANES), BF16)],
        compiler_params=_params(("arbitrary", "arbitrary")),
        name="mixer_b",
    )(proj, proj, proj, rev_b, lam_vecs, subln_g, *weights)


def _gate_merge_kernel(ya_ref, yb_ref, ga_ref, gb_ref, wa_ref, wb_ref, out_ref):
    pa = jnp.dot(ya_ref[...], wa_ref[...], preferred_element_type=F32)
    pb = jnp.dot(yb_ref[...], wb_ref[...], preferred_element_type=F32)
    ga = jax.nn.sigmoid(ga_ref[...].astype(F32))
    gb = jax.nn.sigmoid(gb_ref[...].astype(F32))
    out_ref[...] = (ga * pa + gb * pb).astype(out_ref.dtype)


def _gate_merge(ya, yb, proj_r, wa, wb, tm=1024, tn=512):
    T = yb.shape[0]
    D = D_MODEL
    return pl.pallas_call(
        _gate_merge_kernel,
        grid=(T // tm, D // tn),
        in_specs=[pl.BlockSpec((tm, OUT_WIDTH_A), lambda i, j: (i, 0)),
                  pl.BlockSpec((tm, WIDTH_B), lambda i, j: (i, 0)),
                  pl.BlockSpec((tm, tn), lambda i, j: (i, OFF_GA // tn + j)),
                  pl.BlockSpec((tm, tn), lambda i, j: (i, OFF_GB // tn + j)),
                  pl.BlockSpec((OUT_WIDTH_A, tn), lambda i, j: (0, j)),
                  pl.BlockSpec((WIDTH_B, tn), lambda i, j: (0, j))],
        out_specs=pl.BlockSpec((tm, tn), lambda i, j: (i, j)),
        out_shape=jax.ShapeDtypeStruct((T, D), BF16),
        compiler_params=_params(("parallel", "arbitrary")),
        name="gate_merge",
    )(ya, yb, proj_r, proj_r, wa, wb)


def _out_proj_kernel(m_ref, w_ref, x_ref, o_ref):
    o_ref[...] = x_ref[...] + jnp.dot(m_ref[...], w_ref[...], preferred_element_type=F32)


def _out_proj(merged, w, x2, tm=1024, tn=512):
    T, D = x2.shape
    return pl.pallas_call(
        _out_proj_kernel,
        grid=(T // tm, D // tn),
        in_specs=[pl.BlockSpec((tm, D), lambda i, j: (i, 0)),
                  pl.BlockSpec((D, tn), lambda i, j: (0, j)),
                  pl.BlockSpec((tm, tn), lambda i, j: (i, j))],
        out_specs=pl.BlockSpec((tm, tn), lambda i, j: (i, j)),
        out_shape=jax.ShapeDtypeStruct((T, D), F32),
        compiler_params=_params(("parallel", "arbitrary")),
        name="out_proj",
    )(merged, w, x2)


def _ffn_kernel(x_ref, g_ref, wg_ref, wu_ref, wd_ref, gf_ref, o_ref, h_ref):
    f = pl.program_id(1)

    @pl.when(f == 0)
    def _():
        h_ref[...] = _rms(x_ref[...], g_ref[...]).astype(BF16)
        o_ref[...] = x_ref[...]

    h = h_ref[...]
    a = jnp.dot(h, wg_ref[...], preferred_element_type=F32)
    b = jnp.dot(h, wu_ref[...], preferred_element_type=F32)
    u = (a * jax.nn.sigmoid(a)) * b
    o_ref[...] += jnp.dot(u.astype(BF16), wd_ref[...], preferred_element_type=F32)

    @pl.when(f == pl.num_programs(1) - 1)
    def _():
        o_ref[...] = _rms(o_ref[...], gf_ref[...])


def _ffn(x1, g, wg, wu, wd, gf, tm=1024, tf=512):
    T, D = x1.shape
    F = wg.shape[1]
    return pl.pallas_call(
        _ffn_kernel,
        grid=(T // tm, F // tf),
        in_specs=[pl.BlockSpec((tm, D), lambda i, f: (i, 0), pipeline_mode=pl.Buffered(1)),
                  pl.BlockSpec((1, D), lambda i, f: (0, 0)),
                  pl.BlockSpec((D, tf), lambda i, f: (0, f)),
                  pl.BlockSpec((D, tf), lambda i, f: (0, f)),
                  pl.BlockSpec((tf, D), lambda i, f: (f, 0)),
                  pl.BlockSpec((1, D), lambda i, f: (0, 0))],
        out_specs=pl.BlockSpec((tm, D), lambda i, f: (i, 0)),
        out_shape=jax.ShapeDtypeStruct((T, D), F32),
        scratch_shapes=[pltpu.VMEM((tm, D), BF16)],
        compiler_params=_params(("parallel", "arbitrary")),
        name="ffn",
    )(x1, g, wg, wu, wd, gf)


def _rev_a_index():
    u = np.arange(4 * BLOCK)
    rel = np.clip(2 * BLOCK - u, 0, None)
    return np.stack([_rel_bucket_np(rel * d) for _, d in DIL_PATTERNS])


def _rev_b_index(seq, tq):
    c = np.arange(seq + tq)
    return _rel_bucket_np(np.clip(seq - c, 0, seq - 1))


def _lookup(table, idx):
    idx = jnp.asarray(idx)[None, :]
    out = jnp.zeros((table.shape[1], idx.shape[1]), F32)
    for b in range(NUM_BUCKETS):
        out = jnp.where(idx == b, table[b][:, None], out)
    return out


def kernel(x, norm_attn_g, w_in, w_proj_a, w_proj_b, w_out, rel_bias_table, diff_lambda_q1, diff_lambda_k1, diff_lambda_q2, diff_lambda_k2, diff_subln_g, norm_ffn_g, w_ffn_gate, w_ffn_up, w_ffn_down, norm_final_g):
    B, S, D = x.shape
    T = B * S
    depth = w_in.shape[0]
    assert depth == 1, "the final RMSNorm is fused into the FFN epilogue of a single layer"
    table_a = rel_bias_table[:, :N_HEADS_A].astype(F32)
    table_b = rel_bias_table[:, N_HEADS_A:].astype(F32)
    tq = 256

    idx_a = _rev_a_index()
    rev_a = jnp.stack([_lookup(table_a[:, g * HEADS_PER_GROUP_A:(g + 1) * HEADS_PER_GROUP_A], idx_a[g])
                       for g in range(N_GROUPS_A)])
    npair = OUT_WIDTH_A // LANES
    rev_a = jnp.transpose(rev_a.reshape(N_GROUPS_A, npair, 2, 4 * BLOCK), (1, 0, 2, 3))
    rev_a = rev_a.reshape(npair, 2 * N_GROUPS_A, 4 * BLOCK)
    rev_b = _lookup(table_b, _rev_b_index(S, tq))[:, None, :]

    x2 = x.reshape(T, D)
    l = 0
    lam_init = 0.8 - 0.6 * math.exp(-0.3 * l)
    proj_a, proj_r = _in_proj(x2, norm_attn_g[l][None, :], w_in[l])
    ya, wa, wb, wo = _mixer_a(proj_a.reshape(B, S, PROJ_A), rev_a, [w_proj_a[l], w_proj_b[l], w_out[l]])

    lam_vecs = jnp.stack([diff_lambda_q1[l], diff_lambda_k1[l],
                          diff_lambda_q2[l], diff_lambda_k2[l]]).astype(F32)
    yb, wg, wu, wd = _mixer_b(proj_r.reshape(B, S, PROJ_R), rev_b, lam_vecs, diff_subln_g[l][None, :].astype(F32),
                              lam_init, [w_ffn_gate[l], w_ffn_up[l], w_ffn_down[l]], tq=tq)

    merged = _gate_merge(ya.reshape(T, OUT_WIDTH_A), yb.reshape(T, WIDTH_B), proj_r, wa, wb)
    x2 = _out_proj(merged, wo, x2)
    x2 = _ffn(x2, norm_ffn_g[l][None, :], wg, wu, wd, norm_final_g[None, :])
    return x2.reshape(B, S, D)
```

```python
import functools
import math

import numpy as np
import jax
import jax.numpy as jnp
from jax import lax
from jax.experimental import pallas as pl
from jax.experimental.pallas import tpu as pltpu

D_MODEL = 2048
HEAD_DIM = 64
DIL_PATTERNS = ((128, 1), (512, 4), (2048, 16))
N_GROUPS_A = len(DIL_PATTERNS)
HEADS_PER_GROUP_A = 8
N_HEADS_A = N_GROUPS_A * HEADS_PER_GROUP_A
WIDTH_A = N_HEADS_A * HEAD_DIM
OUT_WIDTH_A = HEADS_PER_GROUP_A * HEAD_DIM
BLOCK = 128
N_HEADS_B = D_MODEL // (2 * HEAD_DIM)
WIDTH_B = N_HEADS_B * 2 * HEAD_DIM
NUM_BUCKETS = 32
MAX_DISTANCE = 2048
D_FF = -(-8 * D_MODEL // (3 * 256)) * 256
PROJ_A = 3 * WIDTH_A
PROJ_R = 3 * WIDTH_B + 2 * D_MODEL
D_IN = PROJ_A + PROJ_R
NORM_EPS = 1e-6
NEG_INF = -1e30
SCALE = HEAD_DIM ** -0.5
LOG2E = math.log2(math.e)

OFF_QB = 0
OFF_KB = WIDTH_B
OFF_VB = 2 * WIDTH_B
OFF_GA = 3 * WIDTH_B
OFF_GB = OFF_GA + D_MODEL

LANES = 128
VMEM_LIMIT = 56 * 1024 * 1024

BF16 = jnp.bfloat16
F32 = jnp.float32


def _rel_bucket_np(dist):
    n = np.maximum(dist, 0)
    max_exact = NUM_BUCKETS // 2
    nf = np.maximum(n, 1).astype(np.float32)
    large = max_exact + (np.log(nf / np.float32(max_exact)) / np.float32(math.log(MAX_DISTANCE / max_exact))
                         * np.float32(NUM_BUCKETS - max_exact)).astype(np.int32)
    large = np.minimum(large, NUM_BUCKETS - 1)
    return np.where(n < max_exact, n, large).astype(np.int32)


def _rms(x, g):
    ms = jnp.mean(x * x, axis=-1, keepdims=True)
    return x * lax.rsqrt(ms + NORM_EPS) * g


def _params(sem, vmem=VMEM_LIMIT):
    return pltpu.CompilerParams(dimension_semantics=sem, vmem_limit_bytes=vmem)


BF16_ROWS = 16


def _cast_specs(weights, grid):
    nsteps = math.prod(grid)
    in_specs, out_specs, out_shapes = [], [], []
    for w in weights:
        rows, cols = w.shape
        blk = next(r for r in range(BF16_ROWS, rows + 1, BF16_ROWS)
                   if rows % r == 0 and nsteps % (rows // r) == 0 and rows // r <= nsteps)
        per = nsteps // (rows // blk)

        def index(*ids, per=per):
            step = 0
            for i, n in zip(ids, grid):
                step = step * n + i
            return (step // per, 0)

        in_specs.append(pl.BlockSpec((blk, cols), index))
        out_specs.append(pl.BlockSpec((blk, cols), index))
        out_shapes.append(jax.ShapeDtypeStruct((rows, cols), BF16))
    return in_specs, out_specs, out_shapes


def _cast_blocks(in_refs, out_refs):
    for i_ref, o_ref in zip(in_refs, out_refs):
        o_ref[...] = i_ref[...].astype(BF16)


def _in_proj_kernel(x_ref, g_ref, w_ref, oa_ref, or_ref, h_ref, *, na):
    j = pl.program_id(1)

    @pl.when(j == 0)
    def _():
        h_ref[...] = _rms(x_ref[...], g_ref[...]).astype(BF16)

    @pl.when(j < na)
    def _():
        oa_ref[...] = jnp.dot(h_ref[...], w_ref[...].astype(BF16), preferred_element_type=F32)

    @pl.when(j >= na)
    def _():
        or_ref[...] = jnp.dot(h_ref[...], w_ref[...].astype(BF16), preferred_element_type=F32).astype(BF16)


def _in_proj(x2, g, w, tm=2048, tn=512):
    T, D = x2.shape
    na = PROJ_A // tn
    return pl.pallas_call(
        functools.partial(_in_proj_kernel, na=na),
        grid=(T // tm, D_IN // tn),
        in_specs=[pl.BlockSpec((tm, D), lambda i, j: (i, 0), pipeline_mode=pl.Buffered(1)),
                  pl.BlockSpec((1, D), lambda i, j: (0, 0)),
                  pl.BlockSpec((D, tn), lambda i, j: (0, j))],
        out_specs=[pl.BlockSpec((tm, tn), lambda i, j: (i, jnp.minimum(j, na - 1))),
                   pl.BlockSpec((tm, tn), lambda i, j: (i, jnp.maximum(j - na, 0)))],
        out_shape=[jax.ShapeDtypeStruct((T, PROJ_A), F32), jax.ShapeDtypeStruct((T, PROJ_R), BF16)],
        scratch_shapes=[pltpu.VMEM((tm, D), BF16)],
        compiler_params=_params(("parallel", "arbitrary")),
        name="in_proj",
    )(x2, g, w)


def _mixer_a_kernel(*refs, seq, n_cast):
    qkv = (refs[0:3], refs[3:6], refs[6:9])
    rev_ref = refs[9]
    o_ref = refs[10 + n_cast]
    (bias_ref, q_st, k_st, vx_st, s_ref, p_ref, mrow_ref, m_ref, l_ref, acc_ref) = refs[11 + 2 * n_cast:]
    _cast_blocks(refs[10:10 + n_cast], refs[11 + n_cast:11 + 2 * n_cast])
    lane = lax.broadcasted_iota(jnp.int32, (BLOCK, LANES), 1)
    lo = lane < HEAD_DIM

    row = lax.broadcasted_iota(jnp.int32, (2 * BLOCK, 2 * BLOCK), 0)
    col = lax.broadcasted_iota(jnp.int32, (2 * BLOCK, 2 * BLOCK), 1)
    rel = BLOCK + (row & (BLOCK - 1)) - col
    band = (rel >= 0) & (rel <= BLOCK)
    band_first = band & (col >= BLOCK)
    for g in range(N_GROUPS_A):
        halves = []
        for hh in range(2):
            x = jnp.broadcast_to(rev_ref[0, g * 2 + hh:g * 2 + hh + 1, :], (BLOCK, 4 * BLOCK))
            halves.append(pltpu.roll(x, 0, 1, stride=1, stride_axis=0)[:, BLOCK:3 * BLOCK])
        toep = jnp.concatenate(halves, axis=0)
        bias_ref[2 * g] = jnp.where(band, toep, NEG_INF)
        bias_ref[2 * g + 1] = jnp.where(band_first, toep, NEG_INF)

    for g in range(N_GROUPS_A):
        k_st[g, 0:BLOCK, :] = jnp.zeros((BLOCK, LANES), BF16)
        vx_st[g, 0:BLOCK, :] = jnp.zeros((BLOCK, 2 * LANES), BF16)
        vx_st[g, :, LANES:] = jnp.ones((BLOCK + seq, LANES), BF16)

    def scores(g, slot, base, first):
        q = q_st[g, base:base + BLOCK, :]
        zero = jnp.zeros_like(q)
        qz = jnp.concatenate([jnp.where(lo, q, zero), jnp.where(lo, zero, q)], axis=0)
        kw = k_st[g, base - BLOCK:base + BLOCK, :]
        s = lax.dot_general(qz, kw, (((1,), (1,)), ((), ())), preferred_element_type=F32)
        s_ref[slot] = s + bias_ref[2 * g + (1 if first else 0)]

    def softmax_group(slot, rg):
        rows = slice(rg * BF16_ROWS, (rg + 1) * BF16_ROWS)
        m = jnp.max(s_ref[slot, rows, :], axis=-1, keepdims=True)
        p_ref[slot, rows, :] = jnp.exp(s_ref[slot, rows, :] - m).astype(BF16)
        mrow_ref[slot, rows, :] = jnp.broadcast_to(m, (BF16_ROWS, LANES))

    def finish(g, slot, base, out_rows):
        acc = jnp.dot(p_ref[slot], vx_st[g, base - BLOCK:base + BLOCK, :], preferred_element_type=F32)
        mrow = mrow_ref[slot]
        m_ref[g, out_rows, :] = jnp.where(lo, mrow[:BLOCK], mrow[BLOCK:])
        l_ref[g, out_rows, :] = jnp.where(lo, acc[:BLOCK, LANES:], acc[BLOCK:, LANES:])
        acc_ref[g, out_rows, :] = jnp.where(lo, acc[:BLOCK, :LANES], acc[BLOCK:, :LANES])

    n_rg = 2 * BLOCK // BF16_ROWS
    blocks = []
    for g, (_, d) in enumerate(DIL_PATTERNS):
        sub_len = seq // d
        q_ref, k_ref, v_ref = qkv[g]
        for r in range(d):
            src = pl.ds(r, sub_len, stride=d) if d > 1 else pl.ds(0, seq)
            dst = slice(BLOCK + r * sub_len, BLOCK + (r + 1) * sub_len)
            q_st[g, dst, :] = (q_ref[0, src, :] * SCALE).astype(BF16)
            k_st[g, dst, :] = k_ref[0, src, :].astype(BF16)
            vx_st[g, dst, :LANES] = v_ref[0, src, :].astype(BF16)
            for n in range(sub_len // BLOCK):
                base = BLOCK + r * sub_len + n * BLOCK
                out_rows = pl.ds(n * BLOCK * d + r, BLOCK, stride=d) if d > 1 else pl.ds(n * BLOCK, BLOCK)
                blocks.append((g, base, n == 0, out_rows))

    nslot = s_ref.shape[0]
    for t in range(len(blocks) + 2):
        if t < len(blocks):
            g, base, first, _ = blocks[t]
            scores(g, t % nslot, base, first)
        for rg in range(n_rg):
            if 1 <= t <= len(blocks):
                softmax_group((t - 1) % nslot, rg)
            if rg == n_rg // 2 and 2 <= t:
                g, base, _, out_rows = blocks[t - 2]
                finish(g, (t - 2) % nslot, base, out_rows)

    def merge(i, carry):
        rows = pl.ds(pl.multiple_of(i * 2 * BLOCK, 2 * BLOCK), 2 * BLOCK)
        ms = [m_ref[g, rows, :] for g in range(N_GROUPS_A)]
        mx = jnp.maximum(jnp.maximum(ms[0], ms[1]), ms[2])
        num = jnp.zeros((2 * BLOCK, LANES), F32)
        den = jnp.zeros((2 * BLOCK, LANES), F32)
        for g in range(N_GROUPS_A):
            w = jnp.exp(ms[g] - mx)
            num = num + w * acc_ref[g, rows, :]
            den = den + w * l_ref[g, rows, :]
        o_ref[0, rows, :] = (num / den).astype(o_ref.dtype)
        return carry

    lax.fori_loop(0, seq // (2 * BLOCK), merge, 0)


def _mixer_a(proj_a, rev_a, weights):
    B, S, _ = proj_a.shape
    npair = OUT_WIDTH_A // LANES
    grid = (B, npair)

    def col(which, g):
        base = (which * WIDTH_A + g * OUT_WIDTH_A) // LANES
        return pl.BlockSpec((1, S, LANES), lambda b, hp: (b, 0, base + hp))

    cast_in, cast_out, cast_shapes = _cast_specs(weights, grid)
    in_specs = [col(which, g) for g in range(N_GROUPS_A) for which in range(3)]
    in_specs.append(pl.BlockSpec((1, 2 * N_GROUPS_A, 4 * BLOCK), lambda b, hp: (hp, 0, 0)))
    return pl.pallas_call(
        functools.partial(_mixer_a_kernel, seq=S, n_cast=len(weights)),
        grid=grid,
        in_specs=in_specs + cast_in,
        out_specs=[pl.BlockSpec((1, S, LANES), lambda b, hp: (b, 0, hp))] + cast_out,
        out_shape=[jax.ShapeDtypeStruct((B, S, OUT_WIDTH_A), BF16)] + cast_shapes,
        scratch_shapes=[pltpu.VMEM((2 * N_GROUPS_A, 2 * BLOCK, 2 * BLOCK), F32),
                        pltpu.VMEM((N_GROUPS_A, BLOCK + S, LANES), BF16),
                        pltpu.VMEM((N_GROUPS_A, BLOCK + S, LANES), BF16),
                        pltpu.VMEM((N_GROUPS_A, BLOCK + S, 2 * LANES), BF16),
                        pltpu.VMEM((4, 2 * BLOCK, 2 * BLOCK), F32),
                        pltpu.VMEM((4, 2 * BLOCK, 2 * BLOCK), BF16),
                        pltpu.VMEM((4, 2 * BLOCK, LANES), F32),
                        pltpu.VMEM((N_GROUPS_A, S, LANES), F32),
                        pltpu.VMEM((N_GROUPS_A, S, LANES), F32),
                        pltpu.VMEM((N_GROUPS_A, S, LANES), F32)],
        compiler_params=_params(("arbitrary", "arbitrary")),
        name="mixer_a",
    )(*([proj_a] * 9), rev_a, *weights)


def _mixer_b_kernel(*refs, tq, seq, lam_init, n_cast):
    q_ref, k_ref, v_ref, rev_ref, lam_ref, g_ref = refs[:6]
    o_ref = refs[6 + n_cast]
    toep_ref, s_ref, p_ref, vx_ref = refs[7 + 2 * n_cast:]
    _cast_blocks(refs[6:6 + n_cast], refs[7 + n_cast:7 + 2 * n_cast])
    nq = seq // tq
    rg_rows = BF16_ROWS
    n_rg = 2 * tq // rg_rows

    x = jnp.broadcast_to(rev_ref[0] * LOG2E, (tq, seq + tq))
    rolled = pltpu.roll(x, 0, 1, stride=1, stride_axis=0)
    for c in range(nq + 1):
        toep_ref[c] = rolled[:, c * tq:(c + 1) * tq]

    vx_ref[:, :LANES] = v_ref[0]
    vx_ref[:, LANES:] = jnp.ones((seq, LANES), BF16)

    lane = lax.broadcasted_iota(jnp.int32, (tq, LANES), 1)
    lo = lane < HEAD_DIM
    row = lax.broadcasted_iota(jnp.int32, (2 * tq, tq), 0)
    col = lax.broadcasted_iota(jnp.int32, (2 * tq, tq), 1)
    causal = col <= (row & (tq - 1))
    lv = lam_ref[...]
    lam = (jnp.exp(jnp.sum(lv[0:1] * lv[1:2], axis=-1, keepdims=True))
           - jnp.exp(jnp.sum(lv[2:3] * lv[3:4], axis=-1, keepdims=True)) + lam_init)

    def score_chunks(qi):
        slot = qi % 2
        q = q_ref[0, qi * tq:(qi + 1) * tq, :]
        zero = jnp.zeros_like(q)
        qz = jnp.concatenate([jnp.where(lo, q, zero), jnp.where(lo, zero, q)], axis=0)

        def chunk(c):
            kc = k_ref[0, c * tq:(c + 1) * tq, :]
            s = lax.dot_general(qz, kc, (((1,), (1,)), ((), ())), preferred_element_type=F32)
            bias = toep_ref[nq - qi + c]
            s = s * (SCALE * LOG2E) + jnp.concatenate([bias, bias], axis=0)
            if c == qi:
                s = jnp.where(causal, s, NEG_INF)
            s_ref[slot, :, c * tq:(c + 1) * tq] = s

        return [functools.partial(chunk, c) for c in range(qi + 1)]

    def softmax_group(qi, rg):
        slot, width = qi % 2, (qi + 1) * tq
        rows = slice(rg * rg_rows, (rg + 1) * rg_rows)
        m = jnp.max(s_ref[slot, rows, :width], axis=-1, keepdims=True)
        p_ref[slot, rows, :width] = jnp.exp2(s_ref[slot, rows, :width] - m).astype(BF16)

    def finish(qi):
        slot, width = qi % 2, (qi + 1) * tq
        acc = jnp.dot(p_ref[slot, :, :width], vx_ref[:width, :], preferred_element_type=F32)
        o = acc[:, :LANES] / acc[:, LANES:]
        y = o[:tq] - lam * o[tq:]
        y = _rms(y, g_ref[...]) * (1.0 - lam_init)
        o_ref[0, qi * tq:(qi + 1) * tq, :] = y.astype(o_ref.dtype)

    for chunk in score_chunks(0):
        chunk()
    for qi in range(nq):
        chunks = score_chunks(qi + 1) if qi + 1 < nq else []
        per = -(-n_rg // (len(chunks) + 1))
        rg = 0
        for chunk in chunks:
            chunk()
            for _ in range(per):
                if rg < n_rg:
                    softmax_group(qi, rg)
                    rg += 1
        while rg < n_rg:
            softmax_group(qi, rg)
            rg += 1
        finish(qi)


def _mixer_b(proj, rev_b, lam_vecs, subln_g, lam_init, weights, tq=256):
    B, S, _ = proj.shape
    H = N_HEADS_B
    nq = S // tq
    grid = (B, H)
    kern = functools.partial(_mixer_b_kernel, tq=tq, seq=S, lam_init=lam_init, n_cast=len(weights))

    def col(off):
        return pl.BlockSpec((1, S, LANES), lambda b, h: (b, 0, off // LANES + h))

    cast_in, cast_out, cast_shapes = _cast_specs(weights, grid)
    return pl.pallas_call(
        kern,
        grid=grid,
        in_specs=[col(OFF_QB), col(OFF_KB), col(OFF_VB),
                  pl.BlockSpec((1, 1, S + tq), lambda b, h: (h, 0, 0)),
                  pl.BlockSpec((4, HEAD_DIM), lambda b, h: (0, 0)),
                  pl.BlockSpec((1, 2 * HEAD_DIM), lambda b, h: (0, 0))] + cast_in,
        out_specs=[pl.BlockSpec((1, S, LANES), lambda b, h: (b, 0, h))] + cast_out,
        out_shape=[jax.ShapeDtypeStruct((B, S, WIDTH_B), BF16)] + cast_shapes,
        scratch_shapes=[pltpu.VMEM((nq + 1, tq, tq), F32),
                        pltpu.VMEM((2, 2 * tq, S), F32),
                        pltpu.VMEM((2, 2 * tq, S), BF16),
                        pltpu.VMEM((S, 2 * LANES), BF16)],
        compiler_params=_params(("arbitrary", "arbitrary")),
        name="mixer_b",
    )(proj, proj, proj, rev_b, lam_vecs, subln_g, *weights)


def _gate_merge_kernel(ya_ref, yb_ref, ga_ref, gb_ref, wa_ref, wb_ref, out_ref):
    pa = jnp.dot(ya_ref[...], wa_ref[...], preferred_element_type=F32)
    pb = jnp.dot(yb_ref[...], wb_ref[...], preferred_element_type=F32)
    ga = jax.nn.sigmoid(ga_ref[...].astype(F32))
    gb = jax.nn.sigmoid(gb_ref[...].astype(F32))
    out_ref[...] = (ga * pa + gb * pb).astype(out_ref.dtype)


def _gate_merge(ya, yb, proj_r, wa, wb, tm=1024, tn=512):
    T = yb.shape[0]
    D = D_MODEL
    return pl.pallas_call(
        _gate_merge_kernel,
        grid=(T // tm, D // tn),
        in_specs=[pl.BlockSpec((tm, OUT_WIDTH_A), lambda i, j: (i, 0)),
                  pl.BlockSpec((tm, WIDTH_B), lambda i, j: (i, 0)),
                  pl.BlockSpec((tm, tn), lambda i, j: (i, OFF_GA // tn + j)),
                  pl.BlockSpec((tm, tn), lambda i, j: (i, OFF_GB // tn + j)),
                  pl.BlockSpec((OUT_WIDTH_A, tn), lambda i, j: (0, j)),
                  pl.BlockSpec((WIDTH_B, tn), lambda i, j: (0, j))],
        out_specs=pl.BlockSpec((tm, tn), lambda i, j: (i, j)),
        out_shape=jax.ShapeDtypeStruct((T, D), BF16),
        compiler_params=_params(("parallel", "arbitrary")),
        name="gate_merge",
    )(ya, yb, proj_r, proj_r, wa, wb)


def _out_proj_kernel(m_ref, w_ref, x_ref, o_ref):
    o_ref[...] = x_ref[...] + jnp.dot(m_ref[...], w_ref[...], preferred_element_type=F32)


def _out_proj(merged, w, x2, tm=1024, tn=512):
    T, D = x2.shape
    return pl.pallas_call(
        _out_proj_kernel,
        grid=(T // tm, D // tn),
        in_specs=[pl.BlockSpec((tm, D), lambda i, j: (i, 0)),
                  pl.BlockSpec((D, tn), lambda i, j: (0, j)),
                  pl.BlockSpec((tm, tn), lambda i, j: (i, j))],
        out_specs=pl.BlockSpec((tm, tn), lambda i, j: (i, j)),
        out_shape=jax.ShapeDtypeStruct((T, D), F32),
        compiler_params=_params(("parallel", "arbitrary")),
        name="out_proj",
    )(merged, w, x2)


def _ffn_kernel(x_ref, g_ref, wg_ref, wu_ref, wd_ref, gf_ref, o_ref, h_ref):
    f = pl.program_id(1)

    @pl.when(f == 0)
    def _():
        h_ref[...] = _rms(x_ref[...], g_ref[...]).astype(BF16)
        o_ref[...] = x_ref[...]

    h = h_ref[...]
    a = jnp.dot(h, wg_ref[...], preferred_element_type=F32)
    b = jnp.dot(h, wu_ref[...], preferred_element_type=F32)
    u = (a * jax.nn.sigmoid(a)) * b
    o_ref[...] += jnp.dot(u.astype(BF16), wd_ref[...], preferred_element_type=F32)

    @pl.when(f == pl.num_programs(1) - 1)
    def _():
        o_ref[...] = _rms(o_ref[...], gf_ref[...])


def _ffn(x1, g, wg, wu, wd, gf, tm=1024, tf=512):
    T, D = x1.shape
    F = wg.shape[1]
    return pl.pallas_call(
        _ffn_kernel,
        grid=(T // tm, F // tf),
        in_specs=[pl.BlockSpec((tm, D), lambda i, f: (i, 0), pipeline_mode=pl.Buffered(1)),
                  pl.BlockSpec((1, D), lambda i, f: (0, 0)),
                  pl.BlockSpec((D, tf), lambda i, f: (0, f)),
                  pl.BlockSpec((D, tf), lambda i, f: (0, f)),
                  pl.BlockSpec((tf, D), lambda i, f: (f, 0)),
                  pl.BlockSpec((1, D), lambda i, f: (0, 0))],
        out_specs=pl.BlockSpec((tm, D), lambda i, f: (i, 0)),
        out_shape=jax.ShapeDtypeStruct((T, D), F32),
        scratch_shapes=[pltpu.VMEM((tm, D), BF16)],
        compiler_params=_params(("parallel", "arbitrary")),
        name="ffn",
    )(x1, g, wg, wu, wd, gf)


def _rev_a_index():
    u = np.arange(4 * BLOCK)
    rel = np.clip(2 * BLOCK - u, 0, None)
    return np.stack([_rel_bucket_np(rel * d) for _, d in DIL_PATTERNS])


def _rev_b_index(seq, tq):
    c = np.arange(seq + tq)
    return _rel_bucket_np(np.clip(seq - c, 0, seq - 1))


def _lookup(table, idx):
    idx = jnp.asarray(idx)[None, :]
    out = jnp.zeros((table.shape[1], idx.shape[1]), F32)
    for b in range(NUM_BUCKETS):
        out = jnp.where(idx == b, table[b][:, None], out)
    return out


def kernel(x, norm_attn_g, w_in, w_proj_a, w_proj_b, w_out, rel_bias_table, diff_lambda_q1, diff_lambda_k1, diff_lambda_q2, diff_lambda_k2, diff_subln_g, norm_ffn_g, w_ffn_gate, w_ffn_up, w_ffn_down, norm_final_g):
    B, S, D = x.shape
    T = B * S
    depth = w_in.shape[0]
    assert depth == 1, "the final RMSNorm is fused into the FFN epilogue of a single layer"
    table_a = rel_bias_table[:, :N_HEADS_A].astype(F32)
    table_b = rel_bias_table[:, N_HEADS_A:].astype(F32)
    tq = 256

    idx_a = _rev_a_index()
    rev_a = jnp.stack([_lookup(table_a[:, g * HEADS_PER_GROUP_A:(g + 1) * HEADS_PER_GROUP_A], idx_a[g])
                       for g in range(N_GROUPS_A)])
    npair = OUT_WIDTH_A // LANES
    rev_a = jnp.transpose(rev_a.reshape(N_GROUPS_A, npair, 2, 4 * BLOCK), (1, 0, 2, 3))
    rev_a = rev_a.reshape(npair, 2 * N_GROUPS_A, 4 * BLOCK)
    rev_b = _lookup(table_b, _rev_b_index(S, tq))[:, None, :]

    x2 = x.reshape(T, D)
    l = 0
    lam_init = 0.8 - 0.6 * math.exp(-0.3 * l)
    proj_a, proj_r = _in_proj(x2, norm_attn_g[l][None, :], w_in[l])
    ya, wa, wb, wo = _mixer_a(proj_a.reshape(B, S, PROJ_A), rev_a, [w_proj_a[l], w_proj_b[l], w_out[l]])

    lam_vecs = jnp.stack([diff_lambda_q1[l], diff_lambda_k1[l],
                          diff_lambda_q2[l], diff_lambda_k2[l]]).astype(F32)
    yb, wg, wu, wd = _mixer_b(proj_r.reshape(B, S, PROJ_R), rev_b, lam_vecs, diff_subln_g[l][None, :].astype(F32),
                              lam_init, [w_ffn_gate[l], w_ffn_up[l], w_ffn_down[l]], tq=tq)

    merged = _gate_merge(ya.reshape(T, OUT_WIDTH_A), yb.reshape(T, WIDTH_B), proj_r, wa, wb)
    x2 = _out_proj(merged, wo, x2)
    x2 = _ffn(x2, norm_ffn_g[l][None, :], wg, wu, wd, norm_final_g[None, :])
    return x2.reshape(B, S, D)
```

```python
import functools
import math

import numpy as np
import jax
import jax.numpy as jnp
from jax import lax
from jax.experimental import pallas as pl
from jax.experimental.pallas import tpu as pltpu

D_MODEL = 2048
HEAD_DIM = 64
DIL_PATTERNS = ((128, 1), (512, 4), (2048, 16))
N_GROUPS_A = len(DIL_PATTERNS)
HEADS_PER_GROUP_A = 8
N_HEADS_A = N_GROUPS_A * HEADS_PER_GROUP_A
WIDTH_A = N_HEADS_A * HEAD_DIM
OUT_WIDTH_A = HEADS_PER_GROUP_A * HEAD_DIM
BLOCK = 128
N_HEADS_B = D_MODEL // (2 * HEAD_DIM)
WIDTH_B = N_HEADS_B * 2 * HEAD_DIM
NUM_BUCKETS = 32
MAX_DISTANCE = 2048
D_FF = -(-8 * D_MODEL // (3 * 256)) * 256
PROJ_A = 3 * WIDTH_A
PROJ_R = 3 * WIDTH_B + 2 * D_MODEL
D_IN = PROJ_A + PROJ_R
NORM_EPS = 1e-6
NEG_INF = -1e30
SCALE = HEAD_DIM ** -0.5
LOG2E = math.log2(math.e)

OFF_QB = 0
OFF_KB = WIDTH_B
OFF_VB = 2 * WIDTH_B
OFF_GA = 3 * WIDTH_B
OFF_GB = OFF_GA + D_MODEL

LANES = 128
VMEM_LIMIT = 56 * 1024 * 1024

BF16 = jnp.bfloat16
F32 = jnp.float32


def _rel_bucket_np(dist):
    n = np.maximum(dist, 0)
    max_exact = NUM_BUCKETS // 2
    nf = np.maximum(n, 1).astype(np.float32)
    large = max_exact + (np.log(nf / np.float32(max_exact)) / np.float32(math.log(MAX_DISTANCE / max_exact))
                         * np.float32(NUM_BUCKETS - max_exact)).astype(np.int32)
    large = np.minimum(large, NUM_BUCKETS - 1)
    return np.where(n < max_exact, n, large).astype(np.int32)


def _rms(x, g):
    ms = jnp.mean(x * x, axis=-1, keepdims=True)
    return x * lax.rsqrt(ms + NORM_EPS) * g


def _params(sem, vmem=VMEM_LIMIT):
    return pltpu.CompilerParams(dimension_semantics=sem, vmem_limit_bytes=vmem)


BF16_ROWS = 16


def _cast_specs(weights, grid):
    nsteps = math.prod(grid)
    in_specs, out_specs, out_shapes = [], [], []
    for w in weights:
        rows, cols = w.shape
        blk = next(r for r in range(BF16_ROWS, rows + 1, BF16_ROWS)
                   if rows % r == 0 and nsteps % (rows // r) == 0 and rows // r <= nsteps)
        per = nsteps // (rows // blk)

        def index(*ids, per=per):
            step = 0
            for i, n in zip(ids, grid):
                step = step * n + i
            return (step // per, 0)

        in_specs.append(pl.BlockSpec((blk, cols), index))
        out_specs.append(pl.BlockSpec((blk, cols), index))
        out_shapes.append(jax.ShapeDtypeStruct((rows, cols), BF16))
    return in_specs, out_specs, out_shapes


def _interleave(lists):
    keyed = [((i + 0.5) / len(items), n, i, item) for n, items in enumerate(lists) for i, item in enumerate(items)]
    return [item for _, _, _, item in sorted(keyed, key=lambda k: k[:3])]


def _cast_blocks(in_refs, out_refs):
    for i_ref, o_ref in zip(in_refs, out_refs):
        o_ref[...] = i_ref[...].astype(BF16)


def _in_proj_kernel(x_ref, g_ref, w_ref, oa_ref, or_ref, h_ref, *, na, nqb):
    j = pl.program_id(1)

    @pl.when(j == 0)
    def _():
        h_ref[...] = _rms(x_ref[...], g_ref[...]).astype(BF16)

    @pl.when(j < na)
    def _():
        oa_ref[...] = jnp.dot(h_ref[...], w_ref[...].astype(BF16), preferred_element_type=F32)

    @pl.when(j >= na)
    def _():
        scale = jnp.where(j < na + nqb, jnp.float32(SCALE * LOG2E), jnp.float32(1.0))
        res = jnp.dot(h_ref[...], w_ref[...].astype(BF16), preferred_element_type=F32)
        or_ref[...] = (res * scale).astype(BF16)


def _in_proj(x2, g, w, tm=2048, tn=512):
    T, D = x2.shape
    na = PROJ_A // tn
    assert OFF_QB == 0 and WIDTH_B % tn == 0
    return pl.pallas_call(
        functools.partial(_in_proj_kernel, na=na, nqb=WIDTH_B // tn),
        grid=(T // tm, D_IN // tn),
        in_specs=[pl.BlockSpec((tm, D), lambda i, j: (i, 0), pipeline_mode=pl.Buffered(1)),
                  pl.BlockSpec((1, D), lambda i, j: (0, 0)),
                  pl.BlockSpec((D, tn), lambda i, j: (0, j))],
        out_specs=[pl.BlockSpec((tm, tn), lambda i, j: (i, jnp.minimum(j, na - 1))),
                   pl.BlockSpec((tm, tn), lambda i, j: (i, jnp.maximum(j - na, 0)))],
        out_shape=[jax.ShapeDtypeStruct((T, PROJ_A), F32), jax.ShapeDtypeStruct((T, PROJ_R), BF16)],
        scratch_shapes=[pltpu.VMEM((tm, D), BF16)],
        compiler_params=_params(("parallel", "arbitrary")),
        name="in_proj",
    )(x2, g, w)


def _mixer_a_kernel(*refs, seq, n_cast):
    qkv = (refs[0:3], refs[3:6], refs[6:9])
    rev_ref = refs[9]
    o_ref = refs[10 + n_cast]
    (bias_ref, q_st, k_st, vx_st, s_ref, p_ref, mrow_ref, m_ref, l_ref, acc_ref) = refs[11 + 2 * n_cast:]
    _cast_blocks(refs[10:10 + n_cast], refs[11 + n_cast:11 + 2 * n_cast])
    lane = lax.broadcasted_iota(jnp.int32, (BLOCK, LANES), 1)
    lo = lane < HEAD_DIM

    row = lax.broadcasted_iota(jnp.int32, (2 * BLOCK, 2 * BLOCK), 0)
    col = lax.broadcasted_iota(jnp.int32, (2 * BLOCK, 2 * BLOCK), 1)
    rel = BLOCK + (row & (BLOCK - 1)) - col
    band = (rel >= 0) & (rel <= BLOCK)
    band_first = band & (col >= BLOCK)
    for g in range(N_GROUPS_A):
        halves = []
        for hh in range(2):
            x = jnp.broadcast_to(rev_ref[0, g * 2 + hh:g * 2 + hh + 1, :], (BLOCK, 4 * BLOCK))
            halves.append(pltpu.roll(x, 0, 1, stride=1, stride_axis=0)[:, BLOCK:3 * BLOCK])
        toep = jnp.concatenate(halves, axis=0)
        bias_ref[2 * g] = jnp.where(band, toep, NEG_INF)
        bias_ref[2 * g + 1] = jnp.where(band_first, toep, NEG_INF)

    for g in range(N_GROUPS_A):
        k_st[g, 0:BLOCK, :] = jnp.zeros((BLOCK, LANES), BF16)
        vx_st[g, 0:BLOCK, :] = jnp.zeros((BLOCK, 2 * LANES), BF16)
        vx_st[g, :, LANES:] = jnp.ones((BLOCK + seq, LANES), BF16)

    def scores(g, slot, base, first):
        q = q_st[g, base:base + BLOCK, :]
        zero = jnp.zeros_like(q)
        qz = jnp.concatenate([jnp.where(lo, q, zero), jnp.where(lo, zero, q)], axis=0)
        kw = k_st[g, base - BLOCK:base + BLOCK, :]
        s = lax.dot_general(qz, kw, (((1,), (1,)), ((), ())), preferred_element_type=F32)
        s_ref[slot] = s + bias_ref[2 * g + (1 if first else 0)]

    def softmax_group(slot, rg):
        rows = slice(rg * BF16_ROWS, (rg + 1) * BF16_ROWS)
        m = jnp.max(s_ref[slot, rows, :], axis=-1, keepdims=True)
        p_ref[slot, rows, :] = jnp.exp(s_ref[slot, rows, :] - m).astype(BF16)
        mrow_ref[slot, rows, :] = jnp.broadcast_to(m, (BF16_ROWS, LANES))

    def finish(g, slot, base, out_rows):
        acc = jnp.dot(p_ref[slot], vx_st[g, base - BLOCK:base + BLOCK, :], preferred_element_type=F32)
        mrow = mrow_ref[slot]
        m_ref[g, out_rows, :] = jnp.where(lo, mrow[:BLOCK], mrow[BLOCK:])
        l_ref[g, out_rows, :] = jnp.where(lo, acc[:BLOCK, LANES:], acc[BLOCK:, LANES:])
        acc_ref[g, out_rows, :] = jnp.where(lo, acc[:BLOCK, :LANES], acc[BLOCK:, :LANES])

    n_rg = 2 * BLOCK // BF16_ROWS
    blocks = []
    for g, (_, d) in enumerate(DIL_PATTERNS):
        sub_len = seq // d
        q_ref, k_ref, v_ref = qkv[g]
        for r in range(d):
            src = pl.ds(r, sub_len, stride=d) if d > 1 else pl.ds(0, seq)
            dst = slice(BLOCK + r * sub_len, BLOCK + (r + 1) * sub_len)
            q_st[g, dst, :] = (q_ref[0, src, :] * SCALE).astype(BF16)
            k_st[g, dst, :] = k_ref[0, src, :].astype(BF16)
            vx_st[g, dst, :LANES] = v_ref[0, src, :].astype(BF16)
            for n in range(sub_len // BLOCK):
                base = BLOCK + r * sub_len + n * BLOCK
                out_rows = pl.ds(n * BLOCK * d + r, BLOCK, stride=d) if d > 1 else pl.ds(n * BLOCK, BLOCK)
                blocks.append((g, base, n == 0, out_rows))

    nslot = s_ref.shape[0]
    for t in range(len(blocks) + 2):
        if t < len(blocks):
            g, base, first, _ = blocks[t]
            scores(g, t % nslot, base, first)
        for rg in range(n_rg):
            if 1 <= t <= len(blocks):
                softmax_group((t - 1) % nslot, rg)
            if rg == n_rg // 2 and 2 <= t:
                g, base, _, out_rows = blocks[t - 2]
                finish(g, (t - 2) % nslot, base, out_rows)

    def merge(i, carry):
        rows = pl.ds(pl.multiple_of(i * 2 * BLOCK, 2 * BLOCK), 2 * BLOCK)
        ms = [m_ref[g, rows, :] for g in range(N_GROUPS_A)]
        mx = jnp.maximum(jnp.maximum(ms[0], ms[1]), ms[2])
        num = jnp.zeros((2 * BLOCK, LANES), F32)
        den = jnp.zeros((2 * BLOCK, LANES), F32)
        for g in range(N_GROUPS_A):
            w = jnp.exp(ms[g] - mx)
            num = num + w * acc_ref[g, rows, :]
            den = den + w * l_ref[g, rows, :]
        o_ref[0, rows, :] = (num / den).astype(o_ref.dtype)
        return carry

    lax.fori_loop(0, seq // (2 * BLOCK), merge, 0)


def _mixer_a(proj_a, rev_a, weights):
    B, S, _ = proj_a.shape
    npair = OUT_WIDTH_A // LANES
    grid = (B, npair)

    def col(which, g):
        base = (which * WIDTH_A + g * OUT_WIDTH_A) // LANES
        return pl.BlockSpec((1, S, LANES), lambda b, hp: (b, 0, base + hp))

    cast_in, cast_out, cast_shapes = _cast_specs(weights, grid)
    in_specs = [col(which, g) for g in range(N_GROUPS_A) for which in range(3)]
    in_specs.append(pl.BlockSpec((1, 2 * N_GROUPS_A, 4 * BLOCK), lambda b, hp: (hp, 0, 0)))
    return pl.pallas_call(
        functools.partial(_mixer_a_kernel, seq=S, n_cast=len(weights)),
        grid=grid,
        in_specs=in_specs + cast_in,
        out_specs=[pl.BlockSpec((1, S, LANES), lambda b, hp: (b, 0, hp))] + cast_out,
        out_shape=[jax.ShapeDtypeStruct((B, S, OUT_WIDTH_A), BF16)] + cast_shapes,
        scratch_shapes=[pltpu.VMEM((2 * N_GROUPS_A, 2 * BLOCK, 2 * BLOCK), F32),
                        pltpu.VMEM((N_GROUPS_A, BLOCK + S, LANES), BF16),
                        pltpu.VMEM((N_GROUPS_A, BLOCK + S, LANES), BF16),
                        pltpu.VMEM((N_GROUPS_A, BLOCK + S, 2 * LANES), BF16),
                        pltpu.VMEM((4, 2 * BLOCK, 2 * BLOCK), F32),
                        pltpu.VMEM((4, 2 * BLOCK, 2 * BLOCK), BF16),
                        pltpu.VMEM((4, 2 * BLOCK, LANES), F32),
                        pltpu.VMEM((N_GROUPS_A, S, LANES), F32),
                        pltpu.VMEM((N_GROUPS_A, S, LANES), F32),
                        pltpu.VMEM((N_GROUPS_A, S, LANES), F32)],
        compiler_params=_params(("arbitrary", "arbitrary")),
        name="mixer_a",
    )(*([proj_a] * 9), rev_a, *weights)


def _mixer_b_kernel(*refs, tq, seq, lam_init, n_cast):
    nq = seq // tq
    q_ref, k_ref, v_ref, rev_ref, lam_ref, g_ref = refs[:6]
    o_ref = refs[6 + n_cast]
    toep_ref, vx_ref = refs[7 + 2 * n_cast:9 + 2 * n_cast]
    s_refs = refs[9 + 2 * n_cast:9 + 2 * n_cast + nq]
    p_refs = refs[9 + 2 * n_cast + nq:]
    _cast_blocks(refs[6:6 + n_cast], refs[7 + n_cast:7 + 2 * n_cast])
    rg_rows = BF16_ROWS
    n_rg = 2 * tq // rg_rows

    x = jnp.broadcast_to(rev_ref[0] * LOG2E, (tq, seq + tq))
    rolled = pltpu.roll(x, 0, 1, stride=1, stride_axis=0)
    toep_ref[:, :seq] = rolled[:, :seq]
    row = lax.broadcasted_iota(jnp.int32, (tq, tq), 0)
    col = lax.broadcasted_iota(jnp.int32, (tq, tq), 1)
    toep_ref[:, seq:] = jnp.where(col <= row, rolled[:, seq:], NEG_INF)

    vx_ref[:, :LANES] = v_ref[0]
    vx_ref[:, LANES:] = jnp.ones((seq, LANES), BF16)

    lane = lax.broadcasted_iota(jnp.int32, (tq, LANES), 1)
    lo = lane < HEAD_DIM
    lv = lam_ref[...]
    lam = (jnp.exp(jnp.sum(lv[0:1] * lv[1:2], axis=-1, keepdims=True))
           - jnp.exp(jnp.sum(lv[2:3] * lv[3:4], axis=-1, keepdims=True)) + lam_init)

    def score_chunks(qi):
        q = q_ref[0, qi * tq:(qi + 1) * tq, :]
        zero = jnp.zeros_like(q)
        qz = jnp.concatenate([jnp.where(lo, q, zero), jnp.where(lo, zero, q)], axis=0)

        def chunk(c):
            kc = k_ref[0, c * tq:(c + 1) * tq, :]
            s_refs[qi][:, c * tq:(c + 1) * tq] = lax.dot_general(
                qz, kc, (((1,), (1,)), ((), ())), preferred_element_type=F32)

        return [functools.partial(chunk, c) for c in range(qi + 1)]

    def softmax_group(qi, rg):
        rows = slice(rg * rg_rows, (rg + 1) * rg_rows)
        brow = (rg * rg_rows) % tq
        bias = toep_ref[brow:brow + rg_rows, (nq - qi) * tq:(nq + 1) * tq]
        t = s_refs[qi][rows, :] + bias
        m = jnp.max(t, axis=-1, keepdims=True)
        p_refs[qi][rows, :] = jnp.exp2(t - m).astype(BF16)

    def finish(qi):
        width = (qi + 1) * tq
        acc = jnp.dot(p_refs[qi][...], vx_ref[:width, :], preferred_element_type=F32)
        o = acc[:, :LANES] / acc[:, LANES:]
        y = o[:tq] - lam * o[tq:]
        y = _rms(y, g_ref[...]) * (1.0 - lam_init)
        o_ref[0, qi * tq:(qi + 1) * tq, :] = y.astype(o_ref.dtype)

    order = list(range(nq - 1, -1, -1))
    for t in range(nq + 2):
        stages = [score_chunks(order[t]) if t < nq else [],
                  [functools.partial(softmax_group, order[t - 1], rg) for rg in range(n_rg)] if 1 <= t <= nq else [],
                  [functools.partial(finish, order[t - 2])] if t >= 2 else []]
        for emit in _interleave(stages):
            emit()


def _mixer_b(proj, rev_b, lam_vecs, subln_g, lam_init, weights, tq=256):
    B, S, _ = proj.shape
    H = N_HEADS_B
    nq = S // tq
    grid = (B, H)
    kern = functools.partial(_mixer_b_kernel, tq=tq, seq=S, lam_init=lam_init, n_cast=len(weights))

    def col(off):
        return pl.BlockSpec((1, S, LANES), lambda b, h: (b, 0, off // LANES + h))

    cast_in, cast_out, cast_shapes = _cast_specs(weights, grid)
    return pl.pallas_call(
        kern,
        grid=grid,
        in_specs=[col(OFF_QB), col(OFF_KB), col(OFF_VB),
                  pl.BlockSpec((1, 1, S + tq), lambda b, h: (h, 0, 0)),
                  pl.BlockSpec((4, HEAD_DIM), lambda b, h: (0, 0)),
                  pl.BlockSpec((1, 2 * HEAD_DIM), lambda b, h: (0, 0))] + cast_in,
        out_specs=[pl.BlockSpec((1, S, LANES), lambda b, h: (b, 0, h))] + cast_out,
        out_shape=[jax.ShapeDtypeStruct((B, S, WIDTH_B), BF16)] + cast_shapes,
        scratch_shapes=([pltpu.VMEM((tq, S + tq), F32), pltpu.VMEM((S, 2 * LANES), BF16)]
                        + [pltpu.VMEM((2 * tq, (i + 1) * tq), F32) for i in range(nq)]
                        + [pltpu.VMEM((2 * tq, (i + 1) * tq), BF16) for i in range(nq)]),
        compiler_params=_params(("arbitrary", "arbitrary")),
        name="mixer_b",
    )(proj, proj, proj, rev_b, lam_vecs, subln_g, *weights)


def _gate_merge_kernel(ya_ref, yb_ref, ga_ref, gb_ref, wa_ref, wb_ref, out_ref):
    pa = jnp.dot(ya_ref[...], wa_ref[...], preferred_element_type=F32)
    pb = jnp.dot(yb_ref[...], wb_ref[...], preferred_element_type=F32)
    ga = jax.nn.sigmoid(ga_ref[...].astype(F32))
    gb = jax.nn.sigmoid(gb_ref[...].astype(F32))
    out_ref[...] = (ga * pa + gb * pb).astype(out_ref.dtype)


def _gate_merge(ya, yb, proj_r, wa, wb, tm=1024, tn=512):
    T = yb.shape[0]
    D = D_MODEL
    return pl.pallas_call(
        _gate_merge_kernel,
        grid=(T // tm, D // tn),
        in_specs=[pl.BlockSpec((tm, OUT_WIDTH_A), lambda i, j: (i, 0)),
                  pl.BlockSpec((tm, WIDTH_B), lambda i, j: (i, 0)),
                  pl.BlockSpec((tm, tn), lambda i, j: (i, OFF_GA // tn + j)),
                  pl.BlockSpec((tm, tn), lambda i, j: (i, OFF_GB // tn + j)),
                  pl.BlockSpec((OUT_WIDTH_A, tn), lambda i, j: (0, j)),
                  pl.BlockSpec((WIDTH_B, tn), lambda i, j: (0, j))],
        out_specs=pl.BlockSpec((tm, tn), lambda i, j: (i, j)),
        out_shape=jax.ShapeDtypeStruct((T, D), BF16),
        compiler_params=_params(("parallel", "arbitrary")),
        name="gate_merge",
    )(ya, yb, proj_r, proj_r, wa, wb)


def _out_proj_kernel(m_ref, w_ref, x_ref, o_ref):
    o_ref[...] = x_ref[...] + jnp.dot(m_ref[...], w_ref[...], preferred_element_type=F32)


def _out_proj(merged, w, x2, tm=1024, tn=512):
    T, D = x2.shape
    return pl.pallas_call(
        _out_proj_kernel,
        grid=(T // tm, D // tn),
        in_specs=[pl.BlockSpec((tm, D), lambda i, j: (i, 0)),
                  pl.BlockSpec((D, tn), lambda i, j: (0, j)),
                  pl.BlockSpec((tm, tn), lambda i, j: (i, j))],
        out_specs=pl.BlockSpec((tm, tn), lambda i, j: (i, j)),
        out_shape=jax.ShapeDtypeStruct((T, D), F32),
        compiler_params=_params(("parallel", "arbitrary")),
        name="out_proj",
    )(merged, w, x2)


def _ffn_kernel(x_ref, g_ref, wg_ref, wu_ref, wd_ref, gf_ref, o_ref, h_ref):
    f = pl.program_id(1)

    @pl.when(f == 0)
    def _():
        h_ref[...] = _rms(x_ref[...], g_ref[...]).astype(BF16)
        o_ref[...] = x_ref[...]

    h = h_ref[...]
    a = jnp.dot(h, wg_ref[...], preferred_element_type=F32)
    b = jnp.dot(h, wu_ref[...], preferred_element_type=F32)
    u = (a * jax.nn.sigmoid(a)) * b
    o_ref[...] += jnp.dot(u.astype(BF16), wd_ref[...], preferred_element_type=F32)

    @pl.when(f == pl.num_programs(1) - 1)
    def _():
        o_ref[...] = _rms(o_ref[...], gf_ref[...])


def _ffn(x1, g, wg, wu, wd, gf, tm=1024, tf=512):
    T, D = x1.shape
    F = wg.shape[1]
    return pl.pallas_call(
        _ffn_kernel,
        grid=(T // tm, F // tf),
        in_specs=[pl.BlockSpec((tm, D), lambda i, f: (i, 0), pipeline_mode=pl.Buffered(1)),
                  pl.BlockSpec((1, D), lambda i, f: (0, 0)),
                  pl.BlockSpec((D, tf), lambda i, f: (0, f)),
                  pl.BlockSpec((D, tf), lambda i, f: (0, f)),
                  pl.BlockSpec((tf, D), lambda i, f: (f, 0)),
                  pl.BlockSpec((1, D), lambda i, f: (0, 0))],
        out_specs=pl.BlockSpec((tm, D), lambda i, f: (i, 0)),
        out_shape=jax.ShapeDtypeStruct((T, D), F32),
        scratch_shapes=[pltpu.VMEM((tm, D), BF16)],
        compiler_params=_params(("parallel", "arbitrary")),
        name="ffn",
    )(x1, g, wg, wu, wd, gf)


def _rev_a_index():
    u = np.arange(4 * BLOCK)
    rel = np.clip(2 * BLOCK - u, 0, None)
    return np.stack([_rel_bucket_np(rel * d) for _, d in DIL_PATTERNS])


def _rev_b_index(seq, tq):
    c = np.arange(seq + tq)
    return _rel_bucket_np(np.clip(seq - c, 0, seq - 1))


def _lookup(table, idx):
    idx = jnp.asarray(idx)[None, :]
    out = jnp.zeros((table.shape[1], idx.shape[1]), F32)
    for b in range(NUM_BUCKETS):
        out = jnp.where(idx == b, table[b][:, None], out)
    return out


def kernel(x, norm_attn_g, w_in, w_proj_a, w_proj_b, w_out, rel_bias_table, diff_lambda_q1, diff_lambda_k1, diff_lambda_q2, diff_lambda_k2, diff_subln_g, norm_ffn_g, w_ffn_gate, w_ffn_up, w_ffn_down, norm_final_g):
    B, S, D = x.shape
    T = B * S
    depth = w_in.shape[0]
    assert depth == 1, "the final RMSNorm is fused into the FFN epilogue of a single layer"
    table_a = rel_bias_table[:, :N_HEADS_A].astype(F32)
    table_b = rel_bias_table[:, N_HEADS_A:].astype(F32)
    tq = 256

    idx_a = _rev_a_index()
    rev_a = jnp.stack([_lookup(table_a[:, g * HEADS_PER_GROUP_A:(g + 1) * HEADS_PER_GROUP_A], idx_a[g])
                       for g in range(N_GROUPS_A)])
    npair = OUT_WIDTH_A // LANES
    rev_a = jnp.transpose(rev_a.reshape(N_GROUPS_A, npair, 2, 4 * BLOCK), (1, 0, 2, 3))
    rev_a = rev_a.reshape(npair, 2 * N_GROUPS_A, 4 * BLOCK)
    rev_b = _lookup(table_b, _rev_b_index(S, tq))[:, None, :]

    x2 = x.reshape(T, D)
    l = 0
    lam_init = 0.8 - 0.6 * math.exp(-0.3 * l)
    proj_a, proj_r = _in_proj(x2, norm_attn_g[l][None, :], w_in[l])
    ya, wa, wb, wo = _mixer_a(proj_a.reshape(B, S, PROJ_A), rev_a, [w_proj_a[l], w_proj_b[l], w_out[l]])

    lam_vecs = jnp.stack([diff_lambda_q1[l], diff_lambda_k1[l],
                          diff_lambda_q2[l], diff_lambda_k2[l]]).astype(F32)
    yb, wg, wu, wd = _mixer_b(proj_r.reshape(B, S, PROJ_R), rev_b, lam_vecs, diff_subln_g[l][None, :].astype(F32),
                              lam_init, [w_ffn_gate[l], w_ffn_up[l], w_ffn_down[l]], tq=tq)

    merged = _gate_merge(ya.reshape(T, OUT_WIDTH_A), yb.reshape(T, WIDTH_B), proj_r, wa, wb)
    x2 = _out_proj(merged, wo, x2)
    x2 = _ffn(x2, norm_ffn_g[l][None, :], wg, wu, wd, norm_final_g[None, :])
    return x2.reshape(B, S, D)
```

```python
import functools
import math

import numpy as np
import jax
import jax.numpy as jnp
from jax import lax
from jax.experimental import pallas as pl
from jax.experimental.pallas import tpu as pltpu

D_MODEL = 2048
HEAD_DIM = 64
DIL_PATTERNS = ((128, 1), (512, 4), (2048, 16))
N_GROUPS_A = len(DIL_PATTERNS)
HEADS_PER_GROUP_A = 8
N_HEADS_A = N_GROUPS_A * HEADS_PER_GROUP_A
WIDTH_A = N_HEADS_A * HEAD_DIM
OUT_WIDTH_A = HEADS_PER_GROUP_A * HEAD_DIM
BLOCK = 128
N_HEADS_B = D_MODEL // (2 * HEAD_DIM)
WIDTH_B = N_HEADS_B * 2 * HEAD_DIM
NUM_BUCKETS = 32
MAX_DISTANCE = 2048
D_FF = -(-8 * D_MODEL // (3 * 256)) * 256
PROJ_A = 3 * WIDTH_A
PROJ_R = 3 * WIDTH_B + 2 * D_MODEL
D_IN = PROJ_A + PROJ_R
NORM_EPS = 1e-6
NEG_INF = -1e30
SCALE = HEAD_DIM ** -0.5
LOG2E = math.log2(math.e)

OFF_QB = 0
OFF_KB = WIDTH_B
OFF_VB = 2 * WIDTH_B
OFF_GA = 3 * WIDTH_B
OFF_GB = OFF_GA + D_MODEL

LANES = 128
VMEM_LIMIT = 56 * 1024 * 1024
VMEM_LIMIT_ATTN_OUT = 58 * 1024 * 1024

BF16 = jnp.bfloat16
F32 = jnp.float32


def _rel_bucket_np(dist):
    n = np.maximum(dist, 0)
    max_exact = NUM_BUCKETS // 2
    nf = np.maximum(n, 1).astype(np.float32)
    large = max_exact + (np.log(nf / np.float32(max_exact)) / np.float32(math.log(MAX_DISTANCE / max_exact))
                         * np.float32(NUM_BUCKETS - max_exact)).astype(np.int32)
    large = np.minimum(large, NUM_BUCKETS - 1)
    return np.where(n < max_exact, n, large).astype(np.int32)


def _rms(x, g):
    ms = jnp.mean(x * x, axis=-1, keepdims=True)
    return x * lax.rsqrt(ms + NORM_EPS) * g


def _params(sem, vmem=VMEM_LIMIT):
    return pltpu.CompilerParams(dimension_semantics=sem, vmem_limit_bytes=vmem)


BF16_ROWS = 16


def _cast_specs(weights, grid):
    nsteps = math.prod(grid)
    in_specs, out_specs, out_shapes = [], [], []
    for w in weights:
        rows, cols = w.shape
        blk = next(r for r in range(BF16_ROWS, rows + 1, BF16_ROWS)
                   if rows % r == 0 and nsteps % (rows // r) == 0 and rows // r <= nsteps)
        per = nsteps // (rows // blk)

        def index(*ids, per=per):
            step = 0
            for i, n in zip(ids, grid):
                step = step * n + i
            return (step // per, 0)

        in_specs.append(pl.BlockSpec((blk, cols), index))
        out_specs.append(pl.BlockSpec((blk, cols), index))
        out_shapes.append(jax.ShapeDtypeStruct((rows, cols), BF16))
    return in_specs, out_specs, out_shapes


def _interleave(lists):
    keyed = [((i + 0.5) / len(items), n, i, item) for n, items in enumerate(lists) for i, item in enumerate(items)]
    return [item for _, _, _, item in sorted(keyed, key=lambda k: k[:3])]


def _cast_blocks(in_refs, out_refs):
    for i_ref, o_ref in zip(in_refs, out_refs):
        o_ref[...] = i_ref[...].astype(BF16)


def _in_proj_kernel(x_ref, g_ref, w_ref, oa_ref, or_ref, h_ref, *, na, nqb):
    j = pl.program_id(1)

    @pl.when(j == 0)
    def _():
        h_ref[...] = _rms(x_ref[...], g_ref[...]).astype(BF16)

    @pl.when(j < na)
    def _():
        oa_ref[...] = jnp.dot(h_ref[...], w_ref[...].astype(BF16), preferred_element_type=F32)

    @pl.when(j >= na)
    def _():
        scale = jnp.where(j < na + nqb, jnp.float32(SCALE * LOG2E), jnp.float32(1.0))
        res = jnp.dot(h_ref[...], w_ref[...].astype(BF16), preferred_element_type=F32)
        or_ref[...] = (res * scale).astype(BF16)


def _in_proj(x2, g, w, tm=2048, tn=512):
    T, D = x2.shape
    na = PROJ_A // tn
    assert OFF_QB == 0 and WIDTH_B % tn == 0
    return pl.pallas_call(
        functools.partial(_in_proj_kernel, na=na, nqb=WIDTH_B // tn),
        grid=(T // tm, D_IN // tn),
        in_specs=[pl.BlockSpec((tm, D), lambda i, j: (i, 0), pipeline_mode=pl.Buffered(1)),
                  pl.BlockSpec((1, D), lambda i, j: (0, 0)),
                  pl.BlockSpec((D, tn), lambda i, j: (0, j))],
        out_specs=[pl.BlockSpec((tm, tn), lambda i, j: (i, jnp.minimum(j, na - 1))),
                   pl.BlockSpec((tm, tn), lambda i, j: (i, jnp.maximum(j - na, 0)))],
        out_shape=[jax.ShapeDtypeStruct((T, PROJ_A), F32), jax.ShapeDtypeStruct((T, PROJ_R), BF16)],
        scratch_shapes=[pltpu.VMEM((tm, D), BF16)],
        compiler_params=_params(("parallel", "arbitrary")),
        name="in_proj",
    )(x2, g, w)


def _mixer_a_kernel(*refs, seq, n_cast):
    qkv = (refs[0:3], refs[3:6], refs[6:9])
    rev_ref = refs[9]
    o_ref = refs[10 + n_cast]
    (bias_ref, q_st, k_st, vx_st, s_ref, p_ref, mrow_ref, m_ref, l_ref, acc_ref) = refs[11 + 2 * n_cast:]
    _cast_blocks(refs[10:10 + n_cast], refs[11 + n_cast:11 + 2 * n_cast])
    lane = lax.broadcasted_iota(jnp.int32, (BLOCK, LANES), 1)
    lo = lane < HEAD_DIM

    row = lax.broadcasted_iota(jnp.int32, (2 * BLOCK, 2 * BLOCK), 0)
    col = lax.broadcasted_iota(jnp.int32, (2 * BLOCK, 2 * BLOCK), 1)
    rel = BLOCK + (row & (BLOCK - 1)) - col
    band = (rel >= 0) & (rel <= BLOCK)
    band_first = band & (col >= BLOCK)
    for g in range(N_GROUPS_A):
        halves = []
        for hh in range(2):
            x = jnp.broadcast_to(rev_ref[0, g * 2 + hh:g * 2 + hh + 1, :], (BLOCK, 4 * BLOCK))
            halves.append(pltpu.roll(x, 0, 1, stride=1, stride_axis=0)[:, BLOCK:3 * BLOCK])
        toep = jnp.concatenate(halves, axis=0)
        bias_ref[2 * g] = jnp.where(band, toep, NEG_INF)
        bias_ref[2 * g + 1] = jnp.where(band_first, toep, NEG_INF)

    for g in range(N_GROUPS_A):
        k_st[g, 0:BLOCK, :] = jnp.zeros((BLOCK, LANES), BF16)
        vx_st[g, 0:BLOCK, :] = jnp.zeros((BLOCK, 2 * LANES), BF16)
        vx_st[g, :, LANES:] = jnp.ones((BLOCK + seq, LANES), BF16)

    def scores(g, slot, base, first):
        q = q_st[g, base:base + BLOCK, :]
        zero = jnp.zeros_like(q)
        qz = jnp.concatenate([jnp.where(lo, q, zero), jnp.where(lo, zero, q)], axis=0)
        kw = k_st[g, base - BLOCK:base + BLOCK, :]
        s = lax.dot_general(qz, kw, (((1,), (1,)), ((), ())), preferred_element_type=F32)
        s_ref[slot] = s + bias_ref[2 * g + (1 if first else 0)]

    def softmax_group(slot, rg):
        rows = slice(rg * BF16_ROWS, (rg + 1) * BF16_ROWS)
        m = jnp.max(s_ref[slot, rows, :], axis=-1, keepdims=True)
        p_ref[slot, rows, :] = jnp.exp(s_ref[slot, rows, :] - m).astype(BF16)
        mrow_ref[slot, rows, :] = jnp.broadcast_to(m, (BF16_ROWS, LANES))

    def finish(g, slot, base, out_rows):
        acc = jnp.dot(p_ref[slot], vx_st[g, base - BLOCK:base + BLOCK, :], preferred_element_type=F32)
        mrow = mrow_ref[slot]
        m_ref[g, out_rows, :] = jnp.where(lo, mrow[:BLOCK], mrow[BLOCK:])
        l_ref[g, out_rows, :] = jnp.where(lo, acc[:BLOCK, LANES:], acc[BLOCK:, LANES:])
        acc_ref[g, out_rows, :] = jnp.where(lo, acc[:BLOCK, :LANES], acc[BLOCK:, :LANES])

    n_rg = 2 * BLOCK // BF16_ROWS
    blocks = []
    for g, (_, d) in enumerate(DIL_PATTERNS):
        sub_len = seq // d
        q_ref, k_ref, v_ref = qkv[g]
        for r in range(d):
            src = pl.ds(r, sub_len, stride=d) if d > 1 else pl.ds(0, seq)
            dst = slice(BLOCK + r * sub_len, BLOCK + (r + 1) * sub_len)
            q_st[g, dst, :] = (q_ref[0, src, :] * SCALE).astype(BF16)
            k_st[g, dst, :] = k_ref[0, src, :].astype(BF16)
            vx_st[g, dst, :LANES] = v_ref[0, src, :].astype(BF16)
            for n in range(sub_len // BLOCK):
                base = BLOCK + r * sub_len + n * BLOCK
                out_rows = pl.ds(n * BLOCK * d + r, BLOCK, stride=d) if d > 1 else pl.ds(n * BLOCK, BLOCK)
                blocks.append((g, base, n == 0, out_rows))

    nslot = s_ref.shape[0]
    for t in range(len(blocks) + 2):
        if t < len(blocks):
            g, base, first, _ = blocks[t]
            scores(g, t % nslot, base, first)
        for rg in range(n_rg):
            if 1 <= t <= len(blocks):
                softmax_group((t - 1) % nslot, rg)
            if rg == n_rg // 2 and 2 <= t:
                g, base, _, out_rows = blocks[t - 2]
                finish(g, (t - 2) % nslot, base, out_rows)

    def merge(i, carry):
        rows = pl.ds(pl.multiple_of(i * 2 * BLOCK, 2 * BLOCK), 2 * BLOCK)
        ms = [m_ref[g, rows, :] for g in range(N_GROUPS_A)]
        mx = jnp.maximum(jnp.maximum(ms[0], ms[1]), ms[2])
        num = jnp.zeros((2 * BLOCK, LANES), F32)
        den = jnp.zeros((2 * BLOCK, LANES), F32)
        for g in range(N_GROUPS_A):
            w = jnp.exp(ms[g] - mx)
            num = num + w * acc_ref[g, rows, :]
            den = den + w * l_ref[g, rows, :]
        o_ref[0, rows, :] = (num / den).astype(o_ref.dtype)
        return carry

    lax.fori_loop(0, seq // (2 * BLOCK), merge, 0)


def _mixer_a(proj_a, rev_a, weights):
    B, S, _ = proj_a.shape
    npair = OUT_WIDTH_A // LANES
    grid = (B, npair)

    def col(which, g):
        base = (which * WIDTH_A + g * OUT_WIDTH_A) // LANES
        return pl.BlockSpec((1, S, LANES), lambda b, hp: (b, 0, base + hp))

    cast_in, cast_out, cast_shapes = _cast_specs(weights, grid)
    in_specs = [col(which, g) for g in range(N_GROUPS_A) for which in range(3)]
    in_specs.append(pl.BlockSpec((1, 2 * N_GROUPS_A, 4 * BLOCK), lambda b, hp: (hp, 0, 0)))
    return pl.pallas_call(
        functools.partial(_mixer_a_kernel, seq=S, n_cast=len(weights)),
        grid=grid,
        in_specs=in_specs + cast_in,
        out_specs=[pl.BlockSpec((1, S, LANES), lambda b, hp: (b, 0, hp))] + cast_out,
        out_shape=[jax.ShapeDtypeStruct((B, S, OUT_WIDTH_A), BF16)] + cast_shapes,
        scratch_shapes=[pltpu.VMEM((2 * N_GROUPS_A, 2 * BLOCK, 2 * BLOCK), F32),
                        pltpu.VMEM((N_GROUPS_A, BLOCK + S, LANES), BF16),
                        pltpu.VMEM((N_GROUPS_A, BLOCK + S, LANES), BF16),
                        pltpu.VMEM((N_GROUPS_A, BLOCK + S, 2 * LANES), BF16),
                        pltpu.VMEM((4, 2 * BLOCK, 2 * BLOCK), F32),
                        pltpu.VMEM((4, 2 * BLOCK, 2 * BLOCK), BF16),
                        pltpu.VMEM((4, 2 * BLOCK, LANES), F32),
                        pltpu.VMEM((N_GROUPS_A, S, LANES), F32),
                        pltpu.VMEM((N_GROUPS_A, S, LANES), F32),
                        pltpu.VMEM((N_GROUPS_A, S, LANES), F32)],
        compiler_params=_params(("arbitrary", "arbitrary")),
        name="mixer_a",
    )(*([proj_a] * 9), rev_a, *weights)


def _mixer_b_kernel(*refs, tq, seq, lam_init, n_cast):
    nq = seq // tq
    q_ref, k_ref, v_ref, rev_ref, lam_ref, g_ref = refs[:6]
    o_ref = refs[6 + n_cast]
    toep_ref, vx_ref = refs[7 + 2 * n_cast:9 + 2 * n_cast]
    s_refs = refs[9 + 2 * n_cast:9 + 2 * n_cast + nq]
    p_refs = refs[9 + 2 * n_cast + nq:]
    _cast_blocks(refs[6:6 + n_cast], refs[7 + n_cast:7 + 2 * n_cast])
    rg_rows = BF16_ROWS
    n_rg = 2 * tq // rg_rows

    x = jnp.broadcast_to(rev_ref[0] * LOG2E, (tq, seq + tq))
    rolled = pltpu.roll(x, 0, 1, stride=1, stride_axis=0)
    toep_ref[:, :seq] = rolled[:, :seq]
    row = lax.broadcasted_iota(jnp.int32, (tq, tq), 0)
    col = lax.broadcasted_iota(jnp.int32, (tq, tq), 1)
    toep_ref[:, seq:] = jnp.where(col <= row, rolled[:, seq:], NEG_INF)

    vx_ref[:, :LANES] = v_ref[0]
    vx_ref[:, LANES:] = jnp.ones((seq, LANES), BF16)

    lane = lax.broadcasted_iota(jnp.int32, (tq, LANES), 1)
    lo = lane < HEAD_DIM
    lv = lam_ref[...]
    lam = (jnp.exp(jnp.sum(lv[0:1] * lv[1:2], axis=-1, keepdims=True))
           - jnp.exp(jnp.sum(lv[2:3] * lv[3:4], axis=-1, keepdims=True)) + lam_init)

    def score_chunks(qi):
        q = q_ref[0, qi * tq:(qi + 1) * tq, :]
        zero = jnp.zeros_like(q)
        qz = jnp.concatenate([jnp.where(lo, q, zero), jnp.where(lo, zero, q)], axis=0)

        def chunk(c):
            kc = k_ref[0, c * tq:(c + 1) * tq, :]
            s_refs[qi][:, c * tq:(c + 1) * tq] = lax.dot_general(
                qz, kc, (((1,), (1,)), ((), ())), preferred_element_type=F32)

        return [functools.partial(chunk, c) for c in range(qi + 1)]

    def softmax_group(qi, rg):
        rows = slice(rg * rg_rows, (rg + 1) * rg_rows)
        brow = (rg * rg_rows) % tq
        bias = toep_ref[brow:brow + rg_rows, (nq - qi) * tq:(nq + 1) * tq]
        t = s_refs[qi][rows, :] + bias
        m = jnp.max(t, axis=-1, keepdims=True)
        p_refs[qi][rows, :] = jnp.exp2(t - m).astype(BF16)

    def finish(qi):
        width = (qi + 1) * tq
        acc = jnp.dot(p_refs[qi][...], vx_ref[:width, :], preferred_element_type=F32)
        o = acc[:, :LANES] / acc[:, LANES:]
        y = o[:tq] - lam * o[tq:]
        y = _rms(y, g_ref[...]) * (1.0 - lam_init)
        o_ref[0, qi * tq:(qi + 1) * tq, :] = y.astype(o_ref.dtype)

    order = list(range(nq - 1, -1, -1))
    for t in range(nq + 2):
        stages = [score_chunks(order[t]) if t < nq else [],
                  [functools.partial(softmax_group, order[t - 1], rg) for rg in range(n_rg)] if 1 <= t <= nq else [],
                  [functools.partial(finish, order[t - 2])] if t >= 2 else []]
        for emit in _interleave(stages):
            emit()


def _mixer_b(proj, rev_b, lam_vecs, subln_g, lam_init, weights, tq=256):
    B, S, _ = proj.shape
    H = N_HEADS_B
    nq = S // tq
    grid = (B, H)
    kern = functools.partial(_mixer_b_kernel, tq=tq, seq=S, lam_init=lam_init, n_cast=len(weights))

    def col(off):
        return pl.BlockSpec((1, S, LANES), lambda b, h: (b, 0, off // LANES + h))

    cast_in, cast_out, cast_shapes = _cast_specs(weights, grid)
    return pl.pallas_call(
        kern,
        grid=grid,
        in_specs=[col(OFF_QB), col(OFF_KB), col(OFF_VB),
                  pl.BlockSpec((1, 1, S + tq), lambda b, h: (h, 0, 0)),
                  pl.BlockSpec((4, HEAD_DIM), lambda b, h: (0, 0)),
                  pl.BlockSpec((1, 2 * HEAD_DIM), lambda b, h: (0, 0))] + cast_in,
        out_specs=[pl.BlockSpec((1, S, LANES), lambda b, h: (b, 0, h))] + cast_out,
        out_shape=[jax.ShapeDtypeStruct((B, S, WIDTH_B), BF16)] + cast_shapes,
        scratch_shapes=([pltpu.VMEM((tq, S + tq), F32), pltpu.VMEM((S, 2 * LANES), BF16)]
                        + [pltpu.VMEM((2 * tq, (i + 1) * tq), F32) for i in range(nq)]
                        + [pltpu.VMEM((2 * tq, (i + 1) * tq), BF16) for i in range(nq)]),
        compiler_params=_params(("arbitrary", "arbitrary")),
        name="mixer_b",
    )(proj, proj, proj, rev_b, lam_vecs, subln_g, *weights)


def _attn_out_kernel(ya_ref, yb_ref, ga_ref, gb_ref, x_ref, wa_ref, wb_ref, wo_ref, g_ref, x1_ref, h_ref):
    pa = jnp.dot(ya_ref[...], wa_ref[...], preferred_element_type=F32)
    pb = jnp.dot(yb_ref[...], wb_ref[...], preferred_element_type=F32)
    ga = jax.nn.sigmoid(ga_ref[...].astype(F32))
    gb = jax.nn.sigmoid(gb_ref[...].astype(F32))
    merged = (ga * pa + gb * pb).astype(BF16)
    x1_ref[...] = x_ref[...] + jnp.dot(merged, wo_ref[...], preferred_element_type=F32)
    h_ref[...] = _rms(x1_ref[...], g_ref[...]).astype(BF16)


def _attn_out(ya, yb, proj_r, x2, wa, wb, wo, g, tm=512):
    T, D = x2.shape

    def resident(shape):
        return pl.BlockSpec(shape, lambda i: (0, 0), pipeline_mode=pl.Buffered(1))

    return pl.pallas_call(
        _attn_out_kernel,
        grid=(T // tm,),
        in_specs=[pl.BlockSpec((tm, OUT_WIDTH_A), lambda i: (i, 0)),
                  pl.BlockSpec((tm, WIDTH_B), lambda i: (i, 0)),
                  pl.BlockSpec((tm, D), lambda i: (i, OFF_GA // D)),
                  pl.BlockSpec((tm, D), lambda i: (i, OFF_GB // D)),
                  pl.BlockSpec((tm, D), lambda i: (i, 0)),
                  resident((OUT_WIDTH_A, D)), resident((WIDTH_B, D)), resident((D, D)), resident((1, D))],
        out_specs=[pl.BlockSpec((tm, D), lambda i: (i, 0)), pl.BlockSpec((tm, D), lambda i: (i, 0))],
        out_shape=[jax.ShapeDtypeStruct((T, D), F32), jax.ShapeDtypeStruct((T, D), BF16)],
        compiler_params=_params(("parallel",), vmem=VMEM_LIMIT_ATTN_OUT),
        name="attn_out",
    )(ya, yb, proj_r, proj_r, x2, wa, wb, wo, g)


def _ffn_kernel(x_ref, h_ref, wg_ref, wu_ref, wd_ref, gf_ref, o_ref):
    f = pl.program_id(1)

    @pl.when(f == 0)
    def _():
        o_ref[...] = x_ref[...]

    h = h_ref[...]
    a = jnp.dot(h, wg_ref[...], preferred_element_type=F32)
    b = jnp.dot(h, wu_ref[...], preferred_element_type=F32)
    u = (a * jax.nn.sigmoid(a)) * b
    o_ref[...] += jnp.dot(u.astype(BF16), wd_ref[...], preferred_element_type=F32)

    @pl.when(f == pl.num_programs(1) - 1)
    def _():
        o_ref[...] = _rms(o_ref[...], gf_ref[...])


def _ffn(x1, h, wg, wu, wd, gf, tm=1024, tf=512):
    T, D = x1.shape
    F = wg.shape[1]
    return pl.pallas_call(
        _ffn_kernel,
        grid=(T // tm, F // tf),
        in_specs=[pl.BlockSpec((tm, D), lambda i, f: (i, 0), pipeline_mode=pl.Buffered(1)),
                  pl.BlockSpec((tm, D), lambda i, f: (i, 0), pipeline_mode=pl.Buffered(1)),
                  pl.BlockSpec((D, tf), lambda i, f: (0, f)),
                  pl.BlockSpec((D, tf), lambda i, f: (0, f)),
                  pl.BlockSpec((tf, D), lambda i, f: (f, 0)),
                  pl.BlockSpec((1, D), lambda i, f: (0, 0))],
        out_specs=pl.BlockSpec((tm, D), lambda i, f: (i, 0)),
        out_shape=jax.ShapeDtypeStruct((T, D), F32),
        compiler_params=_params(("parallel", "arbitrary")),
        name="ffn",
    )(x1, h, wg, wu, wd, gf)


def _rev_a_index():
    u = np.arange(4 * BLOCK)
    rel = np.clip(2 * BLOCK - u, 0, None)
    return np.stack([_rel_bucket_np(rel * d) for _, d in DIL_PATTERNS])


def _rev_b_index(seq, tq):
    c = np.arange(seq + tq)
    return _rel_bucket_np(np.clip(seq - c, 0, seq - 1))


def _lookup(table, idx):
    idx = jnp.asarray(idx)[None, :]
    out = jnp.zeros((table.shape[1], idx.shape[1]), F32)
    for b in range(NUM_BUCKETS):
        out = jnp.where(idx == b, table[b][:, None], out)
    return out


def kernel(x, norm_attn_g, w_in, w_proj_a, w_proj_b, w_out, rel_bias_table, diff_lambda_q1, diff_lambda_k1, diff_lambda_q2, diff_lambda_k2, diff_subln_g, norm_ffn_g, w_ffn_gate, w_ffn_up, w_ffn_down, norm_final_g):
    B, S, D = x.shape
    T = B * S
    depth = w_in.shape[0]
    assert depth == 1, "the final RMSNorm is fused into the FFN epilogue of a single layer"
    table_a = rel_bias_table[:, :N_HEADS_A].astype(F32)
    table_b = rel_bias_table[:, N_HEADS_A:].astype(F32)
    tq = 256

    idx_a = _rev_a_index()
    rev_a = jnp.stack([_lookup(table_a[:, g * HEADS_PER_GROUP_A:(g + 1) * HEADS_PER_GROUP_A], idx_a[g])
                       for g in range(N_GROUPS_A)])
    npair = OUT_WIDTH_A // LANES
    rev_a = jnp.transpose(rev_a.reshape(N_GROUPS_A, npair, 2, 4 * BLOCK), (1, 0, 2, 3))
    rev_a = rev_a.reshape(npair, 2 * N_GROUPS_A, 4 * BLOCK)
    rev_b = _lookup(table_b, _rev_b_index(S, tq))[:, None, :]

    x2 = x.reshape(T, D)
    l = 0
    lam_init = 0.8 - 0.6 * math.exp(-0.3 * l)
    proj_a, proj_r = _in_proj(x2, norm_attn_g[l][None, :], w_in[l])
    ya, wa, wb, wo = _mixer_a(proj_a.reshape(B, S, PROJ_A), rev_a, [w_proj_a[l], w_proj_b[l], w_out[l]])

    lam_vecs = jnp.stack([diff_lambda_q1[l], diff_lambda_k1[l],
                          diff_lambda_q2[l], diff_lambda_k2[l]]).astype(F32)
    yb, wg, wu, wd = _mixer_b(proj_r.reshape(B, S, PROJ_R), rev_b, lam_vecs, diff_subln_g[l][None, :].astype(F32),
                              lam_init, [w_ffn_gate[l], w_ffn_up[l], w_ffn_down[l]], tq=tq)

    x1, h = _attn_out(ya.reshape(T, OUT_WIDTH_A), yb.reshape(T, WIDTH_B), proj_r, x2, wa, wb, wo,
                      norm_ffn_g[l][None, :])
    out = _ffn(x1, h, wg, wu, wd, norm_final_g[None, :])
    return out.reshape(B, S, D)
```

```python
import functools
import math

import numpy as np
import jax
import jax.numpy as jnp
from jax import lax
from jax.experimental import pallas as pl
from jax.experimental.pallas import tpu as pltpu

D_MODEL = 2048
HEAD_DIM = 64
DIL_PATTERNS = ((128, 1), (512, 4), (2048, 16))
N_GROUPS_A = len(DIL_PATTERNS)
HEADS_PER_GROUP_A = 8
N_HEADS_A = N_GROUPS_A * HEADS_PER_GROUP_A
WIDTH_A = N_HEADS_A * HEAD_DIM
OUT_WIDTH_A = HEADS_PER_GROUP_A * HEAD_DIM
BLOCK = 128
N_HEADS_B = D_MODEL // (2 * HEAD_DIM)
WIDTH_B = N_HEADS_B * 2 * HEAD_DIM
NUM_BUCKETS = 32
MAX_DISTANCE = 2048
D_FF = -(-8 * D_MODEL // (3 * 256)) * 256
PROJ_A = 3 * WIDTH_A
PROJ_R = 3 * WIDTH_B + 2 * D_MODEL
D_IN = PROJ_A + PROJ_R
NORM_EPS = 1e-6
NEG_INF = -1e30
SCALE = HEAD_DIM ** -0.5
LOG2E = math.log2(math.e)

OFF_QB = 0
OFF_KB = WIDTH_B
OFF_VB = 2 * WIDTH_B
OFF_GA = 3 * WIDTH_B
OFF_GB = OFF_GA + D_MODEL

LANES = 128
VMEM_LIMIT = 56 * 1024 * 1024
VMEM_LIMIT_ATTN_OUT = 58 * 1024 * 1024
VMEM_LIMIT_FFN = 60 * 1024 * 1024

BF16 = jnp.bfloat16
F32 = jnp.float32


def _rel_bucket_np(dist):
    n = np.maximum(dist, 0)
    max_exact = NUM_BUCKETS // 2
    nf = np.maximum(n, 1).astype(np.float32)
    large = max_exact + (np.log(nf / np.float32(max_exact)) / np.float32(math.log(MAX_DISTANCE / max_exact))
                         * np.float32(NUM_BUCKETS - max_exact)).astype(np.int32)
    large = np.minimum(large, NUM_BUCKETS - 1)
    return np.where(n < max_exact, n, large).astype(np.int32)


def _rms(x, g):
    ms = jnp.mean(x * x, axis=-1, keepdims=True)
    return x * lax.rsqrt(ms + NORM_EPS) * g


def _params(sem, vmem=VMEM_LIMIT):
    return pltpu.CompilerParams(dimension_semantics=sem, vmem_limit_bytes=vmem)


BF16_ROWS = 16


def _cast_specs(weights, grid):
    nsteps = math.prod(grid)
    in_specs, out_specs, out_shapes = [], [], []
    for w in weights:
        rows, cols = w.shape
        blk = next(r for r in range(BF16_ROWS, rows + 1, BF16_ROWS)
                   if rows % r == 0 and nsteps % (rows // r) == 0 and rows // r <= nsteps)
        per = nsteps // (rows // blk)

        def index(*ids, per=per):
            step = 0
            for i, n in zip(ids, grid):
                step = step * n + i
            return (step // per, 0)

        in_specs.append(pl.BlockSpec((blk, cols), index))
        out_specs.append(pl.BlockSpec((blk, cols), index))
        out_shapes.append(jax.ShapeDtypeStruct((rows, cols), BF16))
    return in_specs, out_specs, out_shapes


def _interleave(lists):
    keyed = [((i + 0.5) / len(items), n, i, item) for n, items in enumerate(lists) for i, item in enumerate(items)]
    return [item for _, _, _, item in sorted(keyed, key=lambda k: k[:3])]


def _cast_blocks(in_refs, out_refs):
    for i_ref, o_ref in zip(in_refs, out_refs):
        o_ref[...] = i_ref[...].astype(BF16)


def _in_proj_kernel(x_hbm, g_ref, w_ref, oa_ref, or_ref, h_ref, x_buf, x_sem, *, na, nqb, tm):
    i = pl.program_id(0)
    j = pl.program_id(1)

    def x_copy(tile):
        return pltpu.make_async_copy(x_hbm.at[pl.ds(tile * tm, tm), :], x_buf, x_sem)

    @pl.when(j == 0)
    def _():
        @pl.when(i == 0)
        def _():
            x_copy(0).start()

        x_copy(i).wait()
        h_ref[...] = _rms(x_buf[...], g_ref[...]).astype(BF16)

        @pl.when(i + 1 < pl.num_programs(0))
        def _():
            x_copy(i + 1).start()

    @pl.when(j < na)
    def _():
        oa_ref[...] = jnp.dot(h_ref[...], w_ref[...].astype(BF16), preferred_element_type=F32)

    @pl.when(j >= na)
    def _():
        scale = jnp.where(j < na + nqb, jnp.float32(SCALE * LOG2E), jnp.float32(1.0))
        res = jnp.dot(h_ref[...], w_ref[...].astype(BF16), preferred_element_type=F32)
        or_ref[...] = (res * scale).astype(BF16)


def _in_proj(x2, g, w, tm=2048, tn=512):
    T, D = x2.shape
    na = PROJ_A // tn
    assert OFF_QB == 0 and WIDTH_B % tn == 0
    return pl.pallas_call(
        functools.partial(_in_proj_kernel, na=na, nqb=WIDTH_B // tn, tm=tm),
        grid=(T // tm, D_IN // tn),
        in_specs=[pl.BlockSpec(memory_space=pl.ANY),
                  pl.BlockSpec((1, D), lambda i, j: (0, 0)),
                  pl.BlockSpec((D, tn), lambda i, j: (0, j))],
        out_specs=[pl.BlockSpec((tm, tn), lambda i, j: (i, jnp.minimum(j, na - 1))),
                   pl.BlockSpec((tm, tn), lambda i, j: (i, jnp.maximum(j - na, 0)))],
        out_shape=[jax.ShapeDtypeStruct((T, PROJ_A), F32), jax.ShapeDtypeStruct((T, PROJ_R), BF16)],
        scratch_shapes=[pltpu.VMEM((tm, D), BF16), pltpu.VMEM((tm, D), F32), pltpu.SemaphoreType.DMA(())],
        compiler_params=_params(("arbitrary", "arbitrary")),
        name="in_proj",
    )(x2, g, w)


def _mixer_a_kernel(*refs, seq, n_cast):
    qkv = (refs[0:3], refs[3:6], refs[6:9])
    rev_ref = refs[9]
    o_ref = refs[10 + n_cast]
    (bias_ref, q_st, k_st, vx_st, s_ref, p_ref, mrow_ref, m_ref, l_ref, acc_ref) = refs[11 + 2 * n_cast:]
    _cast_blocks(refs[10:10 + n_cast], refs[11 + n_cast:11 + 2 * n_cast])
    lane = lax.broadcasted_iota(jnp.int32, (BLOCK, LANES), 1)
    lo = lane < HEAD_DIM

    row = lax.broadcasted_iota(jnp.int32, (2 * BLOCK, 2 * BLOCK), 0)
    col = lax.broadcasted_iota(jnp.int32, (2 * BLOCK, 2 * BLOCK), 1)
    rel = BLOCK + (row & (BLOCK - 1)) - col
    band = (rel >= 0) & (rel <= BLOCK)
    band_first = band & (col >= BLOCK)
    for g in range(N_GROUPS_A):
        halves = []
        for hh in range(2):
            x = jnp.broadcast_to(rev_ref[0, g * 2 + hh:g * 2 + hh + 1, :], (BLOCK, 4 * BLOCK))
            halves.append(pltpu.roll(x, 0, 1, stride=1, stride_axis=0)[:, BLOCK:3 * BLOCK])
        toep = jnp.concatenate(halves, axis=0)
        bias_ref[2 * g] = jnp.where(band, toep, NEG_INF)
        bias_ref[2 * g + 1] = jnp.where(band_first, toep, NEG_INF)

    for g in range(N_GROUPS_A):
        k_st[g, 0:BLOCK, :] = jnp.zeros((BLOCK, LANES), BF16)
        vx_st[g, 0:BLOCK, :] = jnp.zeros((BLOCK, 2 * LANES), BF16)
        vx_st[g, :, LANES:] = jnp.ones((BLOCK + seq, LANES), BF16)

    def scores(g, slot, base, first):
        q = q_st[g, base:base + BLOCK, :]
        zero = jnp.zeros_like(q)
        qz = jnp.concatenate([jnp.where(lo, q, zero), jnp.where(lo, zero, q)], axis=0)
        kw = k_st[g, base - BLOCK:base + BLOCK, :]
        s = lax.dot_general(qz, kw, (((1,), (1,)), ((), ())), preferred_element_type=F32)
        s_ref[slot] = s + bias_ref[2 * g + (1 if first else 0)]

    def softmax_group(slot, rg):
        rows = slice(rg * BF16_ROWS, (rg + 1) * BF16_ROWS)
        m = jnp.max(s_ref[slot, rows, :], axis=-1, keepdims=True)
        p_ref[slot, rows, :] = jnp.exp(s_ref[slot, rows, :] - m).astype(BF16)
        mrow_ref[slot, rows, :] = jnp.broadcast_to(m, (BF16_ROWS, LANES))

    def finish(g, slot, base, out_rows):
        acc = jnp.dot(p_ref[slot], vx_st[g, base - BLOCK:base + BLOCK, :], preferred_element_type=F32)
        mrow = mrow_ref[slot]
        m_ref[g, out_rows, :] = jnp.where(lo, mrow[:BLOCK], mrow[BLOCK:])
        l_ref[g, out_rows, :] = jnp.where(lo, acc[:BLOCK, LANES:], acc[BLOCK:, LANES:])
        acc_ref[g, out_rows, :] = jnp.where(lo, acc[:BLOCK, :LANES], acc[BLOCK:, :LANES])

    n_rg = 2 * BLOCK // BF16_ROWS
    blocks = []
    for g, (_, d) in enumerate(DIL_PATTERNS):
        sub_len = seq // d
        q_ref, k_ref, v_ref = qkv[g]
        for r in range(d):
            src = pl.ds(r, sub_len, stride=d) if d > 1 else pl.ds(0, seq)
            dst = slice(BLOCK + r * sub_len, BLOCK + (r + 1) * sub_len)
            q_st[g, dst, :] = (q_ref[0, src, :] * SCALE).astype(BF16)
            k_st[g, dst, :] = k_ref[0, src, :].astype(BF16)
            vx_st[g, dst, :LANES] = v_ref[0, src, :].astype(BF16)
            for n in range(sub_len // BLOCK):
                base = BLOCK + r * sub_len + n * BLOCK
                out_rows = pl.ds(n * BLOCK * d + r, BLOCK, stride=d) if d > 1 else pl.ds(n * BLOCK, BLOCK)
                blocks.append((g, base, n == 0, out_rows))

    nslot = s_ref.shape[0]
    for t in range(len(blocks) + 2):
        if t < len(blocks):
            g, base, first, _ = blocks[t]
            scores(g, t % nslot, base, first)
        for rg in range(n_rg):
            if 1 <= t <= len(blocks):
                softmax_group((t - 1) % nslot, rg)
            if rg == n_rg // 2 and 2 <= t:
                g, base, _, out_rows = blocks[t - 2]
                finish(g, (t - 2) % nslot, base, out_rows)

    def merge(i, carry):
        rows = pl.ds(pl.multiple_of(i * 2 * BLOCK, 2 * BLOCK), 2 * BLOCK)
        ms = [m_ref[g, rows, :] for g in range(N_GROUPS_A)]
        mx = jnp.maximum(jnp.maximum(ms[0], ms[1]), ms[2])
        num = jnp.zeros((2 * BLOCK, LANES), F32)
        den = jnp.zeros((2 * BLOCK, LANES), F32)
        for g in range(N_GROUPS_A):
            w = jnp.exp(ms[g] - mx)
            num = num + w * acc_ref[g, rows, :]
            den = den + w * l_ref[g, rows, :]
        o_ref[0, rows, :] = (num / den).astype(o_ref.dtype)
        return carry

    lax.fori_loop(0, seq // (2 * BLOCK), merge, 0)


def _mixer_a(proj_a, rev_a, weights):
    B, S, _ = proj_a.shape
    npair = OUT_WIDTH_A // LANES
    grid = (B, npair)

    def col(which, g):
        base = (which * WIDTH_A + g * OUT_WIDTH_A) // LANES
        return pl.BlockSpec((1, S, LANES), lambda b, hp: (b, 0, base + hp))

    cast_in, cast_out, cast_shapes = _cast_specs(weights, grid)
    in_specs = [col(which, g) for g in range(N_GROUPS_A) for which in range(3)]
    in_specs.append(pl.BlockSpec((1, 2 * N_GROUPS_A, 4 * BLOCK), lambda b, hp: (hp, 0, 0)))
    return pl.pallas_call(
        functools.partial(_mixer_a_kernel, seq=S, n_cast=len(weights)),
        grid=grid,
        in_specs=in_specs + cast_in,
        out_specs=[pl.BlockSpec((1, S, LANES), lambda b, hp: (b, 0, hp))] + cast_out,
        out_shape=[jax.ShapeDtypeStruct((B, S, OUT_WIDTH_A), BF16)] + cast_shapes,
        scratch_shapes=[pltpu.VMEM((2 * N_GROUPS_A, 2 * BLOCK, 2 * BLOCK), F32),
                        pltpu.VMEM((N_GROUPS_A, BLOCK + S, LANES), BF16),
                        pltpu.VMEM((N_GROUPS_A, BLOCK + S, LANES), BF16),
                        pltpu.VMEM((N_GROUPS_A, BLOCK + S, 2 * LANES), BF16),
                        pltpu.VMEM((4, 2 * BLOCK, 2 * BLOCK), F32),
                        pltpu.VMEM((4, 2 * BLOCK, 2 * BLOCK), BF16),
                        pltpu.VMEM((4, 2 * BLOCK, LANES), F32),
                        pltpu.VMEM((N_GROUPS_A, S, LANES), F32),
                        pltpu.VMEM((N_GROUPS_A, S, LANES), F32),
                        pltpu.VMEM((N_GROUPS_A, S, LANES), F32)],
        compiler_params=_params(("arbitrary", "arbitrary")),
        name="mixer_a",
    )(*([proj_a] * 9), rev_a, *weights)


def _mixer_b_kernel(*refs, tq, seq, lam_init, n_cast):
    nq = seq // tq
    q_ref, k_ref, v_ref, rev_ref, lam_ref, g_ref = refs[:6]
    o_ref = refs[6 + n_cast]
    toep_ref, vx_ref = refs[7 + 2 * n_cast:9 + 2 * n_cast]
    s_refs = refs[9 + 2 * n_cast:9 + 2 * n_cast + nq]
    p_refs = refs[9 + 2 * n_cast + nq:]
    _cast_blocks(refs[6:6 + n_cast], refs[7 + n_cast:7 + 2 * n_cast])
    rg_rows = BF16_ROWS
    n_rg = 2 * tq // rg_rows

    x = jnp.broadcast_to(rev_ref[0] * LOG2E, (tq, seq + tq))
    rolled = pltpu.roll(x, 0, 1, stride=1, stride_axis=0)
    toep_ref[:, :seq] = rolled[:, :seq]
    row = lax.broadcasted_iota(jnp.int32, (tq, tq), 0)
    col = lax.broadcasted_iota(jnp.int32, (tq, tq), 1)
    toep_ref[:, seq:] = jnp.where(col <= row, rolled[:, seq:], NEG_INF)

    vx_ref[:, :LANES] = v_ref[0]
    vx_ref[:, LANES:] = jnp.ones((seq, LANES), BF16)

    lane = lax.broadcasted_iota(jnp.int32, (tq, LANES), 1)
    lo = lane < HEAD_DIM
    lv = lam_ref[...]
    lam = (jnp.exp(jnp.sum(lv[0:1] * lv[1:2], axis=-1, keepdims=True))
           - jnp.exp(jnp.sum(lv[2:3] * lv[3:4], axis=-1, keepdims=True)) + lam_init)

    def score_chunks(qi):
        q = q_ref[0, qi * tq:(qi + 1) * tq, :]
        zero = jnp.zeros_like(q)
        qz = jnp.concatenate([jnp.where(lo, q, zero), jnp.where(lo, zero, q)], axis=0)

        def chunk(c):
            kc = k_ref[0, c * tq:(c + 1) * tq, :]
            s_refs[qi][:, c * tq:(c + 1) * tq] = lax.dot_general(
                qz, kc, (((1,), (1,)), ((), ())), preferred_element_type=F32)

        return [functools.partial(chunk, c) for c in range(qi + 1)]

    def softmax_group(qi, rg):
        rows = slice(rg * rg_rows, (rg + 1) * rg_rows)
        brow = (rg * rg_rows) % tq
        bias = toep_ref[brow:brow + rg_rows, (nq - qi) * tq:(nq + 1) * tq]
        t = s_refs[qi][rows, :] + bias
        m = jnp.max(t, axis=-1, keepdims=True)
        p_refs[qi][rows, :] = jnp.exp2(t - m).astype(BF16)

    def finish(qi):
        width = (qi + 1) * tq
        acc = jnp.dot(p_refs[qi][...], vx_ref[:width, :], preferred_element_type=F32)
        o = acc[:, :LANES] / acc[:, LANES:]
        y = o[:tq] - lam * o[tq:]
        y = _rms(y, g_ref[...]) * (1.0 - lam_init)
        o_ref[0, qi * tq:(qi + 1) * tq, :] = y.astype(o_ref.dtype)

    order = list(range(nq - 1, -1, -1))
    for t in range(nq + 2):
        stages = [score_chunks(order[t]) if t < nq else [],
                  [functools.partial(softmax_group, order[t - 1], rg) for rg in range(n_rg)] if 1 <= t <= nq else [],
                  [functools.partial(finish, order[t - 2])] if t >= 2 else []]
        for emit in _interleave(stages):
            emit()


def _mixer_b(proj, rev_b, lam_vecs, subln_g, lam_init, weights, tq=256):
    B, S, _ = proj.shape
    H = N_HEADS_B
    nq = S // tq
    grid = (B, H)
    kern = functools.partial(_mixer_b_kernel, tq=tq, seq=S, lam_init=lam_init, n_cast=len(weights))

    def col(off):
        return pl.BlockSpec((1, S, LANES), lambda b, h: (b, 0, off // LANES + h))

    cast_in, cast_out, cast_shapes = _cast_specs(weights, grid)
    return pl.pallas_call(
        kern,
        grid=grid,
        in_specs=[col(OFF_QB), col(OFF_KB), col(OFF_VB),
                  pl.BlockSpec((1, 1, S + tq), lambda b, h: (h, 0, 0)),
                  pl.BlockSpec((4, HEAD_DIM), lambda b, h: (0, 0)),
                  pl.BlockSpec((1, 2 * HEAD_DIM), lambda b, h: (0, 0))] + cast_in,
        out_specs=[pl.BlockSpec((1, S, LANES), lambda b, h: (b, 0, h))] + cast_out,
        out_shape=[jax.ShapeDtypeStruct((B, S, WIDTH_B), BF16)] + cast_shapes,
        scratch_shapes=([pltpu.VMEM((tq, S + tq), F32), pltpu.VMEM((S, 2 * LANES), BF16)]
                        + [pltpu.VMEM((2 * tq, (i + 1) * tq), F32) for i in range(nq)]
                        + [pltpu.VMEM((2 * tq, (i + 1) * tq), BF16) for i in range(nq)]),
        compiler_params=_params(("arbitrary", "arbitrary")),
        name="mixer_b",
    )(proj, proj, proj, rev_b, lam_vecs, subln_g, *weights)


def _attn_out_kernel(ya_ref, yb_ref, ga_ref, gb_ref, x_ref, wa_ref, wb_ref, wo_ref, g_ref, x1_ref, h_ref):
    pa = jnp.dot(ya_ref[...], wa_ref[...], preferred_element_type=F32)
    pb = jnp.dot(yb_ref[...], wb_ref[...], preferred_element_type=F32)
    ga = jax.nn.sigmoid(ga_ref[...].astype(F32))
    gb = jax.nn.sigmoid(gb_ref[...].astype(F32))
    merged = (ga * pa + gb * pb).astype(BF16)
    x1_ref[...] = x_ref[...] + jnp.dot(merged, wo_ref[...], preferred_element_type=F32)
    h_ref[...] = _rms(x1_ref[...], g_ref[...]).astype(BF16)


def _attn_out(ya, yb, proj_r, x2, wa, wb, wo, g, tm=512):
    T, D = x2.shape

    def resident(shape):
        return pl.BlockSpec(shape, lambda i: (0, 0), pipeline_mode=pl.Buffered(1))

    return pl.pallas_call(
        _attn_out_kernel,
        grid=(T // tm,),
        in_specs=[pl.BlockSpec((tm, OUT_WIDTH_A), lambda i: (i, 0)),
                  pl.BlockSpec((tm, WIDTH_B), lambda i: (i, 0)),
                  pl.BlockSpec((tm, D), lambda i: (i, OFF_GA // D)),
                  pl.BlockSpec((tm, D), lambda i: (i, OFF_GB // D)),
                  pl.BlockSpec((tm, D), lambda i: (i, 0)),
                  resident((OUT_WIDTH_A, D)), resident((WIDTH_B, D)), resident((D, D)), resident((1, D))],
        out_specs=[pl.BlockSpec((tm, D), lambda i: (i, 0)), pl.BlockSpec((tm, D), lambda i: (i, 0))],
        out_shape=[jax.ShapeDtypeStruct((T, D), F32), jax.ShapeDtypeStruct((T, D), BF16)],
        compiler_params=_params(("parallel",), vmem=VMEM_LIMIT_ATTN_OUT),
        name="attn_out",
    )(ya, yb, proj_r, proj_r, x2, wa, wb, wo, g)


def _ffn_kernel(x1_hbm, h_ref, wg_ref, wu_ref, wd_ref, gf_ref, o_ref, x1_buf, x1_sem, *, tm):
    i = pl.program_id(0)
    f = pl.program_id(1)
    x1_copy = pltpu.make_async_copy(x1_hbm.at[pl.ds(i * tm, tm), :], x1_buf, x1_sem)

    @pl.when(f == 0)
    def _():
        x1_copy.start()
        o_ref[...] = jnp.zeros(o_ref.shape, F32)

    h = h_ref[...]
    a = jnp.dot(h, wg_ref[...], preferred_element_type=F32)
    b = jnp.dot(h, wu_ref[...], preferred_element_type=F32)
    u = (a * jax.nn.sigmoid(a)) * b
    o_ref[...] += jnp.dot(u.astype(BF16), wd_ref[...], preferred_element_type=F32)

    @pl.when(f == pl.num_programs(1) - 1)
    def _():
        x1_copy.wait()
        o_ref[...] = _rms(x1_buf[...] + o_ref[...], gf_ref[...])


def _ffn(x1, h, wg, wu, wd, gf, tm=1024, tf=512):
    T, D = x1.shape
    F = wg.shape[1]
    return pl.pallas_call(
        functools.partial(_ffn_kernel, tm=tm),
        grid=(T // tm, F // tf),
        in_specs=[pl.BlockSpec(memory_space=pl.ANY),
                  pl.BlockSpec((tm, D), lambda i, f: (i, 0)),
                  pl.BlockSpec((D, tf), lambda i, f: (0, f)),
                  pl.BlockSpec((D, tf), lambda i, f: (0, f)),
                  pl.BlockSpec((tf, D), lambda i, f: (f, 0)),
                  pl.BlockSpec((1, D), lambda i, f: (0, 0))],
        out_specs=pl.BlockSpec((tm, D), lambda i, f: (i, 0)),
        out_shape=jax.ShapeDtypeStruct((T, D), F32),
        scratch_shapes=[pltpu.VMEM((tm, D), F32), pltpu.SemaphoreType.DMA(())],
        compiler_params=_params(("parallel", "arbitrary"), vmem=VMEM_LIMIT_FFN),
        name="ffn",
    )(x1, h, wg, wu, wd, gf)


def _rev_a_index():
    u = np.arange(4 * BLOCK)
    rel = np.clip(2 * BLOCK - u, 0, None)
    return np.stack([_rel_bucket_np(rel * d) for _, d in DIL_PATTERNS])


def _rev_b_index(seq, tq):
    c = np.arange(seq + tq)
    return _rel_bucket_np(np.clip(seq - c, 0, seq - 1))


def _lookup(table, idx):
    idx = jnp.asarray(idx)[None, :]
    out = jnp.zeros((table.shape[1], idx.shape[1]), F32)
    for b in range(NUM_BUCKETS):
        out = jnp.where(idx == b, table[b][:, None], out)
    return out


def kernel(x, norm_attn_g, w_in, w_proj_a, w_proj_b, w_out, rel_bias_table, diff_lambda_q1, diff_lambda_k1, diff_lambda_q2, diff_lambda_k2, diff_subln_g, norm_ffn_g, w_ffn_gate, w_ffn_up, w_ffn_down, norm_final_g):
    B, S, D = x.shape
    T = B * S
    depth = w_in.shape[0]
    assert depth == 1, "the final RMSNorm is fused into the FFN epilogue of a single layer"
    table_a = rel_bias_table[:, :N_HEADS_A].astype(F32)
    table_b = rel_bias_table[:, N_HEADS_A:].astype(F32)
    tq = 256

    idx_a = _rev_a_index()
    rev_a = jnp.stack([_lookup(table_a[:, g * HEADS_PER_GROUP_A:(g + 1) * HEADS_PER_GROUP_A], idx_a[g])
                       for g in range(N_GROUPS_A)])
    npair = OUT_WIDTH_A // LANES
    rev_a = jnp.transpose(rev_a.reshape(N_GROUPS_A, npair, 2, 4 * BLOCK), (1, 0, 2, 3))
    rev_a = rev_a.reshape(npair, 2 * N_GROUPS_A, 4 * BLOCK)
    rev_b = _lookup(table_b, _rev_b_index(S, tq))[:, None, :]

    x2 = x.reshape(T, D)
    l = 0
    lam_init = 0.8 - 0.6 * math.exp(-0.3 * l)
    proj_a, proj_r = _in_proj(x2, norm_attn_g[l][None, :], w_in[l])
    ya, wa, wb, wo = _mixer_a(proj_a.reshape(B, S, PROJ_A), rev_a, [w_proj_a[l], w_proj_b[l], w_out[l]])

    lam_vecs = jnp.stack([diff_lambda_q1[l], diff_lambda_k1[l],
                          diff_lambda_q2[l], diff_lambda_k2[l]]).astype(F32)
    yb, wg, wu, wd = _mixer_b(proj_r.reshape(B, S, PROJ_R), rev_b, lam_vecs, diff_subln_g[l][None, :].astype(F32),
                              lam_init, [w_ffn_gate[l], w_ffn_up[l], w_ffn_down[l]], tq=tq)

    x1, h = _attn_out(ya.reshape(T, OUT_WIDTH_A), yb.reshape(T, WIDTH_B), proj_r, x2, wa, wb, wo,
                      norm_ffn_g[l][None, :])
    out = _ffn(x1, h, wg, wu, wd, norm_final_g[None, :])
    return out.reshape(B, S, D)
```

```python
import functools
import math

import numpy as np
import jax
import jax.numpy as jnp
from jax import lax
from jax.experimental import pallas as pl
from jax.experimental.pallas import tpu as pltpu

D_MODEL = 2048
HEAD_DIM = 64
DIL_PATTERNS = ((128, 1), (512, 4), (2048, 16))
N_GROUPS_A = len(DIL_PATTERNS)
HEADS_PER_GROUP_A = 8
N_HEADS_A = N_GROUPS_A * HEADS_PER_GROUP_A
WIDTH_A = N_HEADS_A * HEAD_DIM
OUT_WIDTH_A = HEADS_PER_GROUP_A * HEAD_DIM
BLOCK = 128
N_HEADS_B = D_MODEL // (2 * HEAD_DIM)
WIDTH_B = N_HEADS_B * 2 * HEAD_DIM
NUM_BUCKETS = 32
MAX_DISTANCE = 2048
D_FF = -(-8 * D_MODEL // (3 * 256)) * 256
PROJ_A = 3 * WIDTH_A
PROJ_R = 3 * WIDTH_B + 2 * D_MODEL
D_IN = PROJ_A + PROJ_R
NORM_EPS = 1e-6
NEG_INF = -1e30
SCALE = HEAD_DIM ** -0.5
LOG2E = math.log2(math.e)

OFF_QB = 0
OFF_KB = WIDTH_B
OFF_VB = 2 * WIDTH_B
OFF_GA = 3 * WIDTH_B
OFF_GB = OFF_GA + D_MODEL

LANES = 128
VMEM_LIMIT = 56 * 1024 * 1024
VMEM_LIMIT_ATTN_OUT = 58 * 1024 * 1024
VMEM_LIMIT_FFN = 60 * 1024 * 1024

BF16 = jnp.bfloat16
F32 = jnp.float32


def _rel_bucket_np(dist):
    n = np.maximum(dist, 0)
    max_exact = NUM_BUCKETS // 2
    nf = np.maximum(n, 1).astype(np.float32)
    large = max_exact + (np.log(nf / np.float32(max_exact)) / np.float32(math.log(MAX_DISTANCE / max_exact))
                         * np.float32(NUM_BUCKETS - max_exact)).astype(np.int32)
    large = np.minimum(large, NUM_BUCKETS - 1)
    return np.where(n < max_exact, n, large).astype(np.int32)


def _rms(x, g):
    ms = jnp.mean(x * x, axis=-1, keepdims=True)
    return x * lax.rsqrt(ms + NORM_EPS) * g


def _params(sem, vmem=VMEM_LIMIT):
    return pltpu.CompilerParams(dimension_semantics=sem, vmem_limit_bytes=vmem)


BF16_ROWS = 16


def _cast_specs(weights, grid):
    nsteps = math.prod(grid)
    in_specs, out_specs, out_shapes = [], [], []
    for w in weights:
        rows, cols = w.shape
        blk = next(r for r in range(BF16_ROWS, rows + 1, BF16_ROWS)
                   if rows % r == 0 and nsteps % (rows // r) == 0 and rows // r <= nsteps)
        per = nsteps // (rows // blk)

        def index(*ids, per=per):
            step = 0
            for i, n in zip(ids, grid):
                step = step * n + i
            return (step // per, 0)

        in_specs.append(pl.BlockSpec((blk, cols), index))
        out_specs.append(pl.BlockSpec((blk, cols), index))
        out_shapes.append(jax.ShapeDtypeStruct((rows, cols), BF16))
    return in_specs, out_specs, out_shapes


def _interleave(lists):
    keyed = [((i + 0.5) / len(items), n, i, item) for n, items in enumerate(lists) for i, item in enumerate(items)]
    return [item for _, _, _, item in sorted(keyed, key=lambda k: k[:3])]


def _cast_blocks(in_refs, out_refs):
    for i_ref, o_ref in zip(in_refs, out_refs):
        o_ref[...] = i_ref[...].astype(BF16)


def _in_proj_kernel(x_hbm, g_ref, w_ref, oa_ref, or_ref, h_ref, x_buf, x_sem, *, na, nqb, tm):
    i = pl.program_id(0)
    j = pl.program_id(1)

    def x_copy(tile):
        return pltpu.make_async_copy(x_hbm.at[pl.ds(tile * tm, tm), :], x_buf, x_sem)

    @pl.when(j == 0)
    def _():
        @pl.when(i == 0)
        def _():
            x_copy(0).start()

        x_copy(i).wait()
        h_ref[...] = _rms(x_buf[...], g_ref[...]).astype(BF16)

        @pl.when(i + 1 < pl.num_programs(0))
        def _():
            x_copy(i + 1).start()

    @pl.when(j < na)
    def _():
        oa_ref[...] = jnp.dot(h_ref[...], w_ref[...].astype(BF16), preferred_element_type=F32)

    @pl.when(j >= na)
    def _():
        scale = jnp.where(j < na + nqb, jnp.float32(SCALE * LOG2E), jnp.float32(1.0))
        res = jnp.dot(h_ref[...], w_ref[...].astype(BF16), preferred_element_type=F32)
        or_ref[...] = (res * scale).astype(BF16)


def _in_proj(x2, g, w, tm=2048, tn=512):
    T, D = x2.shape
    na = PROJ_A // tn
    assert OFF_QB == 0 and WIDTH_B % tn == 0
    return pl.pallas_call(
        functools.partial(_in_proj_kernel, na=na, nqb=WIDTH_B // tn, tm=tm),
        grid=(T // tm, D_IN // tn),
        in_specs=[pl.BlockSpec(memory_space=pl.ANY),
                  pl.BlockSpec((1, D), lambda i, j: (0, 0)),
                  pl.BlockSpec((D, tn), lambda i, j: (0, j))],
        out_specs=[pl.BlockSpec((tm, tn), lambda i, j: (i, jnp.minimum(j, na - 1))),
                   pl.BlockSpec((tm, tn), lambda i, j: (i, jnp.maximum(j - na, 0)))],
        out_shape=[jax.ShapeDtypeStruct((T, PROJ_A), F32), jax.ShapeDtypeStruct((T, PROJ_R), BF16)],
        scratch_shapes=[pltpu.VMEM((tm, D), BF16), pltpu.VMEM((tm, D), F32), pltpu.SemaphoreType.DMA(())],
        compiler_params=_params(("arbitrary", "arbitrary")),
        name="in_proj",
    )(x2, g, w)


def _mixer_a_kernel(*refs, seq, n_cast):
    qkv = (refs[0:3], refs[3:6], refs[6:9])
    rev_ref = refs[9]
    o_ref = refs[10 + n_cast]
    (bias_ref, q_st, k_st, vx_st, s_ref, p_ref, mrow_ref, m_ref, l_ref, acc_ref) = refs[11 + 2 * n_cast:]
    _cast_blocks(refs[10:10 + n_cast], refs[11 + n_cast:11 + 2 * n_cast])
    lane = lax.broadcasted_iota(jnp.int32, (BLOCK, LANES), 1)
    lo = lane < HEAD_DIM

    row = lax.broadcasted_iota(jnp.int32, (2 * BLOCK, 2 * BLOCK), 0)
    col = lax.broadcasted_iota(jnp.int32, (2 * BLOCK, 2 * BLOCK), 1)
    rel = BLOCK + (row & (BLOCK - 1)) - col
    band = (rel >= 0) & (rel <= BLOCK)
    band_first = band & (col >= BLOCK)
    for g in range(N_GROUPS_A):
        halves = []
        for hh in range(2):
            x = jnp.broadcast_to(rev_ref[0, g * 2 + hh:g * 2 + hh + 1, :], (BLOCK, 4 * BLOCK))
            halves.append(pltpu.roll(x, 0, 1, stride=1, stride_axis=0)[:, BLOCK:3 * BLOCK])
        toep = jnp.concatenate(halves, axis=0)
        bias_ref[2 * g] = jnp.where(band, toep, NEG_INF)
        bias_ref[2 * g + 1] = jnp.where(band_first, toep, NEG_INF)

    for g in range(N_GROUPS_A):
        k_st[g, 0:BLOCK, :] = jnp.zeros((BLOCK, LANES), BF16)
        vx_st[g, 0:BLOCK, :] = jnp.zeros((BLOCK, 2 * LANES), BF16)
        vx_st[g, :, LANES:] = jnp.ones((BLOCK + seq, LANES), BF16)

    def scores(g, slot, base, first):
        q = q_st[g, base:base + BLOCK, :]
        zero = jnp.zeros_like(q)
        qz = jnp.concatenate([jnp.where(lo, q, zero), jnp.where(lo, zero, q)], axis=0)
        kw = k_st[g, base - BLOCK:base + BLOCK, :]
        s = lax.dot_general(qz, kw, (((1,), (1,)), ((), ())), preferred_element_type=F32)
        s_ref[slot] = s + bias_ref[2 * g + (1 if first else 0)]

    def softmax_group(slot, rg):
        rows = slice(rg * BF16_ROWS, (rg + 1) * BF16_ROWS)
        m = jnp.max(s_ref[slot, rows, :], axis=-1, keepdims=True)
        p_ref[slot, rows, :] = jnp.exp(s_ref[slot, rows, :] - m).astype(BF16)
        mrow_ref[slot, rows, :] = jnp.broadcast_to(m, (BF16_ROWS, LANES))

    def finish(g, slot, base, out_rows):
        acc = jnp.dot(p_ref[slot], vx_st[g, base - BLOCK:base + BLOCK, :], preferred_element_type=F32)
        mrow = mrow_ref[slot]
        m_ref[g, out_rows, :] = jnp.where(lo, mrow[:BLOCK], mrow[BLOCK:])
        l_ref[g, out_rows, :] = jnp.where(lo, acc[:BLOCK, LANES:], acc[BLOCK:, LANES:])
        acc_ref[g, out_rows, :] = jnp.where(lo, acc[:BLOCK, :LANES], acc[BLOCK:, :LANES])

    def merge(rows):
        ms = [m_ref[g, rows, :] for g in range(N_GROUPS_A)]
        mx = jnp.maximum(jnp.maximum(ms[0], ms[1]), ms[2])
        num = jnp.zeros((BLOCK, LANES), F32)
        den = jnp.zeros((BLOCK, LANES), F32)
        for g in range(N_GROUPS_A):
            w = jnp.exp(ms[g] - mx)
            num = num + w * acc_ref[g, rows, :]
            den = den + w * l_ref[g, rows, :]
        o_ref[0, rows, :] = (num / den).astype(o_ref.dtype)

    n_rg = 2 * BLOCK // BF16_ROWS
    blocks = []
    for g in sorted(range(N_GROUPS_A), key=lambda g: -DIL_PATTERNS[g][1]):
        d = DIL_PATTERNS[g][1]
        sub_len = seq // d
        q_ref, k_ref, v_ref = qkv[g]
        for r in range(d):
            src = pl.ds(r, sub_len, stride=d) if d > 1 else pl.ds(0, seq)
            dst = slice(BLOCK + r * sub_len, BLOCK + (r + 1) * sub_len)
            q_st[g, dst, :] = (q_ref[0, src, :] * SCALE).astype(BF16)
            k_st[g, dst, :] = k_ref[0, src, :].astype(BF16)
            vx_st[g, dst, :LANES] = v_ref[0, src, :].astype(BF16)
            for n in range(sub_len // BLOCK):
                base = BLOCK + r * sub_len + n * BLOCK
                out_rows = pl.ds(n * BLOCK * d + r, BLOCK, stride=d) if d > 1 else pl.ds(n * BLOCK, BLOCK)
                blocks.append((g, base, n == 0, out_rows))

    nslot = s_ref.shape[0]
    for t in range(len(blocks) + 2):
        if t < len(blocks):
            g, base, first, _ = blocks[t]
            scores(g, t % nslot, base, first)
        for rg in range(n_rg):
            if 1 <= t <= len(blocks):
                softmax_group((t - 1) % nslot, rg)
            if rg == n_rg // 2 and 2 <= t:
                g, base, _, out_rows = blocks[t - 2]
                finish(g, (t - 2) % nslot, base, out_rows)
                if DIL_PATTERNS[g][1] == 1:
                    merge(out_rows)


def _mixer_a(proj_a, rev_a, weights):
    B, S, _ = proj_a.shape
    npair = OUT_WIDTH_A // LANES
    grid = (B, npair)

    def col(which, g):
        base = (which * WIDTH_A + g * OUT_WIDTH_A) // LANES
        return pl.BlockSpec((1, S, LANES), lambda b, hp: (b, 0, base + hp))

    cast_in, cast_out, cast_shapes = _cast_specs(weights, grid)
    in_specs = [col(which, g) for g in range(N_GROUPS_A) for which in range(3)]
    in_specs.append(pl.BlockSpec((1, 2 * N_GROUPS_A, 4 * BLOCK), lambda b, hp: (hp, 0, 0)))
    return pl.pallas_call(
        functools.partial(_mixer_a_kernel, seq=S, n_cast=len(weights)),
        grid=grid,
        in_specs=in_specs + cast_in,
        out_specs=[pl.BlockSpec((1, S, LANES), lambda b, hp: (b, 0, hp))] + cast_out,
        out_shape=[jax.ShapeDtypeStruct((B, S, OUT_WIDTH_A), BF16)] + cast_shapes,
        scratch_shapes=[pltpu.VMEM((2 * N_GROUPS_A, 2 * BLOCK, 2 * BLOCK), F32),
                        pltpu.VMEM((N_GROUPS_A, BLOCK + S, LANES), BF16),
                        pltpu.VMEM((N_GROUPS_A, BLOCK + S, LANES), BF16),
                        pltpu.VMEM((N_GROUPS_A, BLOCK + S, 2 * LANES), BF16),
                        pltpu.VMEM((4, 2 * BLOCK, 2 * BLOCK), F32),
                        pltpu.VMEM((4, 2 * BLOCK, 2 * BLOCK), BF16),
                        pltpu.VMEM((4, 2 * BLOCK, LANES), F32),
                        pltpu.VMEM((N_GROUPS_A, S, LANES), F32),
                        pltpu.VMEM((N_GROUPS_A, S, LANES), F32),
                        pltpu.VMEM((N_GROUPS_A, S, LANES), F32)],
        compiler_params=_params(("arbitrary", "arbitrary")),
        name="mixer_a",
    )(*([proj_a] * 9), rev_a, *weights)


def _mixer_b_kernel(*refs, tq, seq, lam_init, n_cast):
    nq = seq // tq
    q_ref, k_ref, v_ref, rev_ref, lam_ref, g_ref = refs[:6]
    o_ref = refs[6 + n_cast]
    toep_ref, vx_ref = refs[7 + 2 * n_cast:9 + 2 * n_cast]
    s_refs = refs[9 + 2 * n_cast:9 + 2 * n_cast + nq]
    p_refs = refs[9 + 2 * n_cast + nq:]
    _cast_blocks(refs[6:6 + n_cast], refs[7 + n_cast:7 + 2 * n_cast])
    rg_rows = BF16_ROWS
    n_rg = 2 * tq // rg_rows

    x = jnp.broadcast_to(rev_ref[0] * LOG2E, (tq, seq + tq))
    rolled = pltpu.roll(x, 0, 1, stride=1, stride_axis=0)
    toep_ref[:, :seq] = rolled[:, :seq]
    row = lax.broadcasted_iota(jnp.int32, (tq, tq), 0)
    col = lax.broadcasted_iota(jnp.int32, (tq, tq), 1)
    toep_ref[:, seq:] = jnp.where(col <= row, rolled[:, seq:], NEG_INF)

    vx_ref[:, :LANES] = v_ref[0]
    vx_ref[:, LANES:] = jnp.ones((seq, LANES), BF16)

    lane = lax.broadcasted_iota(jnp.int32, (tq, LANES), 1)
    lo = lane < HEAD_DIM
    lv = lam_ref[...]
    lam = (jnp.exp(jnp.sum(lv[0:1] * lv[1:2], axis=-1, keepdims=True))
           - jnp.exp(jnp.sum(lv[2:3] * lv[3:4], axis=-1, keepdims=True)) + lam_init)

    def score_chunks(qi):
        q = q_ref[0, qi * tq:(qi + 1) * tq, :]
        zero = jnp.zeros_like(q)
        qz = jnp.concatenate([jnp.where(lo, q, zero), jnp.where(lo, zero, q)], axis=0)

        def chunk(c):
            kc = k_ref[0, c * tq:(c + 1) * tq, :]
            s_refs[qi][:, c * tq:(c + 1) * tq] = lax.dot_general(
                qz, kc, (((1,), (1,)), ((), ())), preferred_element_type=F32)

        return [functools.partial(chunk, c) for c in range(qi + 1)]

    def softmax_group(qi, rg):
        rows = slice(rg * rg_rows, (rg + 1) * rg_rows)
        brow = (rg * rg_rows) % tq
        bias = toep_ref[brow:brow + rg_rows, (nq - qi) * tq:(nq + 1) * tq]
        t = s_refs[qi][rows, :] + bias
        m = jnp.max(t, axis=-1, keepdims=True)
        p_refs[qi][rows, :] = jnp.exp2(t - m).astype(BF16)

    halves = {}

    def value_matmul(qi, half):
        acc = jnp.dot(p_refs[qi][half * tq:(half + 1) * tq, :], vx_ref[:(qi + 1) * tq, :],
                      preferred_element_type=F32)
        halves[qi, half] = acc[:, :LANES] / acc[:, LANES:]

    def finish(qi):
        y = halves[qi, 0] - lam * halves[qi, 1]
        y = _rms(y, g_ref[...]) * (1.0 - lam_init)
        o_ref[0, qi * tq:(qi + 1) * tq, :] = y.astype(o_ref.dtype)

    order = list(range(nq - 1, -1, -1))
    for t in range(nq + 2):
        scores_t = score_chunks(order[t]) if t < nq else []
        softmax_t, tail_t = [], []
        if 1 <= t <= nq:
            qi = order[t - 1]
            groups = [functools.partial(softmax_group, qi, rg) for rg in range(n_rg)]
            softmax_t = groups[:n_rg // 2] + [functools.partial(value_matmul, qi, 0)] + groups[n_rg // 2:]
        if t >= 2:
            qi = order[t - 2]
            tail_t = [functools.partial(value_matmul, qi, 1), functools.partial(finish, qi)]
        for emit in _interleave([scores_t, softmax_t, tail_t]):
            emit()


def _mixer_b(proj, rev_b, lam_vecs, subln_g, lam_init, weights, tq=256):
    B, S, _ = proj.shape
    H = N_HEADS_B
    nq = S // tq
    grid = (B, H)
    kern = functools.partial(_mixer_b_kernel, tq=tq, seq=S, lam_init=lam_init, n_cast=len(weights))

    def col(off):
        return pl.BlockSpec((1, S, LANES), lambda b, h: (b, 0, off // LANES + h))

    cast_in, cast_out, cast_shapes = _cast_specs(weights, grid)
    return pl.pallas_call(
        kern,
        grid=grid,
        in_specs=[col(OFF_QB), col(OFF_KB), col(OFF_VB),
                  pl.BlockSpec((1, 1, S + tq), lambda b, h: (h, 0, 0)),
                  pl.BlockSpec((4, HEAD_DIM), lambda b, h: (0, 0)),
                  pl.BlockSpec((1, 2 * HEAD_DIM), lambda b, h: (0, 0))] + cast_in,
        out_specs=[pl.BlockSpec((1, S, LANES), lambda b, h: (b, 0, h))] + cast_out,
        out_shape=[jax.ShapeDtypeStruct((B, S, WIDTH_B), BF16)] + cast_shapes,
        scratch_shapes=([pltpu.VMEM((tq, S + tq), F32), pltpu.VMEM((S, 2 * LANES), BF16)]
                        + [pltpu.VMEM((2 * tq, (i + 1) * tq), F32) for i in range(nq)]
                        + [pltpu.VMEM((2 * tq, (i + 1) * tq), BF16) for i in range(nq)]),
        compiler_params=_params(("arbitrary", "arbitrary")),
        name="mixer_b",
    )(proj, proj, proj, rev_b, lam_vecs, subln_g, *weights)


def _attn_out_kernel(ya_ref, yb_ref, ga_ref, gb_ref, x_ref, wa_ref, wb_ref, wo_ref, g_ref, x1_ref, h_ref):
    pa = jnp.dot(ya_ref[...], wa_ref[...], preferred_element_type=F32)
    pb = jnp.dot(yb_ref[...], wb_ref[...], preferred_element_type=F32)
    ga = jax.nn.sigmoid(ga_ref[...].astype(F32))
    gb = jax.nn.sigmoid(gb_ref[...].astype(F32))
    merged = (ga * pa + gb * pb).astype(BF16)
    x1_ref[...] = x_ref[...] + jnp.dot(merged, wo_ref[...], preferred_element_type=F32)
    h_ref[...] = _rms(x1_ref[...], g_ref[...]).astype(BF16)


def _attn_out(ya, yb, proj_r, x2, wa, wb, wo, g, tm=512):
    T, D = x2.shape

    def resident(shape):
        return pl.BlockSpec(shape, lambda i: (0, 0), pipeline_mode=pl.Buffered(1))

    return pl.pallas_call(
        _attn_out_kernel,
        grid=(T // tm,),
        in_specs=[pl.BlockSpec((tm, OUT_WIDTH_A), lambda i: (i, 0)),
                  pl.BlockSpec((tm, WIDTH_B), lambda i: (i, 0)),
                  pl.BlockSpec((tm, D), lambda i: (i, OFF_GA // D)),
                  pl.BlockSpec((tm, D), lambda i: (i, OFF_GB // D)),
                  pl.BlockSpec((tm, D), lambda i: (i, 0)),
                  resident((OUT_WIDTH_A, D)), resident((WIDTH_B, D)), resident((D, D)), resident((1, D))],
        out_specs=[pl.BlockSpec((tm, D), lambda i: (i, 0)), pl.BlockSpec((tm, D), lambda i: (i, 0))],
        out_shape=[jax.ShapeDtypeStruct((T, D), F32), jax.ShapeDtypeStruct((T, D), BF16)],
        compiler_params=_params(("parallel",), vmem=VMEM_LIMIT_ATTN_OUT),
        name="attn_out",
    )(ya, yb, proj_r, proj_r, x2, wa, wb, wo, g)


def _ffn_kernel(x1_hbm, h_ref, wg_ref, wu_ref, wd_ref, gf_ref, o_ref, x1_buf, x1_sem, *, tm):
    i = pl.program_id(0)
    f = pl.program_id(1)
    x1_copy = pltpu.make_async_copy(x1_hbm.at[pl.ds(i * tm, tm), :], x1_buf, x1_sem)

    @pl.when(f == 0)
    def _():
        x1_copy.start()
        o_ref[...] = jnp.zeros(o_ref.shape, F32)

    h = h_ref[...]
    a = jnp.dot(h, wg_ref[...], preferred_element_type=F32)
    b = jnp.dot(h, wu_ref[...], preferred_element_type=F32)
    u = (a * jax.nn.sigmoid(a)) * b
    o_ref[...] += jnp.dot(u.astype(BF16), wd_ref[...], preferred_element_type=F32)

    @pl.when(f == pl.num_programs(1) - 1)
    def _():
        x1_copy.wait()
        o_ref[...] = _rms(x1_buf[...] + o_ref[...], gf_ref[...])


def _ffn(x1, h, wg, wu, wd, gf, tm=1024, tf=512):
    T, D = x1.shape
    F = wg.shape[1]
    return pl.pallas_call(
        functools.partial(_ffn_kernel, tm=tm),
        grid=(T // tm, F // tf),
        in_specs=[pl.BlockSpec(memory_space=pl.ANY),
                  pl.BlockSpec((tm, D), lambda i, f: (i, 0)),
                  pl.BlockSpec((D, tf), lambda i, f: (0, f)),
                  pl.BlockSpec((D, tf), lambda i, f: (0, f)),
                  pl.BlockSpec((tf, D), lambda i, f: (f, 0)),
                  pl.BlockSpec((1, D), lambda i, f: (0, 0))],
        out_specs=pl.BlockSpec((tm, D), lambda i, f: (i, 0)),
        out_shape=jax.ShapeDtypeStruct((T, D), F32),
        scratch_shapes=[pltpu.VMEM((tm, D), F32), pltpu.SemaphoreType.DMA(())],
        compiler_params=_params(("parallel", "arbitrary"), vmem=VMEM_LIMIT_FFN),
        name="ffn",
    )(x1, h, wg, wu, wd, gf)


def _rev_a_index():
    u = np.arange(4 * BLOCK)
    rel = np.clip(2 * BLOCK - u, 0, None)
    return np.stack([_rel_bucket_np(rel * d) for _, d in DIL_PATTERNS])


def _rev_b_index(seq, tq):
    c = np.arange(seq + tq)
    return _rel_bucket_np(np.clip(seq - c, 0, seq - 1))


def _lookup(table, idx):
    idx = jnp.asarray(idx)[None, :]
    out = jnp.zeros((table.shape[1], idx.shape[1]), F32)
    for b in range(NUM_BUCKETS):
        out = jnp.where(idx == b, table[b][:, None], out)
    return out


def kernel(x, norm_attn_g, w_in, w_proj_a, w_proj_b, w_out, rel_bias_table, diff_lambda_q1, diff_lambda_k1, diff_lambda_q2, diff_lambda_k2, diff_subln_g, norm_ffn_g, w_ffn_gate, w_ffn_up, w_ffn_down, norm_final_g):
    B, S, D = x.shape
    T = B * S
    depth = w_in.shape[0]
    assert depth == 1, "the final RMSNorm is fused into the FFN epilogue of a single layer"
    table_a = rel_bias_table[:, :N_HEADS_A].astype(F32)
    table_b = rel_bias_table[:, N_HEADS_A:].astype(F32)
    tq = 256

    idx_a = _rev_a_index()
    rev_a = jnp.stack([_lookup(table_a[:, g * HEADS_PER_GROUP_A:(g + 1) * HEADS_PER_GROUP_A], idx_a[g])
                       for g in range(N_GROUPS_A)])
    npair = OUT_WIDTH_A // LANES
    rev_a = jnp.transpose(rev_a.reshape(N_GROUPS_A, npair, 2, 4 * BLOCK), (1, 0, 2, 3))
    rev_a = rev_a.reshape(npair, 2 * N_GROUPS_A, 4 * BLOCK)
    rev_b = _lookup(table_b, _rev_b_index(S, tq))[:, None, :]

    x2 = x.reshape(T, D)
    l = 0
    lam_init = 0.8 - 0.6 * math.exp(-0.3 * l)
    proj_a, proj_r = _in_proj(x2, norm_attn_g[l][None, :], w_in[l])
    ya, wa, wb, wo = _mixer_a(proj_a.reshape(B, S, PROJ_A), rev_a, [w_proj_a[l], w_proj_b[l], w_out[l]])

    lam_vecs = jnp.stack([diff_lambda_q1[l], diff_lambda_k1[l],
                          diff_lambda_q2[l], diff_lambda_k2[l]]).astype(F32)
    yb, wg, wu, wd = _mixer_b(proj_r.reshape(B, S, PROJ_R), rev_b, lam_vecs, diff_subln_g[l][None, :].astype(F32),
                              lam_init, [w_ffn_gate[l], w_ffn_up[l], w_ffn_down[l]], tq=tq)

    x1, h = _attn_out(ya.reshape(T, OUT_WIDTH_A), yb.reshape(T, WIDTH_B), proj_r, x2, wa, wb, wo,
                      norm_ffn_g[l][None, :])
    out = _ffn(x1, h, wg, wu, wd, norm_final_g[None, :])
    return out.reshape(B, S, D)
```

```python
import functools
import math

import numpy as np
import jax
import jax.numpy as jnp
from jax import lax
from jax.experimental import pallas as pl
from jax.experimental.pallas import tpu as pltpu

D_MODEL = 2048
HEAD_DIM = 64
DIL_PATTERNS = ((128, 1), (512, 4), (2048, 16))
N_GROUPS_A = len(DIL_PATTERNS)
HEADS_PER_GROUP_A = 8
N_HEADS_A = N_GROUPS_A * HEADS_PER_GROUP_A
WIDTH_A = N_HEADS_A * HEAD_DIM
OUT_WIDTH_A = HEADS_PER_GROUP_A * HEAD_DIM
BLOCK = 128
N_HEADS_B = D_MODEL // (2 * HEAD_DIM)
WIDTH_B = N_HEADS_B * 2 * HEAD_DIM
NUM_BUCKETS = 32
MAX_DISTANCE = 2048
D_FF = -(-8 * D_MODEL // (3 * 256)) * 256
PROJ_A = 3 * WIDTH_A
PROJ_R = 3 * WIDTH_B + 2 * D_MODEL
D_IN = PROJ_A + PROJ_R
NORM_EPS = 1e-6
NEG_INF = -1e30
SCALE = HEAD_DIM ** -0.5
LOG2E = math.log2(math.e)

OFF_QB = 0
OFF_KB = WIDTH_B
OFF_VB = 2 * WIDTH_B
OFF_GA = 3 * WIDTH_B
OFF_GB = OFF_GA + D_MODEL

LANES = 128
VMEM_LIMIT = 56 * 1024 * 1024
VMEM_LIMIT_ATTN_OUT = 58 * 1024 * 1024
VMEM_LIMIT_FFN = 60 * 1024 * 1024

BF16 = jnp.bfloat16
F32 = jnp.float32


def _rel_bucket_np(dist):
    n = np.maximum(dist, 0)
    max_exact = NUM_BUCKETS // 2
    nf = np.maximum(n, 1).astype(np.float32)
    large = max_exact + (np.log(nf / np.float32(max_exact)) / np.float32(math.log(MAX_DISTANCE / max_exact))
                         * np.float32(NUM_BUCKETS - max_exact)).astype(np.int32)
    large = np.minimum(large, NUM_BUCKETS - 1)
    return np.where(n < max_exact, n, large).astype(np.int32)


def _rms(x, g):
    ms = jnp.mean(x * x, axis=-1, keepdims=True)
    return x * lax.rsqrt(ms + NORM_EPS) * g


def _params(sem, vmem=VMEM_LIMIT):
    return pltpu.CompilerParams(dimension_semantics=sem, vmem_limit_bytes=vmem)


BF16_ROWS = 16


def _cast_specs(weights, grid):
    nsteps = math.prod(grid)
    in_specs, out_specs, out_shapes = [], [], []
    for w in weights:
        rows, cols = w.shape
        blk = next(r for r in range(BF16_ROWS, rows + 1, BF16_ROWS)
                   if rows % r == 0 and nsteps % (rows // r) == 0 and rows // r <= nsteps)
        per = nsteps // (rows // blk)

        def index(*ids, per=per):
            step = 0
            for i, n in zip(ids, grid):
                step = step * n + i
            return (step // per, 0)

        in_specs.append(pl.BlockSpec((blk, cols), index))
        out_specs.append(pl.BlockSpec((blk, cols), index))
        out_shapes.append(jax.ShapeDtypeStruct((rows, cols), BF16))
    return in_specs, out_specs, out_shapes


def _interleave(lists):
    keyed = [((i + 0.5) / len(items), n, i, item) for n, items in enumerate(lists) for i, item in enumerate(items)]
    return [item for _, _, _, item in sorted(keyed, key=lambda k: k[:3])]


def _cast_blocks(in_refs, out_refs):
    for i_ref, o_ref in zip(in_refs, out_refs):
        o_ref[...] = i_ref[...].astype(BF16)


def _in_proj_kernel(x_hbm, g_ref, w_ref, oa_ref, or_ref, h_ref, x_buf, x_sem, *, na, nqb, tm):
    i = pl.program_id(0)
    j = pl.program_id(1)

    def x_copy(tile):
        return pltpu.make_async_copy(x_hbm.at[pl.ds(tile * tm, tm), :], x_buf, x_sem)

    @pl.when(j == 0)
    def _():
        @pl.when(i == 0)
        def _():
            x_copy(0).start()

        x_copy(i).wait()
        h_ref[...] = _rms(x_buf[...], g_ref[...]).astype(BF16)

        @pl.when(i + 1 < pl.num_programs(0))
        def _():
            x_copy(i + 1).start()

    @pl.when(j < na)
    def _():
        oa_ref[...] = jnp.dot(h_ref[...], w_ref[...].astype(BF16), preferred_element_type=F32)

    @pl.when(j >= na)
    def _():
        scale = jnp.where(j < na + nqb, jnp.float32(SCALE * LOG2E), jnp.float32(1.0))
        res = jnp.dot(h_ref[...], w_ref[...].astype(BF16), preferred_element_type=F32)
        or_ref[...] = (res * scale).astype(BF16)


def _in_proj(x2, g, w, tm=2048, tn=512):
    T, D = x2.shape
    na = PROJ_A // tn
    assert OFF_QB == 0 and WIDTH_B % tn == 0
    return pl.pallas_call(
        functools.partial(_in_proj_kernel, na=na, nqb=WIDTH_B // tn, tm=tm),
        grid=(T // tm, D_IN // tn),
        in_specs=[pl.BlockSpec(memory_space=pl.ANY),
                  pl.BlockSpec((1, D), lambda i, j: (0, 0)),
                  pl.BlockSpec((D, tn), lambda i, j: (0, j))],
        out_specs=[pl.BlockSpec((tm, tn), lambda i, j: (i, jnp.minimum(j, na - 1))),
                   pl.BlockSpec((tm, tn), lambda i, j: (i, jnp.maximum(j - na, 0)))],
        out_shape=[jax.ShapeDtypeStruct((T, PROJ_A), F32), jax.ShapeDtypeStruct((T, PROJ_R), BF16)],
        scratch_shapes=[pltpu.VMEM((tm, D), BF16), pltpu.VMEM((tm, D), F32), pltpu.SemaphoreType.DMA(())],
        compiler_params=_params(("arbitrary", "arbitrary")),
        name="in_proj",
    )(x2, g, w)


def _mixer_a_kernel(*refs, seq, n_cast):
    qkv = (refs[0:3], refs[3:6], refs[6:9])
    rev_ref = refs[9]
    o_ref = refs[10 + n_cast]
    (bias_ref, q_st, k_st, vx_st, s_ref, p_ref, mrow_ref, m_ref, l_ref, acc_ref) = refs[11 + 2 * n_cast:]
    _cast_blocks(refs[10:10 + n_cast], refs[11 + n_cast:11 + 2 * n_cast])
    lane = lax.broadcasted_iota(jnp.int32, (BLOCK, LANES), 1)
    lo = lane < HEAD_DIM

    row = lax.broadcasted_iota(jnp.int32, (2 * BLOCK, 2 * BLOCK), 0)
    col = lax.broadcasted_iota(jnp.int32, (2 * BLOCK, 2 * BLOCK), 1)
    rel = BLOCK + (row & (BLOCK - 1)) - col
    band = (rel >= 0) & (rel <= BLOCK)
    band_first = band & (col >= BLOCK)
    for g in range(N_GROUPS_A):
        halves = []
        for hh in range(2):
            x = jnp.broadcast_to(rev_ref[0, g * 2 + hh:g * 2 + hh + 1, :], (BLOCK, 4 * BLOCK))
            halves.append(pltpu.roll(x, 0, 1, stride=1, stride_axis=0)[:, BLOCK:3 * BLOCK])
        toep = jnp.concatenate(halves, axis=0)
        bias_ref[2 * g] = jnp.where(band, toep, NEG_INF)
        bias_ref[2 * g + 1] = jnp.where(band_first, toep, NEG_INF)

    for g in range(N_GROUPS_A):
        k_st[g, 0:BLOCK, :] = jnp.zeros((BLOCK, LANES), BF16)
        vx_st[g, 0:BLOCK, :] = jnp.zeros((BLOCK, 2 * LANES), BF16)
        vx_st[g, :, LANES:] = jnp.ones((BLOCK + seq, LANES), BF16)

    def scores(g, slot, base, first):
        q = q_st[g, base:base + BLOCK, :]
        zero = jnp.zeros_like(q)
        qz = jnp.concatenate([jnp.where(lo, q, zero), jnp.where(lo, zero, q)], axis=0)
        kw = k_st[g, base - BLOCK:base + BLOCK, :]
        s = lax.dot_general(qz, kw, (((1,), (1,)), ((), ())), preferred_element_type=F32)
        s_ref[slot] = s + bias_ref[2 * g + (1 if first else 0)]

    def softmax_group(slot, rg):
        rows = slice(rg * BF16_ROWS, (rg + 1) * BF16_ROWS)
        m = jnp.max(s_ref[slot, rows, :], axis=-1, keepdims=True)
        p_ref[slot, rows, :] = jnp.exp(s_ref[slot, rows, :] - m).astype(BF16)
        mrow_ref[slot, rows, :] = jnp.broadcast_to(m, (BF16_ROWS, LANES))

    def finish(g, slot, base, out_rows):
        acc = jnp.dot(p_ref[slot], vx_st[g, base - BLOCK:base + BLOCK, :], preferred_element_type=F32)
        mrow = mrow_ref[slot]
        m_ref[g, out_rows, :] = jnp.where(lo, mrow[:BLOCK], mrow[BLOCK:])
        l_ref[g, out_rows, :] = jnp.where(lo, acc[:BLOCK, LANES:], acc[BLOCK:, LANES:])
        acc_ref[g, out_rows, :] = jnp.where(lo, acc[:BLOCK, :LANES], acc[BLOCK:, :LANES])

    def merge(rows):
        ms = [m_ref[g, rows, :] for g in range(N_GROUPS_A)]
        mx = jnp.maximum(jnp.maximum(ms[0], ms[1]), ms[2])
        num = jnp.zeros((BLOCK, LANES), F32)
        den = jnp.zeros((BLOCK, LANES), F32)
        for g in range(N_GROUPS_A):
            w = jnp.exp(ms[g] - mx)
            num = num + w * acc_ref[g, rows, :]
            den = den + w * l_ref[g, rows, :]
        o_ref[0, rows, :] = (num / den).astype(o_ref.dtype)

    n_rg = 2 * BLOCK // BF16_ROWS
    blocks = []
    for g in sorted(range(N_GROUPS_A), key=lambda g: -DIL_PATTERNS[g][1]):
        d = DIL_PATTERNS[g][1]
        sub_len = seq // d
        q_ref, k_ref, v_ref = qkv[g]
        for r in range(d):
            src = pl.ds(r, sub_len, stride=d) if d > 1 else pl.ds(0, seq)
            dst = slice(BLOCK + r * sub_len, BLOCK + (r + 1) * sub_len)
            q_st[g, dst, :] = (q_ref[0, src, :] * SCALE).astype(BF16)
            k_st[g, dst, :] = k_ref[0, src, :].astype(BF16)
            vx_st[g, dst, :LANES] = v_ref[0, src, :].astype(BF16)
            for n in range(sub_len // BLOCK):
                base = BLOCK + r * sub_len + n * BLOCK
                out_rows = pl.ds(n * BLOCK * d + r, BLOCK, stride=d) if d > 1 else pl.ds(n * BLOCK, BLOCK)
                blocks.append((g, base, n == 0, out_rows))

    nslot = s_ref.shape[0]
    for t in range(len(blocks) + 2):
        if t < len(blocks):
            g, base, first, _ = blocks[t]
            scores(g, t % nslot, base, first)
        for rg in range(n_rg):
            if 1 <= t <= len(blocks):
                softmax_group((t - 1) % nslot, rg)
            if rg == n_rg // 2 and 2 <= t:
                g, base, _, out_rows = blocks[t - 2]
                finish(g, (t - 2) % nslot, base, out_rows)
                if DIL_PATTERNS[g][1] == 1:
                    merge(out_rows)


def _mixer_a(proj_a, rev_a, weights):
    B, S, _ = proj_a.shape
    npair = OUT_WIDTH_A // LANES
    grid = (B, npair)

    def col(which, g):
        base = (which * WIDTH_A + g * OUT_WIDTH_A) // LANES
        return pl.BlockSpec((1, S, LANES), lambda b, hp: (b, 0, base + hp))

    cast_in, cast_out, cast_shapes = _cast_specs(weights, grid)
    in_specs = [col(which, g) for g in range(N_GROUPS_A) for which in range(3)]
    in_specs.append(pl.BlockSpec((1, 2 * N_GROUPS_A, 4 * BLOCK), lambda b, hp: (hp, 0, 0)))
    return pl.pallas_call(
        functools.partial(_mixer_a_kernel, seq=S, n_cast=len(weights)),
        grid=grid,
        in_specs=in_specs + cast_in,
        out_specs=[pl.BlockSpec((1, S, LANES), lambda b, hp: (b, 0, hp))] + cast_out,
        out_shape=[jax.ShapeDtypeStruct((B, S, OUT_WIDTH_A), BF16)] + cast_shapes,
        scratch_shapes=[pltpu.VMEM((2 * N_GROUPS_A, 2 * BLOCK, 2 * BLOCK), F32),
                        pltpu.VMEM((N_GROUPS_A, BLOCK + S, LANES), BF16),
                        pltpu.VMEM((N_GROUPS_A, BLOCK + S, LANES), BF16),
                        pltpu.VMEM((N_GROUPS_A, BLOCK + S, 2 * LANES), BF16),
                        pltpu.VMEM((4, 2 * BLOCK, 2 * BLOCK), F32),
                        pltpu.VMEM((4, 2 * BLOCK, 2 * BLOCK), BF16),
                        pltpu.VMEM((4, 2 * BLOCK, LANES), F32),
                        pltpu.VMEM((N_GROUPS_A, S, LANES), F32),
                        pltpu.VMEM((N_GROUPS_A, S, LANES), F32),
                        pltpu.VMEM((N_GROUPS_A, S, LANES), F32)],
        compiler_params=_params(("arbitrary", "arbitrary")),
        name="mixer_a",
    )(*([proj_a] * 9), rev_a, *weights)


def _mixer_b_kernel(*refs, tq, seq, lam_init, n_cast):
    nq = seq // tq
    q_ref, k_ref, v_ref, rev_ref, lam_ref, g_ref = refs[:6]
    o_ref = refs[6 + n_cast]
    toep_ref, vx_ref = refs[7 + 2 * n_cast:9 + 2 * n_cast]
    s_refs = refs[9 + 2 * n_cast:9 + 2 * n_cast + nq]
    p_refs = refs[9 + 2 * n_cast + nq:]
    _cast_blocks(refs[6:6 + n_cast], refs[7 + n_cast:7 + 2 * n_cast])
    rg_rows = BF16_ROWS
    n_rg = 2 * tq // rg_rows

    x = jnp.broadcast_to(rev_ref[0] * LOG2E, (tq, seq + tq))
    rolled = pltpu.roll(x, 0, 1, stride=1, stride_axis=0)
    toep_ref[:, :seq] = rolled[:, :seq]
    row = lax.broadcasted_iota(jnp.int32, (tq, tq), 0)
    col = lax.broadcasted_iota(jnp.int32, (tq, tq), 1)
    toep_ref[:, seq:] = jnp.where(col <= row, rolled[:, seq:], NEG_INF)

    vx_ref[:, :LANES] = v_ref[0]
    vx_ref[:, LANES:] = jnp.ones((seq, LANES), BF16)

    lane = lax.broadcasted_iota(jnp.int32, (tq, LANES), 1)
    lo = lane < HEAD_DIM
    lv = lam_ref[...]
    lam = (jnp.exp(jnp.sum(lv[0:1] * lv[1:2], axis=-1, keepdims=True))
           - jnp.exp(jnp.sum(lv[2:3] * lv[3:4], axis=-1, keepdims=True)) + lam_init)

    def score_chunks(qi):
        q = q_ref[0, qi * tq:(qi + 1) * tq, :]
        zero = jnp.zeros_like(q)
        qz = jnp.concatenate([jnp.where(lo, q, zero), jnp.where(lo, zero, q)], axis=0)

        def chunk(c):
            kc = k_ref[0, c * tq:(c + 1) * tq, :]
            s_refs[qi][:, c * tq:(c + 1) * tq] = lax.dot_general(
                qz, kc, (((1,), (1,)), ((), ())), preferred_element_type=F32)

        return [functools.partial(chunk, c) for c in range(qi + 1)]

    def softmax_group(qi, rg):
        rows = slice(rg * rg_rows, (rg + 1) * rg_rows)
        brow = (rg * rg_rows) % tq
        bias = toep_ref[brow:brow + rg_rows, (nq - qi) * tq:(nq + 1) * tq]
        t = s_refs[qi][rows, :] + bias
        m = jnp.max(t, axis=-1, keepdims=True)
        p_refs[qi][rows, :] = jnp.exp2(t - m).astype(BF16)

    halves = {}

    def value_matmul(qi, half):
        acc = jnp.dot(p_refs[qi][half * tq:(half + 1) * tq, :], vx_ref[:(qi + 1) * tq, :],
                      preferred_element_type=F32)
        halves[qi, half] = acc[:, :LANES] / acc[:, LANES:]

    def finish(qi):
        y = halves[qi, 0] - lam * halves[qi, 1]
        y = _rms(y, g_ref[...]) * (1.0 - lam_init)
        o_ref[0, qi * tq:(qi + 1) * tq, :] = y.astype(o_ref.dtype)

    order = list(range(nq - 1, -1, -1))
    for t in range(nq + 2):
        scores_t = score_chunks(order[t]) if t < nq else []
        softmax_t, tail_t = [], []
        if 1 <= t <= nq:
            qi = order[t - 1]
            groups = [functools.partial(softmax_group, qi, rg) for rg in range(n_rg)]
            softmax_t = groups[:n_rg // 2] + [functools.partial(value_matmul, qi, 0)] + groups[n_rg // 2:]
        if t >= 2:
            qi = order[t - 2]
            tail_t = [functools.partial(value_matmul, qi, 1), functools.partial(finish, qi)]
        for emit in _interleave([scores_t, softmax_t, tail_t]):
            emit()


def _mixer_b(proj, rev_b, lam_vecs, subln_g, lam_init, weights, tq=256):
    B, S, _ = proj.shape
    H = N_HEADS_B
    nq = S // tq
    grid = (B, H)
    kern = functools.partial(_mixer_b_kernel, tq=tq, seq=S, lam_init=lam_init, n_cast=len(weights))

    def col(off):
        return pl.BlockSpec((1, S, LANES), lambda b, h: (b, 0, off // LANES + h))

    cast_in, cast_out, cast_shapes = _cast_specs(weights, grid)
    return pl.pallas_call(
        kern,
        grid=grid,
        in_specs=[col(OFF_QB), col(OFF_KB), col(OFF_VB),
                  pl.BlockSpec((1, 1, S + tq), lambda b, h: (h, 0, 0)),
                  pl.BlockSpec((4, HEAD_DIM), lambda b, h: (0, 0)),
                  pl.BlockSpec((1, 2 * HEAD_DIM), lambda b, h: (0, 0))] + cast_in,
        out_specs=[pl.BlockSpec((1, S, LANES), lambda b, h: (b, 0, h))] + cast_out,
        out_shape=[jax.ShapeDtypeStruct((B, S, WIDTH_B), BF16)] + cast_shapes,
        scratch_shapes=([pltpu.VMEM((tq, S + tq), F32), pltpu.VMEM((S, 2 * LANES), BF16)]
                        + [pltpu.VMEM((2 * tq, (i + 1) * tq), F32) for i in range(nq)]
                        + [pltpu.VMEM((2 * tq, (i + 1) * tq), BF16) for i in range(nq)]),
        compiler_params=_params(("arbitrary", "arbitrary")),
        name="mixer_b",
    )(proj, proj, proj, rev_b, lam_vecs, subln_g, *weights)


def _attn_out_kernel(ya_ref, yb_ref, ga_ref, gb_ref, x_ref, wa_ref, wb_ref, wo_ref, g_ref, x1_ref, h_ref):
    pa = jnp.dot(ya_ref[...], wa_ref[...], preferred_element_type=F32)
    pb = jnp.dot(yb_ref[...], wb_ref[...], preferred_element_type=F32)
    ga = jax.nn.sigmoid(ga_ref[...].astype(F32))
    gb = jax.nn.sigmoid(gb_ref[...].astype(F32))
    merged = (ga * pa + gb * pb).astype(BF16)
    x1_ref[...] = x_ref[...] + jnp.dot(merged, wo_ref[...], preferred_element_type=F32)
    h_ref[...] = _rms(x1_ref[...], g_ref[...]).astype(BF16)


def _attn_out(ya, yb, proj_r, x2, wa, wb, wo, g, tm=512):
    T, D = x2.shape

    def resident(shape):
        return pl.BlockSpec(shape, lambda i: (0, 0), pipeline_mode=pl.Buffered(1))

    return pl.pallas_call(
        _attn_out_kernel,
        grid=(T // tm,),
        in_specs=[pl.BlockSpec((tm, OUT_WIDTH_A), lambda i: (i, 0)),
                  pl.BlockSpec((tm, WIDTH_B), lambda i: (i, 0)),
                  pl.BlockSpec((tm, D), lambda i: (i, OFF_GA // D)),
                  pl.BlockSpec((tm, D), lambda i: (i, OFF_GB // D)),
                  pl.BlockSpec((tm, D), lambda i: (i, 0)),
                  resident((OUT_WIDTH_A, D)), resident((WIDTH_B, D)), resident((D, D)), resident((1, D))],
        out_specs=[pl.BlockSpec((tm, D), lambda i: (i, 0)), pl.BlockSpec((tm, D), lambda i: (i, 0))],
        out_shape=[jax.ShapeDtypeStruct((T, D), F32), jax.ShapeDtypeStruct((T, D), BF16)],
        compiler_params=_params(("parallel",), vmem=VMEM_LIMIT_ATTN_OUT),
        name="attn_out",
    )(ya, yb, proj_r, proj_r, x2, wa, wb, wo, g)


def _ffn_kernel(x1_hbm, h_ref, wg_ref, wu_ref, wd_ref, gf_ref, o_ref, x1_buf, x1_sem, *, tm):
    i = pl.program_id(0)
    f = pl.program_id(1)
    x1_copy = pltpu.make_async_copy(x1_hbm.at[pl.ds(i * tm, tm), :], x1_buf, x1_sem)

    @pl.when(f == 0)
    def _():
        x1_copy.start()
        o_ref[...] = jnp.zeros(o_ref.shape, F32)

    h = h_ref[...]
    a = jnp.dot(h, wg_ref[...], preferred_element_type=F32)
    b = jnp.dot(h, wu_ref[...], preferred_element_type=F32)
    u = (a * jax.nn.sigmoid(a)) * b
    o_ref[...] += jnp.dot(u.astype(BF16), wd_ref[...], preferred_element_type=F32)

    @pl.when(f == pl.num_programs(1) - 1)
    def _():
        x1_copy.wait()
        o_ref[...] = _rms(x1_buf[...] + o_ref[...], gf_ref[...])


def _ffn(x1, h, wg, wu, wd, gf, tm=1024, tf=512):
    T, D = x1.shape
    F = wg.shape[1]
    return pl.pallas_call(
        functools.partial(_ffn_kernel, tm=tm),
        grid=(T // tm, F // tf),
        in_specs=[pl.BlockSpec(memory_space=pl.ANY),
                  pl.BlockSpec((tm, D), lambda i, f: (i, 0)),
                  pl.BlockSpec((D, tf), lambda i, f: (0, f)),
                  pl.BlockSpec((D, tf), lambda i, f: (0, f)),
                  pl.BlockSpec((tf, D), lambda i, f: (f, 0)),
                  pl.BlockSpec((1, D), lambda i, f: (0, 0))],
        out_specs=pl.BlockSpec((tm, D), lambda i, f: (i, 0)),
        out_shape=jax.ShapeDtypeStruct((T, D), F32),
        scratch_shapes=[pltpu.VMEM((tm, D), F32), pltpu.SemaphoreType.DMA(())],
        compiler_params=_params(("parallel", "arbitrary"), vmem=VMEM_LIMIT_FFN),
        name="ffn",
    )(x1, h, wg, wu, wd, gf)


def _rev_a_index():
    u = np.arange(4 * BLOCK)
    rel = np.clip(2 * BLOCK - u, 0, None)
    return np.stack([_rel_bucket_np(rel * d) for _, d in DIL_PATTERNS])


def _rev_b_index(seq, tq):
    c = np.arange(seq + tq)
    return _rel_bucket_np(np.clip(seq - c, 0, seq - 1))


def _lookup(table, idx):
    onehot = jnp.asarray(np.asarray(idx)[None, :] == np.arange(NUM_BUCKETS)[:, None])
    return jnp.sum(jnp.where(onehot[:, None, :], table[:, :, None], 0.0), axis=0)


def kernel(x, norm_attn_g, w_in, w_proj_a, w_proj_b, w_out, rel_bias_table, diff_lambda_q1, diff_lambda_k1, diff_lambda_q2, diff_lambda_k2, diff_subln_g, norm_ffn_g, w_ffn_gate, w_ffn_up, w_ffn_down, norm_final_g):
    B, S, D = x.shape
    T = B * S
    depth = w_in.shape[0]
    assert depth == 1, "the final RMSNorm is fused into the FFN epilogue of a single layer"
    table_a = rel_bias_table[:, :N_HEADS_A].astype(F32)
    table_b = rel_bias_table[:, N_HEADS_A:].astype(F32)
    tq = 256

    idx_a = _rev_a_index()
    rev_a = jnp.stack([_lookup(table_a[:, g * HEADS_PER_GROUP_A:(g + 1) * HEADS_PER_GROUP_A], idx_a[g])
                       for g in range(N_GROUPS_A)])
    npair = OUT_WIDTH_A // LANES
    rev_a = jnp.transpose(rev_a.reshape(N_GROUPS_A, npair, 2, 4 * BLOCK), (1, 0, 2, 3))
    rev_a = rev_a.reshape(npair, 2 * N_GROUPS_A, 4 * BLOCK)
    rev_b = _lookup(table_b, _rev_b_index(S, tq))[:, None, :]

    x2 = x.reshape(T, D)
    l = 0
    lam_init = 0.8 - 0.6 * math.exp(-0.3 * l)
    proj_a, proj_r = _in_proj(x2, norm_attn_g[l][None, :], w_in[l])
    ya, wa, wb, wo = _mixer_a(proj_a.reshape(B, S, PROJ_A), rev_a, [w_proj_a[l], w_proj_b[l], w_out[l]])

    lam_vecs = jnp.stack([diff_lambda_q1[l], diff_lambda_k1[l],
                          diff_lambda_q2[l], diff_lambda_k2[l]]).astype(F32)
    yb, wg, wu, wd = _mixer_b(proj_r.reshape(B, S, PROJ_R), rev_b, lam_vecs, diff_subln_g[l][None, :].astype(F32),
                              lam_init, [w_ffn_gate[l], w_ffn_up[l], w_ffn_down[l]], tq=tq)

    x1, h = _attn_out(ya.reshape(T, OUT_WIDTH_A), yb.reshape(T, WIDTH_B), proj_r, x2, wa, wb, wo,
                      norm_ffn_g[l][None, :])
    out = _ffn(x1, h, wg, wu, wd, norm_final_g[None, :])
    return out.reshape(B, S, D)
```

```python
import functools
import math

import numpy as np
import jax
import jax.numpy as jnp
from jax import lax
from jax.experimental import pallas as pl
from jax.experimental.pallas import tpu as pltpu

D_MODEL = 2048
HEAD_DIM = 64
DIL_PATTERNS = ((128, 1), (512, 4), (2048, 16))
N_GROUPS_A = len(DIL_PATTERNS)
HEADS_PER_GROUP_A = 8
N_HEADS_A = N_GROUPS_A * HEADS_PER_GROUP_A
WIDTH_A = N_HEADS_A * HEAD_DIM
OUT_WIDTH_A = HEADS_PER_GROUP_A * HEAD_DIM
BLOCK = 128
N_HEADS_B = D_MODEL // (2 * HEAD_DIM)
WIDTH_B = N_HEADS_B * 2 * HEAD_DIM
NUM_BUCKETS = 32
MAX_DISTANCE = 2048
D_FF = -(-8 * D_MODEL // (3 * 256)) * 256
PROJ_A = 3 * WIDTH_A
PROJ_R = 3 * WIDTH_B + 2 * D_MODEL
D_IN = PROJ_A + PROJ_R
NORM_EPS = 1e-6
NEG_INF = -1e30
SCALE = HEAD_DIM ** -0.5
LOG2E = math.log2(math.e)

OFF_QB = 0
OFF_KB = WIDTH_B
OFF_VB = 2 * WIDTH_B
OFF_GA = 3 * WIDTH_B
OFF_GB = OFF_GA + D_MODEL

LANES = 128
VMEM_LIMIT = 56 * 1024 * 1024
VMEM_LIMIT_ATTN_OUT = 58 * 1024 * 1024
VMEM_LIMIT_FFN = 60 * 1024 * 1024

BF16 = jnp.bfloat16
F32 = jnp.float32


def _rel_bucket_np(dist):
    n = np.maximum(dist, 0)
    max_exact = NUM_BUCKETS // 2
    nf = np.maximum(n, 1).astype(np.float32)
    large = max_exact + (np.log(nf / np.float32(max_exact)) / np.float32(math.log(MAX_DISTANCE / max_exact))
                         * np.float32(NUM_BUCKETS - max_exact)).astype(np.int32)
    large = np.minimum(large, NUM_BUCKETS - 1)
    return np.where(n < max_exact, n, large).astype(np.int32)


def _rms(x, g):
    ms = jnp.mean(x * x, axis=-1, keepdims=True)
    return x * lax.rsqrt(ms + NORM_EPS) * g


def _params(sem, vmem=VMEM_LIMIT):
    return pltpu.CompilerParams(dimension_semantics=sem, vmem_limit_bytes=vmem)


BF16_ROWS = 16


def _cast_specs(weights, grid):
    nsteps = math.prod(grid)
    in_specs, out_specs, out_shapes = [], [], []
    for w in weights:
        rows, cols = w.shape
        blk = next(r for r in range(BF16_ROWS, rows + 1, BF16_ROWS)
                   if rows % r == 0 and nsteps % (rows // r) == 0 and rows // r <= nsteps)
        per = nsteps // (rows // blk)

        def index(*ids, per=per):
            step = 0
            for i, n in zip(ids, grid):
                step = step * n + i
            return (step // per, 0)

        in_specs.append(pl.BlockSpec((blk, cols), index))
        out_specs.append(pl.BlockSpec((blk, cols), index))
        out_shapes.append(jax.ShapeDtypeStruct((rows, cols), BF16))
    return in_specs, out_specs, out_shapes


def _interleave(lists):
    keyed = [((i + 0.5) / len(items), n, i, item) for n, items in enumerate(lists) for i, item in enumerate(items)]
    return [item for _, _, _, item in sorted(keyed, key=lambda k: k[:3])]


def _cast_blocks(in_refs, out_refs):
    for i_ref, o_ref in zip(in_refs, out_refs):
        o_ref[...] = i_ref[...].astype(BF16)


def _in_proj_kernel(x_hbm, g_ref, w_ref, oa_ref, or_ref, h_ref, x_buf, x_sem, *, na, nqb, tm):
    i = pl.program_id(0)
    j = pl.program_id(1)

    def x_copy(tile):
        return pltpu.make_async_copy(x_hbm.at[pl.ds(tile * tm, tm), :], x_buf, x_sem)

    @pl.when(j == 0)
    def _():
        @pl.when(i == 0)
        def _():
            x_copy(0).start()

        x_copy(i).wait()
        h_ref[...] = _rms(x_buf[...], g_ref[...]).astype(BF16)

        @pl.when(i + 1 < pl.num_programs(0))
        def _():
            x_copy(i + 1).start()

    @pl.when(j < na)
    def _():
        oa_ref[...] = jnp.dot(h_ref[...], w_ref[...].astype(BF16), preferred_element_type=F32)

    @pl.when(j >= na)
    def _():
        scale = jnp.where(j < na + nqb, jnp.float32(SCALE * LOG2E), jnp.float32(1.0))
        res = jnp.dot(h_ref[...], w_ref[...].astype(BF16), preferred_element_type=F32)
        or_ref[...] = (res * scale).astype(BF16)


def _in_proj(x2, g, w, tm=2048, tn=512):
    T, D = x2.shape
    na = PROJ_A // tn
    assert OFF_QB == 0 and WIDTH_B % tn == 0
    return pl.pallas_call(
        functools.partial(_in_proj_kernel, na=na, nqb=WIDTH_B // tn, tm=tm),
        grid=(T // tm, D_IN // tn),
        in_specs=[pl.BlockSpec(memory_space=pl.ANY),
                  pl.BlockSpec((1, D), lambda i, j: (0, 0)),
                  pl.BlockSpec((D, tn), lambda i, j: (0, j))],
        out_specs=[pl.BlockSpec((tm, tn), lambda i, j: (i, jnp.minimum(j, na - 1))),
                   pl.BlockSpec((tm, tn), lambda i, j: (i, jnp.maximum(j - na, 0)))],
        out_shape=[jax.ShapeDtypeStruct((T, PROJ_A), F32), jax.ShapeDtypeStruct((T, PROJ_R), BF16)],
        scratch_shapes=[pltpu.VMEM((tm, D), BF16), pltpu.VMEM((tm, D), F32), pltpu.SemaphoreType.DMA(())],
        compiler_params=_params(("arbitrary", "arbitrary")),
        name="in_proj",
    )(x2, g, w)


def _mixer_a_kernel(*refs, seq, n_cast):
    qkv = (refs[0:3], refs[3:6], refs[6:9])
    rev_ref = refs[9]
    o_ref = refs[10 + n_cast]
    (bias_ref, q_st, k_st, vx_st, s_ref, p_ref, mrow_ref, m_ref, l_ref, acc_ref) = refs[11 + 2 * n_cast:]
    _cast_blocks(refs[10:10 + n_cast], refs[11 + n_cast:11 + 2 * n_cast])
    lane = lax.broadcasted_iota(jnp.int32, (BLOCK, LANES), 1)
    lo = lane < HEAD_DIM

    row = lax.broadcasted_iota(jnp.int32, (2 * BLOCK, 2 * BLOCK), 0)
    col = lax.broadcasted_iota(jnp.int32, (2 * BLOCK, 2 * BLOCK), 1)
    rel = BLOCK + (row & (BLOCK - 1)) - col
    band = (rel >= 0) & (rel <= BLOCK)
    band_first = band & (col >= BLOCK)
    for g in range(N_GROUPS_A):
        halves = []
        for hh in range(2):
            x = jnp.broadcast_to(rev_ref[0, g * 2 + hh:g * 2 + hh + 1, :], (BLOCK, 4 * BLOCK))
            halves.append(pltpu.roll(x, 0, 1, stride=1, stride_axis=0)[:, BLOCK:3 * BLOCK])
        toep = jnp.concatenate(halves, axis=0)
        bias_ref[2 * g] = jnp.where(band, toep, NEG_INF)
        bias_ref[2 * g + 1] = jnp.where(band_first, toep, NEG_INF)

    for g in range(N_GROUPS_A):
        k_st[g, 0:BLOCK, :] = jnp.zeros((BLOCK, LANES), BF16)
        vx_st[g, 0:BLOCK, :] = jnp.zeros((BLOCK, 2 * LANES), BF16)
        vx_st[g, :, LANES:] = jnp.ones((BLOCK + seq, LANES), BF16)

    def scores(g, slot, base, first):
        q = q_st[g, base:base + BLOCK, :]
        zero = jnp.zeros_like(q)
        qz = jnp.concatenate([jnp.where(lo, q, zero), jnp.where(lo, zero, q)], axis=0)
        kw = k_st[g, base - BLOCK:base + BLOCK, :]
        s = lax.dot_general(qz, kw, (((1,), (1,)), ((), ())), preferred_element_type=F32)
        s_ref[slot] = s + bias_ref[2 * g + (1 if first else 0)]

    def softmax_group(slot, rg):
        rows = slice(rg * BF16_ROWS, (rg + 1) * BF16_ROWS)
        m = jnp.max(s_ref[slot, rows, :], axis=-1, keepdims=True)
        p_ref[slot, rows, :] = jnp.exp(s_ref[slot, rows, :] - m).astype(BF16)
        mrow_ref[slot, rows, :] = jnp.broadcast_to(m, (BF16_ROWS, LANES))

    def finish(g, slot, base, out_rows):
        acc = jnp.dot(p_ref[slot], vx_st[g, base - BLOCK:base + BLOCK, :], preferred_element_type=F32)
        mrow = mrow_ref[slot]
        m_ref[g, out_rows, :] = jnp.where(lo, mrow[:BLOCK], mrow[BLOCK:])
        l_ref[g, out_rows, :] = jnp.where(lo, acc[:BLOCK, LANES:], acc[BLOCK:, LANES:])
        acc_ref[g, out_rows, :] = jnp.where(lo, acc[:BLOCK, :LANES], acc[BLOCK:, :LANES])

    def merge(rows):
        ms = [m_ref[g, rows, :] for g in range(N_GROUPS_A)]
        mx = jnp.maximum(jnp.maximum(ms[0], ms[1]), ms[2])
        num = jnp.zeros((BLOCK, LANES), F32)
        den = jnp.zeros((BLOCK, LANES), F32)
        for g in range(N_GROUPS_A):
            w = jnp.exp(ms[g] - mx)
            num = num + w * acc_ref[g, rows, :]
            den = den + w * l_ref[g, rows, :]
        o_ref[0, rows, :] = (num / den).astype(o_ref.dtype)

    n_rg = 2 * BLOCK // BF16_ROWS
    blocks = []
    for g in sorted(range(N_GROUPS_A), key=lambda g: -DIL_PATTERNS[g][1]):
        d = DIL_PATTERNS[g][1]
        sub_len = seq // d
        q_ref, k_ref, v_ref = qkv[g]
        for r in range(d):
            src = pl.ds(r, sub_len, stride=d) if d > 1 else pl.ds(0, seq)
            dst = slice(BLOCK + r * sub_len, BLOCK + (r + 1) * sub_len)
            q_st[g, dst, :] = (q_ref[0, src, :] * SCALE).astype(BF16)
            k_st[g, dst, :] = k_ref[0, src, :].astype(BF16)
            vx_st[g, dst, :LANES] = v_ref[0, src, :].astype(BF16)
            for n in range(sub_len // BLOCK):
                base = BLOCK + r * sub_len + n * BLOCK
                out_rows = pl.ds(n * BLOCK * d + r, BLOCK, stride=d) if d > 1 else pl.ds(n * BLOCK, BLOCK)
                blocks.append((g, base, n == 0, out_rows))

    nslot = s_ref.shape[0]
    for t in range(len(blocks) + 2):
        if t < len(blocks):
            g, base, first, _ = blocks[t]
            scores(g, t % nslot, base, first)
        for rg in range(n_rg):
            if 1 <= t <= len(blocks):
                softmax_group((t - 1) % nslot, rg)
            if rg == n_rg // 2 and 2 <= t:
                g, base, _, out_rows = blocks[t - 2]
                finish(g, (t - 2) % nslot, base, out_rows)
                if DIL_PATTERNS[g][1] == 1:
                    merge(out_rows)


def _mixer_a(proj_a, rev_a, weights):
    B, S, _ = proj_a.shape
    npair = OUT_WIDTH_A // LANES
    grid = (B, npair)

    def col(which, g):
        base = (which * WIDTH_A + g * OUT_WIDTH_A) // LANES
        return pl.BlockSpec((1, S, LANES), lambda b, hp: (b, 0, base + hp))

    cast_in, cast_out, cast_shapes = _cast_specs(weights, grid)
    in_specs = [col(which, g) for g in range(N_GROUPS_A) for which in range(3)]
    in_specs.append(pl.BlockSpec((1, 2 * N_GROUPS_A, 4 * BLOCK), lambda b, hp: (hp, 0, 0)))
    return pl.pallas_call(
        functools.partial(_mixer_a_kernel, seq=S, n_cast=len(weights)),
        grid=grid,
        in_specs=in_specs + cast_in,
        out_specs=[pl.BlockSpec((1, S, LANES), lambda b, hp: (b, 0, hp))] + cast_out,
        out_shape=[jax.ShapeDtypeStruct((B, S, OUT_WIDTH_A), BF16)] + cast_shapes,
        scratch_shapes=[pltpu.VMEM((2 * N_GROUPS_A, 2 * BLOCK, 2 * BLOCK), F32),
                        pltpu.VMEM((N_GROUPS_A, BLOCK + S, LANES), BF16),
                        pltpu.VMEM((N_GROUPS_A, BLOCK + S, LANES), BF16),
                        pltpu.VMEM((N_GROUPS_A, BLOCK + S, 2 * LANES), BF16),
                        pltpu.VMEM((4, 2 * BLOCK, 2 * BLOCK), F32),
                        pltpu.VMEM((4, 2 * BLOCK, 2 * BLOCK), BF16),
                        pltpu.VMEM((4, 2 * BLOCK, LANES), F32),
                        pltpu.VMEM((N_GROUPS_A, S, LANES), F32),
                        pltpu.VMEM((N_GROUPS_A, S, LANES), F32),
                        pltpu.VMEM((N_GROUPS_A, S, LANES), F32)],
        compiler_params=_params(("arbitrary", "arbitrary")),
        name="mixer_a",
    )(*([proj_a] * 9), rev_a, *weights)


def _mixer_b_kernel(*refs, tq, seq, lam_init, n_cast):
    nq = seq // tq
    q_ref, k_ref, v_ref, fwd_ref, lam_ref, gmat_ref = refs[:6]
    o_ref = refs[6 + n_cast]
    toep_ref, kz_ref, vxt_ref = refs[7 + 2 * n_cast:10 + 2 * n_cast]
    s_refs = refs[10 + 2 * n_cast:]
    _cast_blocks(refs[6:6 + n_cast], refs[7 + n_cast:7 + 2 * n_cast])
    nt = (((1,), (1,)), ((), ()))

    x = jnp.broadcast_to(fwd_ref[0] * LOG2E, (tq, seq + tq))
    toep_ref[...] = pltpu.roll(x, 0, 1, stride=1, stride_axis=0)

    lane = lax.broadcasted_iota(jnp.int32, (tq, LANES), 1)
    lo = lane < HEAD_DIM
    for c in range(nq):
        kc = k_ref[0, c * tq:(c + 1) * tq, :]
        zero = jnp.zeros_like(kc)
        kz_ref[c] = jnp.concatenate([jnp.where(lo, kc, zero), jnp.where(lo, zero, kc)], axis=0)

    def eye(n):
        r = lax.broadcasted_iota(jnp.int32, (n, n), 0)
        c = lax.broadcasted_iota(jnp.int32, (n, n), 1)
        return jnp.where(r == c, 1.0, 0.0).astype(BF16)

    vxt_ref[:LANES, :] = lax.dot_general(eye(LANES), v_ref[0], nt, preferred_element_type=F32).astype(BF16)
    vxt_ref[LANES:, :] = jnp.ones((BF16_ROWS, seq), BF16)
    eye_q = eye(tq)

    lv = lam_ref[...]
    lam = (jnp.exp(jnp.sum(lv[0:1] * lv[1:2], axis=-1, keepdims=True))
           - jnp.exp(jnp.sum(lv[2:3] * lv[3:4], axis=-1, keepdims=True)) + lam_init)
    gmat = jnp.concatenate([gmat_ref[...]] * (tq // LANES), axis=1)

    def score_chunks(qi):
        qb = q_ref[0, qi * tq:(qi + 1) * tq, :]

        def chunk(c):
            s_refs[qi][c] = lax.dot_general(kz_ref[c], qb, nt, preferred_element_type=F32)

        return [functools.partial(chunk, c) for c in range(qi + 1)]

    def biased(qi, c, half):
        off = (qi - c + 1) * tq
        return s_refs[qi][c, half * tq:(half + 1) * tq, :] + toep_ref[:, off:off + tq]

    stats = {}

    def col_max(qi, half, c):
        m = jnp.max(biased(qi, c, half), axis=0, keepdims=True)
        stats[qi, half] = m if c == 0 else jnp.maximum(stats[qi, half], m)

    accs = {}

    def value_chunk(qi, half, c):
        p = jnp.exp2(biased(qi, c, half) - stats[qi, half]).astype(BF16)
        part = jnp.dot(vxt_ref[:, c * tq:(c + 1) * tq], p, preferred_element_type=F32)
        accs[qi, half] = part if c == 0 else accs[qi, half] + part

    def finish(qi):
        o = [accs[qi, h][:LANES] / accs[qi, h][LANES:LANES + 1] for h in range(2)]
        y = o[0] - lam * o[1]
        ms = jnp.sum(y * y, axis=0, keepdims=True) * (1.0 / (2 * HEAD_DIM))
        y = (y * lax.rsqrt(ms + NORM_EPS) * gmat * (1.0 - lam_init)).astype(o_ref.dtype)
        o_ref[0, qi * tq:(qi + 1) * tq, :] = lax.dot_general(
            eye_q, y, nt, preferred_element_type=F32).astype(o_ref.dtype)

    order = list(range(nq - 1, -1, -1))
    for t in range(nq + 3):
        stages = [score_chunks(order[t]) if t < nq else [],
                  [functools.partial(col_max, order[t - 1], half, c)
                   for half in range(2) for c in range(order[t - 1] + 1)] if 1 <= t <= nq else [],
                  [functools.partial(value_chunk, order[t - 2], half, c)
                   for half in range(2) for c in range(order[t - 2] + 1)] if 2 <= t <= nq + 1 else [],
                  [functools.partial(finish, order[t - 3])] if t >= 3 else []]
        for emit in _interleave(stages):
            emit()


def _mixer_b(proj, fwd_b, lam_vecs, gmat, lam_init, weights, tq=256):
    B, S, _ = proj.shape
    H = N_HEADS_B
    nq = S // tq
    grid = (B, H)
    kern = functools.partial(_mixer_b_kernel, tq=tq, seq=S, lam_init=lam_init, n_cast=len(weights))

    def col(off):
        return pl.BlockSpec((1, S, LANES), lambda b, h: (b, 0, off // LANES + h))

    cast_in, cast_out, cast_shapes = _cast_specs(weights, grid)
    return pl.pallas_call(
        kern,
        grid=grid,
        in_specs=[col(OFF_QB), col(OFF_KB), col(OFF_VB),
                  pl.BlockSpec((1, 1, S + tq), lambda b, h: (h, 0, 0)),
                  pl.BlockSpec((4, HEAD_DIM), lambda b, h: (0, 0)),
                  pl.BlockSpec((2 * HEAD_DIM, LANES), lambda b, h: (0, 0))] + cast_in,
        out_specs=[pl.BlockSpec((1, S, LANES), lambda b, h: (b, 0, h))] + cast_out,
        out_shape=[jax.ShapeDtypeStruct((B, S, WIDTH_B), BF16)] + cast_shapes,
        scratch_shapes=([pltpu.VMEM((tq, S + tq), F32), pltpu.VMEM((nq, 2 * tq, LANES), BF16),
                         pltpu.VMEM((LANES + BF16_ROWS, S), BF16)]
                        + [pltpu.VMEM((i + 1, 2 * tq, tq), F32) for i in range(nq)]),
        compiler_params=_params(("arbitrary", "arbitrary")),
        name="mixer_b",
    )(proj, proj, proj, fwd_b, lam_vecs, gmat, *weights)


def _attn_out_kernel(ya_ref, yb_ref, ga_ref, gb_ref, x_ref, wa_ref, wb_ref, wo_ref, g_ref, x1_ref, h_ref):
    pa = jnp.dot(ya_ref[...], wa_ref[...], preferred_element_type=F32)
    pb = jnp.dot(yb_ref[...], wb_ref[...], preferred_element_type=F32)
    ga = jax.nn.sigmoid(ga_ref[...].astype(F32))
    gb = jax.nn.sigmoid(gb_ref[...].astype(F32))
    merged = (ga * pa + gb * pb).astype(BF16)
    x1_ref[...] = x_ref[...] + jnp.dot(merged, wo_ref[...], preferred_element_type=F32)
    h_ref[...] = _rms(x1_ref[...], g_ref[...]).astype(BF16)


def _attn_out(ya, yb, proj_r, x2, wa, wb, wo, g, tm=512):
    T, D = x2.shape

    def resident(shape):
        return pl.BlockSpec(shape, lambda i: (0, 0), pipeline_mode=pl.Buffered(1))

    return pl.pallas_call(
        _attn_out_kernel,
        grid=(T // tm,),
        in_specs=[pl.BlockSpec((tm, OUT_WIDTH_A), lambda i: (i, 0)),
                  pl.BlockSpec((tm, WIDTH_B), lambda i: (i, 0)),
                  pl.BlockSpec((tm, D), lambda i: (i, OFF_GA // D)),
                  pl.BlockSpec((tm, D), lambda i: (i, OFF_GB // D)),
                  pl.BlockSpec((tm, D), lambda i: (i, 0)),
                  resident((OUT_WIDTH_A, D)), resident((WIDTH_B, D)), resident((D, D)), resident((1, D))],
        out_specs=[pl.BlockSpec((tm, D), lambda i: (i, 0)), pl.BlockSpec((tm, D), lambda i: (i, 0))],
        out_shape=[jax.ShapeDtypeStruct((T, D), F32), jax.ShapeDtypeStruct((T, D), BF16)],
        compiler_params=_params(("parallel",), vmem=VMEM_LIMIT_ATTN_OUT),
        name="attn_out",
    )(ya, yb, proj_r, proj_r, x2, wa, wb, wo, g)


def _ffn_kernel(x1_hbm, h_ref, wg_ref, wu_ref, wd_ref, gf_ref, o_ref, x1_buf, x1_sem, *, tm):
    i = pl.program_id(0)
    f = pl.program_id(1)
    x1_copy = pltpu.make_async_copy(x1_hbm.at[pl.ds(i * tm, tm), :], x1_buf, x1_sem)

    @pl.when(f == 0)
    def _():
        x1_copy.start()
        o_ref[...] = jnp.zeros(o_ref.shape, F32)

    h = h_ref[...]
    a = jnp.dot(h, wg_ref[...], preferred_element_type=F32)
    b = jnp.dot(h, wu_ref[...], preferred_element_type=F32)
    u = (a * jax.nn.sigmoid(a)) * b
    o_ref[...] += jnp.dot(u.astype(BF16), wd_ref[...], preferred_element_type=F32)

    @pl.when(f == pl.num_programs(1) - 1)
    def _():
        x1_copy.wait()
        o_ref[...] = _rms(x1_buf[...] + o_ref[...], gf_ref[...])


def _ffn(x1, h, wg, wu, wd, gf, tm=1024, tf=512):
    T, D = x1.shape
    F = wg.shape[1]
    return pl.pallas_call(
        functools.partial(_ffn_kernel, tm=tm),
        grid=(T // tm, F // tf),
        in_specs=[pl.BlockSpec(memory_space=pl.ANY),
                  pl.BlockSpec((tm, D), lambda i, f: (i, 0)),
                  pl.BlockSpec((D, tf), lambda i, f: (0, f)),
                  pl.BlockSpec((D, tf), lambda i, f: (0, f)),
                  pl.BlockSpec((tf, D), lambda i, f: (f, 0)),
                  pl.BlockSpec((1, D), lambda i, f: (0, 0))],
        out_specs=pl.BlockSpec((tm, D), lambda i, f: (i, 0)),
        out_shape=jax.ShapeDtypeStruct((T, D), F32),
        scratch_shapes=[pltpu.VMEM((tm, D), F32), pltpu.SemaphoreType.DMA(())],
        compiler_params=_params(("parallel", "arbitrary"), vmem=VMEM_LIMIT_FFN),
        name="ffn",
    )(x1, h, wg, wu, wd, gf)


def _rev_a_index():
    u = np.arange(4 * BLOCK)
    rel = np.clip(2 * BLOCK - u, 0, None)
    return np.stack([_rel_bucket_np(rel * d) for _, d in DIL_PATTERNS])


def _fwd_b_index(seq, tq):
    return _rel_bucket_np(np.clip(np.arange(seq + tq) - tq, 0, seq - 1))


def _lookup(table, idx):
    onehot = jnp.asarray(np.asarray(idx)[None, :] == np.arange(NUM_BUCKETS)[:, None])
    return jnp.sum(jnp.where(onehot[:, None, :], table[:, :, None], 0.0), axis=0)


def kernel(x, norm_attn_g, w_in, w_proj_a, w_proj_b, w_out, rel_bias_table, diff_lambda_q1, diff_lambda_k1, diff_lambda_q2, diff_lambda_k2, diff_subln_g, norm_ffn_g, w_ffn_gate, w_ffn_up, w_ffn_down, norm_final_g):
    B, S, D = x.shape
    T = B * S
    depth = w_in.shape[0]
    assert depth == 1, "the final RMSNorm is fused into the FFN epilogue of a single layer"
    table_a = rel_bias_table[:, :N_HEADS_A].astype(F32)
    table_b = rel_bias_table[:, N_HEADS_A:].astype(F32)
    tq = 256

    idx_a = _rev_a_index()
    rev_a = jnp.stack([_lookup(table_a[:, g * HEADS_PER_GROUP_A:(g + 1) * HEADS_PER_GROUP_A], idx_a[g])
                       for g in range(N_GROUPS_A)])
    npair = OUT_WIDTH_A // LANES
    rev_a = jnp.transpose(rev_a.reshape(N_GROUPS_A, npair, 2, 4 * BLOCK), (1, 0, 2, 3))
    rev_a = rev_a.reshape(npair, 2 * N_GROUPS_A, 4 * BLOCK)
    fwd_b = jnp.where(np.arange(S + tq) >= tq, _lookup(table_b, _fwd_b_index(S, tq)), NEG_INF)[:, None, :]

    x2 = x.reshape(T, D)
    l = 0
    lam_init = 0.8 - 0.6 * math.exp(-0.3 * l)
    proj_a, proj_r = _in_proj(x2, norm_attn_g[l][None, :], w_in[l])
    ya, wa, wb, wo = _mixer_a(proj_a.reshape(B, S, PROJ_A), rev_a, [w_proj_a[l], w_proj_b[l], w_out[l]])

    lam_vecs = jnp.stack([diff_lambda_q1[l], diff_lambda_k1[l],
                          diff_lambda_q2[l], diff_lambda_k2[l]]).astype(F32)
    gmat = jnp.broadcast_to(diff_subln_g[l].astype(F32)[:, None], (2 * HEAD_DIM, LANES))
    yb, wg, wu, wd = _mixer_b(proj_r.reshape(B, S, PROJ_R), fwd_b, lam_vecs, gmat,
                              lam_init, [w_ffn_gate[l], w_ffn_up[l], w_ffn_down[l]], tq=tq)

    x1, h = _attn_out(ya.reshape(T, OUT_WIDTH_A), yb.reshape(T, WIDTH_B), proj_r, x2, wa, wb, wo,
                      norm_ffn_g[l][None, :])
    out = _ffn(x1, h, wg, wu, wd, norm_final_g[None, :])
    return out.reshape(B, S, D)
```

```python
import functools
import math

import numpy as np
import jax
import jax.numpy as jnp
from jax import lax
from jax.experimental import pallas as pl
from jax.experimental.pallas import tpu as pltpu

D_MODEL = 2048
HEAD_DIM = 64
DIL_PATTERNS = ((128, 1), (512, 4), (2048, 16))
N_GROUPS_A = len(DIL_PATTERNS)
HEADS_PER_GROUP_A = 8
N_HEADS_A = N_GROUPS_A * HEADS_PER_GROUP_A
WIDTH_A = N_HEADS_A * HEAD_DIM
OUT_WIDTH_A = HEADS_PER_GROUP_A * HEAD_DIM
BLOCK = 128
N_HEADS_B = D_MODEL // (2 * HEAD_DIM)
WIDTH_B = N_HEADS_B * 2 * HEAD_DIM
NUM_BUCKETS = 32
MAX_DISTANCE = 2048
PROJ_A = 3 * WIDTH_A
PROJ_R = 3 * WIDTH_B + 2 * D_MODEL
D_IN = PROJ_A + PROJ_R
NORM_EPS = 1e-6
NEG_INF = -1e30
SCALE = HEAD_DIM ** -0.5
LOG2E = math.log2(math.e)

OFF_QB = 0
OFF_KB = WIDTH_B
OFF_VB = 2 * WIDTH_B
OFF_GA = 3 * WIDTH_B
OFF_GB = OFF_GA + D_MODEL

LANES = 128
VMEM_LIMIT = 56 * 1024 * 1024
VMEM_LIMIT_ATTN_OUT = 58 * 1024 * 1024
VMEM_LIMIT_FFN = 60 * 1024 * 1024

BF16 = jnp.bfloat16
F32 = jnp.float32


def _rel_bucket_np(dist):
    n = np.maximum(dist, 0)
    max_exact = NUM_BUCKETS // 2
    nf = np.maximum(n, 1).astype(np.float32)
    large = max_exact + (np.log(nf / np.float32(max_exact)) / np.float32(math.log(MAX_DISTANCE / max_exact))
                         * np.float32(NUM_BUCKETS - max_exact)).astype(np.int32)
    large = np.minimum(large, NUM_BUCKETS - 1)
    return np.where(n < max_exact, n, large).astype(np.int32)


def _rms(x, g):
    ms = jnp.mean(x * x, axis=-1, keepdims=True)
    return x * lax.rsqrt(ms + NORM_EPS) * g


def _params(sem, vmem=VMEM_LIMIT):
    return pltpu.CompilerParams(dimension_semantics=sem, vmem_limit_bytes=vmem)


BF16_ROWS = 16


def _cast_specs(weights, grid):
    nsteps = math.prod(grid)
    in_specs, out_specs, out_shapes = [], [], []
    for w in weights:
        rows, cols = w.shape
        blk = next(r for r in range(BF16_ROWS, rows + 1, BF16_ROWS)
                   if rows % r == 0 and nsteps % (rows // r) == 0 and rows // r <= nsteps)
        per = nsteps // (rows // blk)

        def index(*ids, per=per):
            step = 0
            for i, n in zip(ids, grid):
                step = step * n + i
            return (step // per, 0)

        in_specs.append(pl.BlockSpec((blk, cols), index))
        out_specs.append(pl.BlockSpec((blk, cols), index))
        out_shapes.append(jax.ShapeDtypeStruct((rows, cols), BF16))
    return in_specs, out_specs, out_shapes


def _interleave(lists):
    keyed = [((i + 0.5) / len(items), n, i, item) for n, items in enumerate(lists) for i, item in enumerate(items)]
    return [item for _, _, _, item in sorted(keyed, key=lambda k: k[:3])]


def _cast_blocks(in_refs, out_refs):
    for i_ref, o_ref in zip(in_refs, out_refs):
        o_ref[...] = i_ref[...].astype(BF16)


def _in_proj_kernel(x_hbm, g_ref, w_ref, oa_ref, or_ref, h_ref, x_buf, x_sem, *, na, nqb, tm):
    i = pl.program_id(0)
    j = pl.program_id(1)

    def x_copy(tile):
        return pltpu.make_async_copy(x_hbm.at[pl.ds(tile * tm, tm), :], x_buf, x_sem)

    @pl.when(j == 0)
    def _():
        @pl.when(i == 0)
        def _():
            x_copy(0).start()

        x_copy(i).wait()
        h_ref[...] = _rms(x_buf[...], g_ref[...]).astype(BF16)

        @pl.when(i + 1 < pl.num_programs(0))
        def _():
            x_copy(i + 1).start()

    @pl.when(j < na)
    def _():
        oa_ref[...] = jnp.dot(h_ref[...], w_ref[...].astype(BF16), preferred_element_type=F32)

    @pl.when(j >= na)
    def _():
        scale = jnp.where(j < na + nqb, jnp.float32(SCALE * LOG2E), jnp.float32(1.0))
        res = jnp.dot(h_ref[...], w_ref[...].astype(BF16), preferred_element_type=F32)
        or_ref[...] = (res * scale).astype(BF16)


def _in_proj(x2, g, w, tm=2048, tn=512):
    T, D = x2.shape
    na = PROJ_A // tn
    assert OFF_QB == 0 and WIDTH_B % tn == 0
    return pl.pallas_call(
        functools.partial(_in_proj_kernel, na=na, nqb=WIDTH_B // tn, tm=tm),
        grid=(T // tm, D_IN // tn),
        in_specs=[pl.BlockSpec(memory_space=pl.ANY),
                  pl.BlockSpec((1, D), lambda i, j: (0, 0)),
                  pl.BlockSpec((D, tn), lambda i, j: (0, j))],
        out_specs=[pl.BlockSpec((tm, tn), lambda i, j: (i, jnp.minimum(j, na - 1))),
                   pl.BlockSpec((tm, tn), lambda i, j: (i, jnp.maximum(j - na, 0)))],
        out_shape=[jax.ShapeDtypeStruct((T, PROJ_A), F32), jax.ShapeDtypeStruct((T, PROJ_R), BF16)],
        scratch_shapes=[pltpu.VMEM((tm, D), BF16), pltpu.VMEM((tm, D), F32), pltpu.SemaphoreType.DMA(())],
        compiler_params=_params(("arbitrary", "arbitrary")),
        name="in_proj",
    )(x2, g, w)


def _mixer_a_kernel(*refs, seq, n_cast):
    qkv = (refs[0:3], refs[3:6], refs[6:9])
    rev_ref = refs[9]
    o_ref = refs[10 + n_cast]
    (bias_ref, q_st, kt_st, vx_st, s_ref, p_ref, mrow_ref, m_ref, l_ref, acc_ref) = refs[11 + 2 * n_cast:]
    _cast_blocks(refs[10:10 + n_cast], refs[11 + n_cast:11 + 2 * n_cast])
    lane = lax.broadcasted_iota(jnp.int32, (BLOCK, LANES), 1)
    lo = lane < HEAD_DIM

    row = lax.broadcasted_iota(jnp.int32, (2 * BLOCK, 2 * BLOCK), 0)
    col = lax.broadcasted_iota(jnp.int32, (2 * BLOCK, 2 * BLOCK), 1)
    rel = BLOCK + (row & (BLOCK - 1)) - col
    band = (rel >= 0) & (rel <= BLOCK)
    band_first = band & (col >= BLOCK)
    for g in range(N_GROUPS_A):
        halves = []
        for hh in range(2):
            x = jnp.broadcast_to(rev_ref[0, g * 2 + hh:g * 2 + hh + 1, :], (BLOCK, 4 * BLOCK))
            halves.append(pltpu.roll(x, 0, 1, stride=1, stride_axis=0)[:, BLOCK:3 * BLOCK])
        toep = jnp.concatenate(halves, axis=0)
        bias_ref[2 * g] = jnp.where(band, toep, NEG_INF)
        bias_ref[2 * g + 1] = jnp.where(band_first, toep, NEG_INF)

    eye = jnp.where(lax.broadcasted_iota(jnp.int32, (LANES, LANES), 0) == lane, 1.0, 0.0).astype(BF16)
    for g in range(N_GROUPS_A):
        kt_st[g, :, 0:BLOCK] = jnp.zeros((LANES, BLOCK), BF16)
        vx_st[g, 0:BLOCK, :] = jnp.zeros((BLOCK, 2 * LANES), BF16)
        vx_st[g, :, LANES:] = jnp.ones((BLOCK + seq, LANES), BF16)

    def scores(g, slot, base, first):
        q = q_st[g, base:base + BLOCK, :]
        zero = jnp.zeros_like(q)
        qz = jnp.concatenate([jnp.where(lo, q, zero), jnp.where(lo, zero, q)], axis=0)
        s = jnp.dot(qz, kt_st[g, :, base - BLOCK:base + BLOCK], preferred_element_type=F32)
        s_ref[slot] = s + bias_ref[2 * g + (1 if first else 0)]

    def softmax_group(slot, rg):
        rows = slice(rg * BF16_ROWS, (rg + 1) * BF16_ROWS)
        m = jnp.max(s_ref[slot, rows, :], axis=-1, keepdims=True)
        p_ref[slot, rows, :] = jnp.exp(s_ref[slot, rows, :] - m).astype(BF16)
        mrow_ref[slot, rows, :] = jnp.broadcast_to(m, (BF16_ROWS, LANES))

    def finish(g, slot, base, out_rows):
        acc = jnp.dot(p_ref[slot], vx_st[g, base - BLOCK:base + BLOCK, :], preferred_element_type=F32)
        mrow = mrow_ref[slot]
        m_ref[g, out_rows, :] = jnp.where(lo, mrow[:BLOCK], mrow[BLOCK:])
        l_ref[g, out_rows, :] = jnp.where(lo, acc[:BLOCK, LANES:], acc[BLOCK:, LANES:])
        acc_ref[g, out_rows, :] = jnp.where(lo, acc[:BLOCK, :LANES], acc[BLOCK:, :LANES])

    def merge(rows):
        ms = [m_ref[g, rows, :] for g in range(N_GROUPS_A)]
        mx = jnp.maximum(jnp.maximum(ms[0], ms[1]), ms[2])
        num = jnp.zeros((BLOCK, LANES), F32)
        den = jnp.zeros((BLOCK, LANES), F32)
        for g in range(N_GROUPS_A):
            w = jnp.exp(ms[g] - mx)
            num = num + w * acc_ref[g, rows, :]
            den = den + w * l_ref[g, rows, :]
        o_ref[0, rows, :] = (num / den).astype(o_ref.dtype)

    n_rg = 2 * BLOCK // BF16_ROWS
    blocks = []
    for g in sorted(range(N_GROUPS_A), key=lambda g: -DIL_PATTERNS[g][1]):
        d = DIL_PATTERNS[g][1]
        sub_len = seq // d
        q_ref, k_ref, v_ref = qkv[g]
        for r in range(d):
            src = pl.ds(r, sub_len, stride=d) if d > 1 else pl.ds(0, seq)
            dst = slice(BLOCK + r * sub_len, BLOCK + (r + 1) * sub_len)
            q_st[g, dst, :] = (q_ref[0, src, :] * SCALE).astype(BF16)
            kt_st[g, :, dst] = lax.dot_general(eye, k_ref[0, src, :].astype(BF16), (((1,), (1,)), ((), ())),
                                               preferred_element_type=F32).astype(BF16)
            vx_st[g, dst, :LANES] = v_ref[0, src, :].astype(BF16)
            for n in range(sub_len // BLOCK):
                base = BLOCK + r * sub_len + n * BLOCK
                out_rows = pl.ds(n * BLOCK * d + r, BLOCK, stride=d) if d > 1 else pl.ds(n * BLOCK, BLOCK)
                blocks.append((g, base, n == 0, out_rows))

    nslot = s_ref.shape[0]
    for t in range(len(blocks) + 2):
        if t < len(blocks):
            g, base, first, _ = blocks[t]
            scores(g, t % nslot, base, first)
        for rg in range(n_rg):
            if 1 <= t <= len(blocks):
                softmax_group((t - 1) % nslot, rg)
            if rg == n_rg // 2 and 2 <= t:
                g, base, _, out_rows = blocks[t - 2]
                finish(g, (t - 2) % nslot, base, out_rows)
                if DIL_PATTERNS[g][1] == 1:
                    merge(out_rows)


def _mixer_a(proj_a, rev_a, weights):
    B, S, _ = proj_a.shape
    npair = OUT_WIDTH_A // LANES
    grid = (B, npair)

    def col(which, g):
        base = (which * WIDTH_A + g * OUT_WIDTH_A) // LANES
        return pl.BlockSpec((1, S, LANES), lambda b, hp: (b, 0, base + hp))

    cast_in, cast_out, cast_shapes = _cast_specs(weights, grid)
    in_specs = [col(which, g) for g in range(N_GROUPS_A) for which in range(3)]
    in_specs.append(pl.BlockSpec((1, 2 * N_GROUPS_A, 4 * BLOCK), lambda b, hp: (hp, 0, 0)))
    return pl.pallas_call(
        functools.partial(_mixer_a_kernel, seq=S, n_cast=len(weights)),
        grid=grid,
        in_specs=in_specs + cast_in,
        out_specs=[pl.BlockSpec((1, S, LANES), lambda b, hp: (b, 0, hp))] + cast_out,
        out_shape=[jax.ShapeDtypeStruct((B, S, OUT_WIDTH_A), BF16)] + cast_shapes,
        scratch_shapes=[pltpu.VMEM((2 * N_GROUPS_A, 2 * BLOCK, 2 * BLOCK), F32),
                        pltpu.VMEM((N_GROUPS_A, BLOCK + S, LANES), BF16),
                        pltpu.VMEM((N_GROUPS_A, LANES, BLOCK + S), BF16),
                        pltpu.VMEM((N_GROUPS_A, BLOCK + S, 2 * LANES), BF16),
                        pltpu.VMEM((4, 2 * BLOCK, 2 * BLOCK), F32),
                        pltpu.VMEM((4, 2 * BLOCK, 2 * BLOCK), BF16),
                        pltpu.VMEM((4, 2 * BLOCK, LANES), F32),
                        pltpu.VMEM((N_GROUPS_A, S, LANES), F32),
                        pltpu.VMEM((N_GROUPS_A, S, LANES), F32),
                        pltpu.VMEM((N_GROUPS_A, S, LANES), F32)],
        compiler_params=_params(("arbitrary", "arbitrary")),
        name="mixer_a",
    )(*([proj_a] * 9), rev_a, *weights)


def _mixer_b_kernel(*refs, tq, seq, lam_init, n_cast):
    nq = seq // tq
    q_ref, k_ref, v_ref, rev_ref, lam_ref, g_ref = refs[:6]
    o_ref = refs[6 + n_cast]
    toep_ref, vx_ref = refs[7 + 2 * n_cast:9 + 2 * n_cast]
    s_refs = refs[9 + 2 * n_cast:9 + 2 * n_cast + nq]
    p_refs = refs[9 + 2 * n_cast + nq:]
    _cast_blocks(refs[6:6 + n_cast], refs[7 + n_cast:7 + 2 * n_cast])
    rg_rows = BF16_ROWS
    n_rg = 2 * tq // rg_rows

    x = jnp.broadcast_to(rev_ref[0] * LOG2E, (tq, seq + tq))
    rolled = pltpu.roll(x, 0, 1, stride=1, stride_axis=0)
    toep_ref[:, :seq] = rolled[:, :seq]
    row = lax.broadcasted_iota(jnp.int32, (tq, tq), 0)
    col = lax.broadcasted_iota(jnp.int32, (tq, tq), 1)
    toep_ref[:, seq:] = jnp.where(col <= row, rolled[:, seq:], NEG_INF)

    vx_ref[:, :LANES] = v_ref[0]
    vx_ref[:, LANES:] = jnp.ones((seq, LANES), BF16)

    lane = lax.broadcasted_iota(jnp.int32, (tq, LANES), 1)
    lo = lane < HEAD_DIM
    lv = lam_ref[...]
    lam = (jnp.exp(jnp.sum(lv[0:1] * lv[1:2], axis=-1, keepdims=True))
           - jnp.exp(jnp.sum(lv[2:3] * lv[3:4], axis=-1, keepdims=True)) + lam_init)

    def score_chunks(qi):
        q = q_ref[0, qi * tq:(qi + 1) * tq, :]
        zero = jnp.zeros_like(q)
        qz = jnp.concatenate([jnp.where(lo, q, zero), jnp.where(lo, zero, q)], axis=0)

        def chunk(c):
            kc = k_ref[0, c * tq:(c + 1) * tq, :]
            s_refs[qi][:, c * tq:(c + 1) * tq] = lax.dot_general(
                qz, kc, (((1,), (1,)), ((), ())), preferred_element_type=F32)

        return [functools.partial(chunk, c) for c in range(qi + 1)]

    def softmax_group(qi, rg):
        rows = slice(rg * rg_rows, (rg + 1) * rg_rows)
        brow = (rg * rg_rows) % tq
        bias = toep_ref[brow:brow + rg_rows, (nq - qi) * tq:(nq + 1) * tq]
        t = s_refs[qi][rows, :] + bias
        m = jnp.max(t, axis=-1, keepdims=True)
        p_refs[qi][rows, :] = jnp.exp2(t - m).astype(BF16)

    halves = {}

    def value_matmul(qi, half):
        acc = jnp.dot(p_refs[qi][half * tq:(half + 1) * tq, :], vx_ref[:(qi + 1) * tq, :],
                      preferred_element_type=F32)
        halves[qi, half] = acc[:, :LANES] / acc[:, LANES:]

    def finish(qi):
        y = halves[qi, 0] - lam * halves[qi, 1]
        y = _rms(y, g_ref[...]) * (1.0 - lam_init)
        o_ref[0, qi * tq:(qi + 1) * tq, :] = y.astype(o_ref.dtype)

    order = list(range(nq - 1, -1, -1))
    for t in range(nq + 2):
        scores_t = score_chunks(order[t]) if t < nq else []
        softmax_t, tail_t = [], []
        if 1 <= t <= nq:
            qi = order[t - 1]
            groups = [functools.partial(softmax_group, qi, rg) for rg in range(n_rg)]
            softmax_t = groups[:n_rg // 2] + [functools.partial(value_matmul, qi, 0)] + groups[n_rg // 2:]
        if t >= 2:
            qi = order[t - 2]
            tail_t = [functools.partial(value_matmul, qi, 1), functools.partial(finish, qi)]
        for emit in _interleave([scores_t, softmax_t, tail_t]):
            emit()


def _mixer_b(proj, rev_b, lam_vecs, subln_g, lam_init, weights, tq=256):
    B, S, _ = proj.shape
    H = N_HEADS_B
    nq = S // tq
    grid = (B, H)
    kern = functools.partial(_mixer_b_kernel, tq=tq, seq=S, lam_init=lam_init, n_cast=len(weights))

    def col(off):
        return pl.BlockSpec((1, S, LANES), lambda b, h: (b, 0, off // LANES + h))

    cast_in, cast_out, cast_shapes = _cast_specs(weights, grid)
    return pl.pallas_call(
        kern,
        grid=grid,
        in_specs=[col(OFF_QB), col(OFF_KB), col(OFF_VB),
                  pl.BlockSpec((1, 1, S + tq), lambda b, h: (h, 0, 0)),
                  pl.BlockSpec((4, HEAD_DIM), lambda b, h: (0, 0)),
                  pl.BlockSpec((1, 2 * HEAD_DIM), lambda b, h: (0, 0))] + cast_in,
        out_specs=[pl.BlockSpec((1, S, LANES), lambda b, h: (b, 0, h))] + cast_out,
        out_shape=[jax.ShapeDtypeStruct((B, S, WIDTH_B), BF16)] + cast_shapes,
        scratch_shapes=([pltpu.VMEM((tq, S + tq), F32), pltpu.VMEM((S, 2 * LANES), BF16)]
                        + [pltpu.VMEM((2 * tq, (i + 1) * tq), F32) for i in range(nq)]
                        + [pltpu.VMEM((2 * tq, (i + 1) * tq), BF16) for i in range(nq)]),
        compiler_params=_params(("arbitrary", "arbitrary")),
        name="mixer_b",
    )(proj, proj, proj, rev_b, lam_vecs, subln_g, *weights)


def _attn_out_kernel(ya_ref, yb_ref, ga_ref, gb_ref, x_ref, wa_ref, wb_ref, wo_ref, g_ref, x1_ref, h_ref):
    pa = jnp.dot(ya_ref[...], wa_ref[...], preferred_element_type=F32)
    pb = jnp.dot(yb_ref[...], wb_ref[...], preferred_element_type=F32)
    ga = jax.nn.sigmoid(ga_ref[...].astype(F32))
    gb = jax.nn.sigmoid(gb_ref[...].astype(F32))
    merged = (ga * pa + gb * pb).astype(BF16)
    x1_ref[...] = x_ref[...] + jnp.dot(merged, wo_ref[...], preferred_element_type=F32)
    h_ref[...] = _rms(x1_ref[...], g_ref[...]).astype(BF16)


def _attn_out(ya, yb, proj_r, x2, wa, wb, wo, g, tm=512):
    T, D = x2.shape

    def resident(shape):
        return pl.BlockSpec(shape, lambda i: (0, 0), pipeline_mode=pl.Buffered(1))

    return pl.pallas_call(
        _attn_out_kernel,
        grid=(T // tm,),
        in_specs=[pl.BlockSpec((tm, OUT_WIDTH_A), lambda i: (i, 0)),
                  pl.BlockSpec((tm, WIDTH_B), lambda i: (i, 0)),
                  pl.BlockSpec((tm, D), lambda i: (i, OFF_GA // D)),
                  pl.BlockSpec((tm, D), lambda i: (i, OFF_GB // D)),
                  pl.BlockSpec((tm, D), lambda i: (i, 0)),
                  resident((OUT_WIDTH_A, D)), resident((WIDTH_B, D)), resident((D, D)), resident((1, D))],
        out_specs=[pl.BlockSpec((tm, D), lambda i: (i, 0)), pl.BlockSpec((tm, D), lambda i: (i, 0))],
        out_shape=[jax.ShapeDtypeStruct((T, D), F32), jax.ShapeDtypeStruct((T, D), BF16)],
        compiler_params=_params(("parallel",), vmem=VMEM_LIMIT_ATTN_OUT),
        name="attn_out",
    )(ya, yb, proj_r, proj_r, x2, wa, wb, wo, g)


def _ffn_kernel(x1_hbm, h_ref, wg_ref, wu_ref, wd_ref, gf_ref, o_ref, x1_buf, x1_sem, *, tm):
    i = pl.program_id(0)
    f = pl.program_id(1)
    x1_copy = pltpu.make_async_copy(x1_hbm.at[pl.ds(i * tm, tm), :], x1_buf, x1_sem)

    @pl.when(f == 0)
    def _():
        x1_copy.start()
        o_ref[...] = jnp.zeros(o_ref.shape, F32)

    h = h_ref[...]
    a = jnp.dot(h, wg_ref[...], preferred_element_type=F32)
    b = jnp.dot(h, wu_ref[...], preferred_element_type=F32)
    u = (a * jax.nn.sigmoid(a)) * b
    o_ref[...] += jnp.dot(u.astype(BF16), wd_ref[...], preferred_element_type=F32)

    @pl.when(f == pl.num_programs(1) - 1)
    def _():
        x1_copy.wait()
        o_ref[...] = _rms(x1_buf[...] + o_ref[...], gf_ref[...])


def _ffn(x1, h, wg, wu, wd, gf, tm=1024, tf=512):
    T, D = x1.shape
    F = wg.shape[1]
    return pl.pallas_call(
        functools.partial(_ffn_kernel, tm=tm),
        grid=(T // tm, F // tf),
        in_specs=[pl.BlockSpec(memory_space=pl.ANY),
                  pl.BlockSpec((tm, D), lambda i, f: (i, 0)),
                  pl.BlockSpec((D, tf), lambda i, f: (0, f)),
                  pl.BlockSpec((D, tf), lambda i, f: (0, f)),
                  pl.BlockSpec((tf, D), lambda i, f: (f, 0)),
                  pl.BlockSpec((1, D), lambda i, f: (0, 0))],
        out_specs=pl.BlockSpec((tm, D), lambda i, f: (i, 0)),
        out_shape=jax.ShapeDtypeStruct((T, D), F32),
        scratch_shapes=[pltpu.VMEM((tm, D), F32), pltpu.SemaphoreType.DMA(())],
        compiler_params=_params(("parallel", "arbitrary"), vmem=VMEM_LIMIT_FFN),
        name="ffn",
    )(x1, h, wg, wu, wd, gf)


def _rev_a_index():
    u = np.arange(4 * BLOCK)
    rel = np.clip(2 * BLOCK - u, 0, None)
    return np.stack([_rel_bucket_np(rel * d) for _, d in DIL_PATTERNS])


def _rev_b_index(seq, tq):
    c = np.arange(seq + tq)
    return _rel_bucket_np(np.clip(seq - c, 0, seq - 1))


def _lookup(table, idx):
    onehot = jnp.asarray(np.asarray(idx)[None, :] == np.arange(NUM_BUCKETS)[:, None])
    return jnp.sum(jnp.where(onehot[:, None, :], table[:, :, None], 0.0), axis=0)


def kernel(x, norm_attn_g, w_in, w_proj_a, w_proj_b, w_out, rel_bias_table, diff_lambda_q1, diff_lambda_k1, diff_lambda_q2, diff_lambda_k2, diff_subln_g, norm_ffn_g, w_ffn_gate, w_ffn_up, w_ffn_down, norm_final_g):
    B, S, D = x.shape
    T = B * S
    depth = w_in.shape[0]
    assert depth == 1, "the final RMSNorm is fused into the FFN epilogue of a single layer"
    table_a = rel_bias_table[:, :N_HEADS_A].astype(F32)
    table_b = rel_bias_table[:, N_HEADS_A:].astype(F32)
    tq = 256

    idx_a = _rev_a_index()
    rev_a = jnp.stack([_lookup(table_a[:, g * HEADS_PER_GROUP_A:(g + 1) * HEADS_PER_GROUP_A], idx_a[g])
                       for g in range(N_GROUPS_A)])
    npair = OUT_WIDTH_A // LANES
    rev_a = jnp.transpose(rev_a.reshape(N_GROUPS_A, npair, 2, 4 * BLOCK), (1, 0, 2, 3))
    rev_a = rev_a.reshape(npair, 2 * N_GROUPS_A, 4 * BLOCK)
    rev_b = _lookup(table_b, _rev_b_index(S, tq))[:, None, :]

    x2 = x.reshape(T, D)
    l = 0
    lam_init = 0.8 - 0.6 * math.exp(-0.3 * l)
    proj_a, proj_r = _in_proj(x2, norm_attn_g[l][None, :], w_in[l])
    ya, wa, wb, wo = _mixer_a(proj_a.reshape(B, S, PROJ_A), rev_a, [w_proj_a[l], w_proj_b[l], w_out[l]])

    lam_vecs = jnp.stack([diff_lambda_q1[l], diff_lambda_k1[l],
                          diff_lambda_q2[l], diff_lambda_k2[l]]).astype(F32)
    yb, wg, wu, wd = _mixer_b(proj_r.reshape(B, S, PROJ_R), rev_b, lam_vecs, diff_subln_g[l][None, :].astype(F32),
                              lam_init, [w_ffn_gate[l], w_ffn_up[l], w_ffn_down[l]], tq=tq)

    x1, h = _attn_out(ya.reshape(T, OUT_WIDTH_A), yb.reshape(T, WIDTH_B), proj_r, x2, wa, wb, wo,
                      norm_ffn_g[l][None, :])
    out = _ffn(x1, h, wg, wu, wd, norm_final_g[None, :])
    return out.reshape(B, S, D)
```

```python
import functools
import math

import numpy as np
import jax
import jax.numpy as jnp
from jax import lax
from jax.experimental import pallas as pl
from jax.experimental.pallas import tpu as pltpu

D_MODEL = 2048
HEAD_DIM = 64
DIL_PATTERNS = ((128, 1), (512, 4), (2048, 16))
N_GROUPS_A = len(DIL_PATTERNS)
HEADS_PER_GROUP_A = 8
N_HEADS_A = N_GROUPS_A * HEADS_PER_GROUP_A
WIDTH_A = N_HEADS_A * HEAD_DIM
OUT_WIDTH_A = HEADS_PER_GROUP_A * HEAD_DIM
BLOCK = 128
N_HEADS_B = D_MODEL // (2 * HEAD_DIM)
WIDTH_B = N_HEADS_B * 2 * HEAD_DIM
NUM_BUCKETS = 32
MAX_DISTANCE = 2048
D_FF = -(-8 * D_MODEL // (3 * 256)) * 256
PROJ_A = 3 * WIDTH_A
PROJ_R = 3 * WIDTH_B + 2 * D_MODEL
D_IN = PROJ_A + PROJ_R
NORM_EPS = 1e-6
NEG_INF = -1e30
SCALE = HEAD_DIM ** -0.5
LOG2E = math.log2(math.e)

OFF_QB = 0
OFF_KB = WIDTH_B
OFF_VB = 2 * WIDTH_B
OFF_GA = 3 * WIDTH_B
OFF_GB = OFF_GA + D_MODEL

LANES = 128
VMEM_LIMIT = 56 * 1024 * 1024
VMEM_LIMIT_ATTN_OUT = 58 * 1024 * 1024
VMEM_LIMIT_FFN = 60 * 1024 * 1024

BF16 = jnp.bfloat16
F32 = jnp.float32


def _rel_bucket_np(dist):
    n = np.maximum(dist, 0)
    max_exact = NUM_BUCKETS // 2
    nf = np.maximum(n, 1).astype(np.float32)
    large = max_exact + (np.log(nf / np.float32(max_exact)) / np.float32(math.log(MAX_DISTANCE / max_exact))
                         * np.float32(NUM_BUCKETS - max_exact)).astype(np.int32)
    large = np.minimum(large, NUM_BUCKETS - 1)
    return np.where(n < max_exact, n, large).astype(np.int32)


def _rms(x, g):
    ms = jnp.mean(x * x, axis=-1, keepdims=True)
    return x * lax.rsqrt(ms + NORM_EPS) * g


def _params(sem, vmem=VMEM_LIMIT):
    return pltpu.CompilerParams(dimension_semantics=sem, vmem_limit_bytes=vmem)


BF16_ROWS = 16


def _cast_specs(weights, grid):
    nsteps = math.prod(grid)
    in_specs, out_specs, out_shapes = [], [], []
    for w in weights:
        rows, cols = w.shape
        blk = next(r for r in range(BF16_ROWS, rows + 1, BF16_ROWS)
                   if rows % r == 0 and nsteps % (rows // r) == 0 and rows // r <= nsteps)
        per = nsteps // (rows // blk)

        def index(*ids, per=per):
            step = 0
            for i, n in zip(ids, grid):
                step = step * n + i
            return (step // per, 0)

        in_specs.append(pl.BlockSpec((blk, cols), index))
        out_specs.append(pl.BlockSpec((blk, cols), index))
        out_shapes.append(jax.ShapeDtypeStruct((rows, cols), BF16))
    return in_specs, out_specs, out_shapes


def _interleave(lists):
    keyed = [((i + 0.5) / len(items), n, i, item) for n, items in enumerate(lists) for i, item in enumerate(items)]
    return [item for _, _, _, item in sorted(keyed, key=lambda k: k[:3])]


def _cast_blocks(in_refs, out_refs):
    for i_ref, o_ref in zip(in_refs, out_refs):
        o_ref[...] = i_ref[...].astype(BF16)


def _in_proj_kernel(x_hbm, g_ref, w_ref, oa_ref, or_ref, h_ref, x_buf, x_sem, *, na, nqb, tm):
    i = pl.program_id(0)
    j = pl.program_id(1)

    def x_copy(tile):
        return pltpu.make_async_copy(x_hbm.at[pl.ds(tile * tm, tm), :], x_buf, x_sem)

    @pl.when(j == 0)
    def _():
        @pl.when(i == 0)
        def _():
            x_copy(0).start()

        x_copy(i).wait()
        h_ref[...] = _rms(x_buf[...], g_ref[...]).astype(BF16)

        @pl.when(i + 1 < pl.num_programs(0))
        def _():
            x_copy(i + 1).start()

    @pl.when(j < na)
    def _():
        oa_ref[...] = jnp.dot(h_ref[...], w_ref[...].astype(BF16), preferred_element_type=F32)

    @pl.when(j >= na)
    def _():
        scale = jnp.where(j < na + nqb, jnp.float32(SCALE * LOG2E), jnp.float32(1.0))
        res = jnp.dot(h_ref[...], w_ref[...].astype(BF16), preferred_element_type=F32)
        or_ref[...] = (res * scale).astype(BF16)


def _in_proj(x2, g, w, tm=2048, tn=512):
    T, D = x2.shape
    na = PROJ_A // tn
    assert OFF_QB == 0 and WIDTH_B % tn == 0
    return pl.pallas_call(
        functools.partial(_in_proj_kernel, na=na, nqb=WIDTH_B // tn, tm=tm),
        grid=(T // tm, D_IN // tn),
        in_specs=[pl.BlockSpec(memory_space=pl.ANY),
                  pl.BlockSpec((1, D), lambda i, j: (0, 0)),
                  pl.BlockSpec((D, tn), lambda i, j: (0, j))],
        out_specs=[pl.BlockSpec((tm, tn), lambda i, j: (i, jnp.minimum(j, na - 1))),
                   pl.BlockSpec((tm, tn), lambda i, j: (i, jnp.maximum(j - na, 0)))],
        out_shape=[jax.ShapeDtypeStruct((T, PROJ_A), F32), jax.ShapeDtypeStruct((T, PROJ_R), BF16)],
        scratch_shapes=[pltpu.VMEM((tm, D), BF16), pltpu.VMEM((tm, D), F32), pltpu.SemaphoreType.DMA(())],
        compiler_params=_params(("arbitrary", "arbitrary")),
        name="in_proj",
    )(x2, g, w)


def _mixer_a_kernel(*refs, seq, n_cast):
    qkv = (refs[0:3], refs[3:6], refs[6:9])
    rev_ref = refs[9]
    o_ref = refs[10 + n_cast]
    (bias_ref, q_st, k_st, vx_st, s_ref, p_ref, mrow_ref, m_ref, l_ref, acc_ref) = refs[11 + 2 * n_cast:]
    _cast_blocks(refs[10:10 + n_cast], refs[11 + n_cast:11 + 2 * n_cast])
    lane = lax.broadcasted_iota(jnp.int32, (BLOCK, LANES), 1)
    lo = lane < HEAD_DIM

    row = lax.broadcasted_iota(jnp.int32, (2 * BLOCK, 2 * BLOCK), 0)
    col = lax.broadcasted_iota(jnp.int32, (2 * BLOCK, 2 * BLOCK), 1)
    rel = BLOCK + (row & (BLOCK - 1)) - col
    band = (rel >= 0) & (rel <= BLOCK)
    band_first = band & (col >= BLOCK)
    for g in range(N_GROUPS_A):
        halves = []
        for hh in range(2):
            x = jnp.broadcast_to(rev_ref[0, g * 2 + hh:g * 2 + hh + 1, :], (BLOCK, 4 * BLOCK))
            halves.append(pltpu.roll(x, 0, 1, stride=1, stride_axis=0)[:, BLOCK:3 * BLOCK])
        toep = jnp.concatenate(halves, axis=0)
        bias_ref[2 * g] = jnp.where(band, toep, NEG_INF)
        bias_ref[2 * g + 1] = jnp.where(band_first, toep, NEG_INF)

    for g in range(N_GROUPS_A):
        k_st[g, 0:BLOCK, :] = jnp.zeros((BLOCK, LANES), BF16)
        vx_st[g, 0:BLOCK, :] = jnp.zeros((BLOCK, 2 * LANES), BF16)
        vx_st[g, :, LANES:] = jnp.ones((BLOCK + seq, LANES), BF16)

    def scores(g, slot, base, first):
        q = q_st[g, base:base + BLOCK, :]
        zero = jnp.zeros_like(q)
        qz = jnp.concatenate([jnp.where(lo, q, zero), jnp.where(lo, zero, q)], axis=0)
        kw = k_st[g, base - BLOCK:base + BLOCK, :]
        s = lax.dot_general(qz, kw, (((1,), (1,)), ((), ())), preferred_element_type=F32)
        s_ref[slot] = s + bias_ref[2 * g + (1 if first else 0)]

    def softmax_group(slot, rg):
        rows = slice(rg * BF16_ROWS, (rg + 1) * BF16_ROWS)
        m = jnp.max(s_ref[slot, rows, :], axis=-1, keepdims=True)
        p_ref[slot, rows, :] = jnp.exp(s_ref[slot, rows, :] - m).astype(BF16)
        mrow_ref[slot, rows, :] = jnp.broadcast_to(m, (BF16_ROWS, LANES))

    def finish(g, slot, base, out_rows):
        acc = jnp.dot(p_ref[slot], vx_st[g, base - BLOCK:base + BLOCK, :], preferred_element_type=F32)
        mrow = mrow_ref[slot]
        m_ref[g, out_rows, :] = jnp.where(lo, mrow[:BLOCK], mrow[BLOCK:])
        l_ref[g, out_rows, :] = jnp.where(lo, acc[:BLOCK, LANES:], acc[BLOCK:, LANES:])
        acc_ref[g, out_rows, :] = jnp.where(lo, acc[:BLOCK, :LANES], acc[BLOCK:, :LANES])

    def merge(rows):
        ms = [m_ref[g, rows, :] for g in range(N_GROUPS_A)]
        mx = jnp.maximum(jnp.maximum(ms[0], ms[1]), ms[2])
        num = jnp.zeros((BLOCK, LANES), F32)
        den = jnp.zeros((BLOCK, LANES), F32)
        for g in range(N_GROUPS_A):
            w = jnp.exp(ms[g] - mx)
            num = num + w * acc_ref[g, rows, :]
            den = den + w * l_ref[g, rows, :]
        o_ref[0, rows, :] = (num / den).astype(o_ref.dtype)

    n_rg = 2 * BLOCK // BF16_ROWS
    blocks = []
    for g in sorted(range(N_GROUPS_A), key=lambda g: -DIL_PATTERNS[g][1]):
        d = DIL_PATTERNS[g][1]
        sub_len = seq // d
        q_ref, k_ref, v_ref = qkv[g]
        for r in range(d):
            src = pl.ds(r, sub_len, stride=d) if d > 1 else pl.ds(0, seq)
            dst = slice(BLOCK + r * sub_len, BLOCK + (r + 1) * sub_len)
            q_st[g, dst, :] = (q_ref[0, src, :] * SCALE).astype(BF16)
            k_st[g, dst, :] = k_ref[0, src, :].astype(BF16)
            vx_st[g, dst, :LANES] = v_ref[0, src, :].astype(BF16)
            for n in range(sub_len // BLOCK):
                base = BLOCK + r * sub_len + n * BLOCK
                out_rows = pl.ds(n * BLOCK * d + r, BLOCK, stride=d) if d > 1 else pl.ds(n * BLOCK, BLOCK)
                blocks.append((g, base, n == 0, out_rows))

    nslot = s_ref.shape[0]
    for t in range(len(blocks) + 2):
        if t < len(blocks):
            g, base, first, _ = blocks[t]
            scores(g, t % nslot, base, first)
        for rg in range(n_rg):
            if 1 <= t <= len(blocks):
                softmax_group((t - 1) % nslot, rg)
            if rg == n_rg // 2 and 2 <= t:
                g, base, _, out_rows = blocks[t - 2]
                finish(g, (t - 2) % nslot, base, out_rows)
                if DIL_PATTERNS[g][1] == 1:
                    merge(out_rows)


def _mixer_a(proj_a, rev_a, weights):
    B, S, _ = proj_a.shape
    npair = OUT_WIDTH_A // LANES
    grid = (B, npair)

    def col(which, g):
        base = (which * WIDTH_A + g * OUT_WIDTH_A) // LANES
        return pl.BlockSpec((1, S, LANES), lambda b, hp: (b, 0, base + hp))

    cast_in, cast_out, cast_shapes = _cast_specs(weights, grid)
    in_specs = [col(which, g) for g in range(N_GROUPS_A) for which in range(3)]
    in_specs.append(pl.BlockSpec((1, 2 * N_GROUPS_A, 4 * BLOCK), lambda b, hp: (hp, 0, 0)))
    return pl.pallas_call(
        functools.partial(_mixer_a_kernel, seq=S, n_cast=len(weights)),
        grid=grid,
        in_specs=in_specs + cast_in,
        out_specs=[pl.BlockSpec((1, S, LANES), lambda b, hp: (b, 0, hp))] + cast_out,
        out_shape=[jax.ShapeDtypeStruct((B, S, OUT_WIDTH_A), BF16)] + cast_shapes,
        scratch_shapes=[pltpu.VMEM((2 * N_GROUPS_A, 2 * BLOCK, 2 * BLOCK), F32),
                        pltpu.VMEM((N_GROUPS_A, BLOCK + S, LANES), BF16),
                        pltpu.VMEM((N_GROUPS_A, BLOCK + S, LANES), BF16),
                        pltpu.VMEM((N_GROUPS_A, BLOCK + S, 2 * LANES), BF16),
                        pltpu.VMEM((4, 2 * BLOCK, 2 * BLOCK), F32),
                        pltpu.VMEM((4, 2 * BLOCK, 2 * BLOCK), BF16),
                        pltpu.VMEM((4, 2 * BLOCK, LANES), F32),
                        pltpu.VMEM((N_GROUPS_A, S, LANES), F32),
                        pltpu.VMEM((N_GROUPS_A, S, LANES), F32),
                        pltpu.VMEM((N_GROUPS_A, S, LANES), F32)],
        compiler_params=_params(("arbitrary", "arbitrary")),
        name="mixer_a",
    )(*([proj_a] * 9), rev_a, *weights)


def _mixer_b_kernel(*refs, tq, seq, lam_init, n_cast):
    nq = seq // tq
    q_ref, k_ref, v_ref, rev_ref, lam_ref, g_ref = refs[:6]
    o_ref = refs[6 + n_cast]
    toep_ref, vx_ref, kt_ref = refs[7 + 2 * n_cast:10 + 2 * n_cast]
    s_refs = refs[10 + 2 * n_cast:10 + 2 * n_cast + nq]
    p_refs = refs[10 + 2 * n_cast + nq:]
    _cast_blocks(refs[6:6 + n_cast], refs[7 + n_cast:7 + 2 * n_cast])
    rg_rows = BF16_ROWS
    n_rg = 2 * tq // rg_rows

    x = jnp.broadcast_to(rev_ref[0] * LOG2E, (tq, seq + tq))
    rolled = pltpu.roll(x, 0, 1, stride=1, stride_axis=0)
    toep_ref[:, :seq] = rolled[:, :seq]
    row = lax.broadcasted_iota(jnp.int32, (tq, tq), 0)
    col = lax.broadcasted_iota(jnp.int32, (tq, tq), 1)
    toep_ref[:, seq:] = jnp.where(col <= row, rolled[:, seq:], NEG_INF)

    vx_ref[:, :LANES] = v_ref[0]
    vx_ref[:, LANES:] = jnp.ones((seq, LANES), BF16)

    lane = lax.broadcasted_iota(jnp.int32, (tq, LANES), 1)
    lo = lane < HEAD_DIM
    eye = jnp.where(lax.broadcasted_iota(jnp.int32, (LANES, LANES), 0)
                    == lax.broadcasted_iota(jnp.int32, (LANES, LANES), 1), 1.0, 0.0).astype(BF16)
    kt_ref[...] = lax.dot_general(eye, k_ref[0], (((1,), (1,)), ((), ())),
                                  preferred_element_type=F32).astype(BF16)
    lv = lam_ref[...]
    lam = (jnp.exp(jnp.sum(lv[0:1] * lv[1:2], axis=-1, keepdims=True))
           - jnp.exp(jnp.sum(lv[2:3] * lv[3:4], axis=-1, keepdims=True)) + lam_init)

    def score_chunks(qi):
        q = q_ref[0, qi * tq:(qi + 1) * tq, :]
        zero = jnp.zeros_like(q)
        qz = jnp.concatenate([jnp.where(lo, q, zero), jnp.where(lo, zero, q)], axis=0)

        def chunk(c):
            s_refs[qi][:, c * tq:(c + 1) * tq] = jnp.dot(
                qz, kt_ref[:, c * tq:(c + 1) * tq], preferred_element_type=F32)

        return [functools.partial(chunk, c) for c in range(qi + 1)]

    def softmax_group(qi, rg):
        rows = slice(rg * rg_rows, (rg + 1) * rg_rows)
        brow = (rg * rg_rows) % tq
        bias = toep_ref[brow:brow + rg_rows, (nq - qi) * tq:(nq + 1) * tq]
        t = s_refs[qi][rows, :] + bias
        m = jnp.max(t, axis=-1, keepdims=True)
        p_refs[qi][rows, :] = jnp.exp2(t - m).astype(BF16)

    halves = {}

    def value_matmul(qi, half):
        acc = jnp.dot(p_refs[qi][half * tq:(half + 1) * tq, :], vx_ref[:(qi + 1) * tq, :],
                      preferred_element_type=F32)
        halves[qi, half] = acc[:, :LANES] / acc[:, LANES:]

    def finish(qi):
        y = halves[qi, 0] - lam * halves[qi, 1]
        y = _rms(y, g_ref[...]) * (1.0 - lam_init)
        o_ref[0, qi * tq:(qi + 1) * tq, :] = y.astype(o_ref.dtype)

    order = list(range(nq - 1, -1, -1))
    for t in range(nq + 2):
        scores_t = score_chunks(order[t]) if t < nq else []
        softmax_t, tail_t = [], []
        if 1 <= t <= nq:
            qi = order[t - 1]
            groups = [functools.partial(softmax_group, qi, rg) for rg in range(n_rg)]
            softmax_t = groups[:n_rg // 2] + [functools.partial(value_matmul, qi, 0)] + groups[n_rg // 2:]
        if t >= 2:
            qi = order[t - 2]
            tail_t = [functools.partial(value_matmul, qi, 1), functools.partial(finish, qi)]
        for emit in _interleave([scores_t, softmax_t, tail_t]):
            emit()


def _mixer_b(proj, rev_b, lam_vecs, subln_g, lam_init, weights, tq=256):
    B, S, _ = proj.shape
    H = N_HEADS_B
    nq = S // tq
    grid = (B, H)
    kern = functools.partial(_mixer_b_kernel, tq=tq, seq=S, lam_init=lam_init, n_cast=len(weights))

    def col(off):
        return pl.BlockSpec((1, S, LANES), lambda b, h: (b, 0, off // LANES + h))

    cast_in, cast_out, cast_shapes = _cast_specs(weights, grid)
    return pl.pallas_call(
        kern,
        grid=grid,
        in_specs=[col(OFF_QB), col(OFF_KB), col(OFF_VB),
                  pl.BlockSpec((1, 1, S + tq), lambda b, h: (h, 0, 0)),
                  pl.BlockSpec((4, HEAD_DIM), lambda b, h: (0, 0)),
                  pl.BlockSpec((1, 2 * HEAD_DIM), lambda b, h: (0, 0))] + cast_in,
        out_specs=[pl.BlockSpec((1, S, LANES), lambda b, h: (b, 0, h))] + cast_out,
        out_shape=[jax.ShapeDtypeStruct((B, S, WIDTH_B), BF16)] + cast_shapes,
        scratch_shapes=([pltpu.VMEM((tq, S + tq), F32), pltpu.VMEM((S, 2 * LANES), BF16),
                         pltpu.VMEM((LANES, S), BF16)]
                        + [pltpu.VMEM((2 * tq, (i + 1) * tq), F32) for i in range(nq)]
                        + [pltpu.VMEM((2 * tq, (i + 1) * tq), BF16) for i in range(nq)]),
        compiler_params=_params(("arbitrary", "arbitrary")),
        name="mixer_b",
    )(proj, proj, proj, rev_b, lam_vecs, subln_g, *weights)


def _attn_out_kernel(ya_ref, yb_ref, ga_ref, gb_ref, x_ref, wa_ref, wb_ref, wo_ref, g_ref, x1_ref, h_ref):
    pa = jnp.dot(ya_ref[...], wa_ref[...], preferred_element_type=F32)
    pb = jnp.dot(yb_ref[...], wb_ref[...], preferred_element_type=F32)
    ga = jax.nn.sigmoid(ga_ref[...].astype(F32))
    gb = jax.nn.sigmoid(gb_ref[...].astype(F32))
    merged = (ga * pa + gb * pb).astype(BF16)
    x1_ref[...] = x_ref[...] + jnp.dot(merged, wo_ref[...], preferred_element_type=F32)
    h_ref[...] = _rms(x1_ref[...], g_ref[...]).astype(BF16)


def _attn_out(ya, yb, proj_r, x2, wa, wb, wo, g, tm=512):
    T, D = x2.shape

    def resident(shape):
        return pl.BlockSpec(shape, lambda i: (0, 0), pipeline_mode=pl.Buffered(1))

    return pl.pallas_call(
        _attn_out_kernel,
        grid=(T // tm,),
        in_specs=[pl.BlockSpec((tm, OUT_WIDTH_A), lambda i: (i, 0)),
                  pl.BlockSpec((tm, WIDTH_B), lambda i: (i, 0)),
                  pl.BlockSpec((tm, D), lambda i: (i, OFF_GA // D)),
                  pl.BlockSpec((tm, D), lambda i: (i, OFF_GB // D)),
                  pl.BlockSpec((tm, D), lambda i: (i, 0)),
                  resident((OUT_WIDTH_A, D)), resident((WIDTH_B, D)), resident((D, D)), resident((1, D))],
        out_specs=[pl.BlockSpec((tm, D), lambda i: (i, 0)), pl.BlockSpec((tm, D), lambda i: (i, 0))],
        out_shape=[jax.ShapeDtypeStruct((T, D), F32), jax.ShapeDtypeStruct((T, D), BF16)],
        compiler_params=_params(("parallel",), vmem=VMEM_LIMIT_ATTN_OUT),
        name="attn_out",
    )(ya, yb, proj_r, proj_r, x2, wa, wb, wo, g)


def _ffn_kernel(x1_hbm, h_ref, wg_ref, wu_ref, wd_ref, gf_ref, o_ref, x1_buf, x1_sem, *, tm):
    i = pl.program_id(0)
    f = pl.program_id(1)
    x1_copy = pltpu.make_async_copy(x1_hbm.at[pl.ds(i * tm, tm), :], x1_buf, x1_sem)

    @pl.when(f == 0)
    def _():
        x1_copy.start()
        o_ref[...] = jnp.zeros(o_ref.shape, F32)

    h = h_ref[...]
    a = jnp.dot(h, wg_ref[...], preferred_element_type=F32)
    b = jnp.dot(h, wu_ref[...], preferred_element_type=F32)
    u = (a * jax.nn.sigmoid(a)) * b
    o_ref[...] += jnp.dot(u.astype(BF16), wd_ref[...], preferred_element_type=F32)

    @pl.when(f == pl.num_programs(1) - 1)
    def _():
        x1_copy.wait()
        o_ref[...] = _rms(x1_buf[...] + o_ref[...], gf_ref[...])


def _ffn(x1, h, wg, wu, wd, gf, tm=1024, tf=512):
    T, D = x1.shape
    F = wg.shape[1]
    return pl.pallas_call(
        functools.partial(_ffn_kernel, tm=tm),
        grid=(T // tm, F // tf),
        in_specs=[pl.BlockSpec(memory_space=pl.ANY),
                  pl.BlockSpec((tm, D), lambda i, f: (i, 0)),
                  pl.BlockSpec((D, tf), lambda i, f: (0, f)),
                  pl.BlockSpec((D, tf), lambda i, f: (0, f)),
                  pl.BlockSpec((tf, D), lambda i, f: (f, 0)),
                  pl.BlockSpec((1, D), lambda i, f: (0, 0))],
        out_specs=pl.BlockSpec((tm, D), lambda i, f: (i, 0)),
        out_shape=jax.ShapeDtypeStruct((T, D), F32),
        scratch_shapes=[pltpu.VMEM((tm, D), F32), pltpu.SemaphoreType.DMA(())],
        compiler_params=_params(("parallel", "arbitrary"), vmem=VMEM_LIMIT_FFN),
        name="ffn",
    )(x1, h, wg, wu, wd, gf)


def _rev_a_index():
    u = np.arange(4 * BLOCK)
    rel = np.clip(2 * BLOCK - u, 0, None)
    return np.stack([_rel_bucket_np(rel * d) for _, d in DIL_PATTERNS])


def _rev_b_index(seq, tq):
    c = np.arange(seq + tq)
    return _rel_bucket_np(np.clip(seq - c, 0, seq - 1))


def _lookup(table, idx):
    onehot = jnp.asarray(np.asarray(idx)[None, :] == np.arange(NUM_BUCKETS)[:, None])
    return jnp.sum(jnp.where(onehot[:, None, :], table[:, :, None], 0.0), axis=0)


def kernel(x, norm_attn_g, w_in, w_proj_a, w_proj_b, w_out, rel_bias_table, diff_lambda_q1, diff_lambda_k1, diff_lambda_q2, diff_lambda_k2, diff_subln_g, norm_ffn_g, w_ffn_gate, w_ffn_up, w_ffn_down, norm_final_g):
    B, S, D = x.shape
    T = B * S
    depth = w_in.shape[0]
    assert depth == 1, "the final RMSNorm is fused into the FFN epilogue of a single layer"
    table_a = rel_bias_table[:, :N_HEADS_A].astype(F32)
    table_b = rel_bias_table[:, N_HEADS_A:].astype(F32)
    tq = 256

    idx_a = _rev_a_index()
    rev_a = jnp.stack([_lookup(table_a[:, g * HEADS_PER_GROUP_A:(g + 1) * HEADS_PER_GROUP_A], idx_a[g])
                       for g in range(N_GROUPS_A)])
    npair = OUT_WIDTH_A // LANES
    rev_a = jnp.transpose(rev_a.reshape(N_GROUPS_A, npair, 2, 4 * BLOCK), (1, 0, 2, 3))
    rev_a = rev_a.reshape(npair, 2 * N_GROUPS_A, 4 * BLOCK)
    rev_b = _lookup(table_b, _rev_b_index(S, tq))[:, None, :]

    x2 = x.reshape(T, D)
    l = 0
    lam_init = 0.8 - 0.6 * math.exp(-0.3 * l)
    proj_a, proj_r = _in_proj(x2, norm_attn_g[l][None, :], w_in[l])
    ya, wa, wb, wo = _mixer_a(proj_a.reshape(B, S, PROJ_A), rev_a, [w_proj_a[l], w_proj_b[l], w_out[l]])

    lam_vecs = jnp.stack([diff_lambda_q1[l], diff_lambda_k1[l],
                          diff_lambda_q2[l], diff_lambda_k2[l]]).astype(F32)
    yb, wg, wu, wd = _mixer_b(proj_r.reshape(B, S, PROJ_R), rev_b, lam_vecs, diff_subln_g[l][None, :].astype(F32),
                              lam_init, [w_ffn_gate[l], w_ffn_up[l], w_ffn_down[l]], tq=tq)

    x1, h = _attn_out(ya.reshape(T, OUT_WIDTH_A), yb.reshape(T, WIDTH_B), proj_r, x2, wa, wb, wo,
                      norm_ffn_g[l][None, :])
    out = _ffn(x1, h, wg, wu, wd, norm_final_g[None, :])
    return out.reshape(B, S, D)
```

```python
import functools
import math

import numpy as np
import jax
import jax.numpy as jnp
from jax import lax
from jax.experimental import pallas as pl
from jax.experimental.pallas import tpu as pltpu

D_MODEL = 2048
HEAD_DIM = 64
DIL_PATTERNS = ((128, 1), (512, 4), (2048, 16))
N_GROUPS_A = len(DIL_PATTERNS)
HEADS_PER_GROUP_A = 8
N_HEADS_A = N_GROUPS_A * HEADS_PER_GROUP_A
WIDTH_A = N_HEADS_A * HEAD_DIM
OUT_WIDTH_A = HEADS_PER_GROUP_A * HEAD_DIM
BLOCK = 128
N_HEADS_B = D_MODEL // (2 * HEAD_DIM)
WIDTH_B = N_HEADS_B * 2 * HEAD_DIM
NUM_BUCKETS = 32
MAX_DISTANCE = 2048
PROJ_A = 3 * WIDTH_A
PROJ_R = 3 * WIDTH_B + 2 * D_MODEL
D_IN = PROJ_A + PROJ_R
NORM_EPS = 1e-6
NEG_INF = -1e30
SCALE = HEAD_DIM ** -0.5
LOG2E = math.log2(math.e)

OFF_QB = 0
OFF_KB = WIDTH_B
OFF_VB = 2 * WIDTH_B
OFF_GA = 3 * WIDTH_B
OFF_GB = OFF_GA + D_MODEL

LANES = 128
VMEM_LIMIT = 56 * 1024 * 1024
VMEM_LIMIT_ATTN_OUT = 58 * 1024 * 1024
VMEM_LIMIT_FFN = 60 * 1024 * 1024

BF16 = jnp.bfloat16
F32 = jnp.float32


def _rel_bucket_np(dist):
    n = np.maximum(dist, 0)
    max_exact = NUM_BUCKETS // 2
    nf = np.maximum(n, 1).astype(np.float32)
    large = max_exact + (np.log(nf / np.float32(max_exact)) / np.float32(math.log(MAX_DISTANCE / max_exact))
                         * np.float32(NUM_BUCKETS - max_exact)).astype(np.int32)
    large = np.minimum(large, NUM_BUCKETS - 1)
    return np.where(n < max_exact, n, large).astype(np.int32)


def _rms(x, g):
    ms = jnp.mean(x * x, axis=-1, keepdims=True)
    return x * lax.rsqrt(ms + NORM_EPS) * g


def _params(sem, vmem=VMEM_LIMIT):
    return pltpu.CompilerParams(dimension_semantics=sem, vmem_limit_bytes=vmem)


BF16_ROWS = 16


def _cast_specs(weights, grid):
    nsteps = math.prod(grid)
    in_specs, out_specs, out_shapes = [], [], []
    for w in weights:
        rows, cols = w.shape
        blk = next(r for r in range(BF16_ROWS, rows + 1, BF16_ROWS)
                   if rows % r == 0 and nsteps % (rows // r) == 0 and rows // r <= nsteps)
        per = nsteps // (rows // blk)

        def index(*ids, per=per):
            step = 0
            for i, n in zip(ids, grid):
                step = step * n + i
            return (step // per, 0)

        in_specs.append(pl.BlockSpec((blk, cols), index))
        out_specs.append(pl.BlockSpec((blk, cols), index))
        out_shapes.append(jax.ShapeDtypeStruct((rows, cols), BF16))
    return in_specs, out_specs, out_shapes


def _interleave(lists):
    keyed = [((i + 0.5) / len(items), n, i, item) for n, items in enumerate(lists) for i, item in enumerate(items)]
    return [item for _, _, _, item in sorted(keyed, key=lambda k: k[:3])]


def _cast_blocks(in_refs, out_refs):
    for i_ref, o_ref in zip(in_refs, out_refs):
        o_ref[...] = i_ref[...].astype(BF16)


def _in_proj_kernel(x_hbm, g_ref, w_ref, oa_ref, or_ref, h_ref, x_buf, x_sem, *, na, nqb, tm):
    i = pl.program_id(0)
    j = pl.program_id(1)

    def x_copy(tile):
        return pltpu.make_async_copy(x_hbm.at[pl.ds(tile * tm, tm), :], x_buf, x_sem)

    @pl.when(j == 0)
    def _():
        @pl.when(i == 0)
        def _():
            x_copy(0).start()

        x_copy(i).wait()
        h_ref[...] = _rms(x_buf[...], g_ref[...]).astype(BF16)

        @pl.when(i + 1 < pl.num_programs(0))
        def _():
            x_copy(i + 1).start()

    @pl.when(j < na)
    def _():
        oa_ref[...] = jnp.dot(h_ref[...], w_ref[...].astype(BF16), preferred_element_type=F32)

    @pl.when(j >= na)
    def _():
        scale = jnp.where(j < na + nqb, jnp.float32(SCALE * LOG2E), jnp.float32(1.0))
        res = jnp.dot(h_ref[...], w_ref[...].astype(BF16), preferred_element_type=F32)
        or_ref[...] = (res * scale).astype(BF16)


def _in_proj(x2, g, w, tm=2048, tn=512):
    T, D = x2.shape
    na = PROJ_A // tn
    assert OFF_QB == 0 and WIDTH_B % tn == 0
    return pl.pallas_call(
        functools.partial(_in_proj_kernel, na=na, nqb=WIDTH_B // tn, tm=tm),
        grid=(T // tm, D_IN // tn),
        in_specs=[pl.BlockSpec(memory_space=pl.ANY),
                  pl.BlockSpec((1, D), lambda i, j: (0, 0)),
                  pl.BlockSpec((D, tn), lambda i, j: (0, j))],
        out_specs=[pl.BlockSpec((tm, tn), lambda i, j: (i, jnp.minimum(j, na - 1))),
                   pl.BlockSpec((tm, tn), lambda i, j: (i, jnp.maximum(j - na, 0)))],
        out_shape=[jax.ShapeDtypeStruct((T, PROJ_A), F32), jax.ShapeDtypeStruct((T, PROJ_R), BF16)],
        scratch_shapes=[pltpu.VMEM((tm, D), BF16), pltpu.VMEM((tm, D), F32), pltpu.SemaphoreType.DMA(())],
        compiler_params=_params(("arbitrary", "arbitrary")),
        name="in_proj",
    )(x2, g, w)


def _mixer_a_kernel(*refs, seq, n_cast):
    qkv = (refs[0:3], refs[3:6], refs[6:9])
    rev_ref = refs[9]
    o_ref = refs[10 + n_cast]
    (bias_ref, q_st, kt_st, vx_st, s_ref, p_ref, mrow_ref, m_ref, l_ref, acc_ref) = refs[11 + 2 * n_cast:]
    _cast_blocks(refs[10:10 + n_cast], refs[11 + n_cast:11 + 2 * n_cast])
    lane = lax.broadcasted_iota(jnp.int32, (BLOCK, LANES), 1)
    lo = lane < HEAD_DIM

    row = lax.broadcasted_iota(jnp.int32, (2 * BLOCK, 2 * BLOCK), 0)
    col = lax.broadcasted_iota(jnp.int32, (2 * BLOCK, 2 * BLOCK), 1)
    rel = BLOCK + (row & (BLOCK - 1)) - col
    band = (rel >= 0) & (rel <= BLOCK)
    band_first = band & (col >= BLOCK)
    @pl.when(pl.program_id(1) == 0)
    def _():
        for g in range(N_GROUPS_A):
            halves = []
            for hh in range(2):
                x = jnp.broadcast_to(rev_ref[0, g * 2 + hh:g * 2 + hh + 1, :], (BLOCK, 4 * BLOCK))
                halves.append(pltpu.roll(x, 0, 1, stride=1, stride_axis=0)[:, BLOCK:3 * BLOCK])
            toep = jnp.concatenate(halves, axis=0)
            bias_ref[2 * g] = jnp.where(band, toep, NEG_INF)
            bias_ref[2 * g + 1] = jnp.where(band_first, toep, NEG_INF)

    eye = jnp.where(lax.broadcasted_iota(jnp.int32, (LANES, LANES), 0) == lane, 1.0, 0.0).astype(BF16)
    for g in range(N_GROUPS_A):
        kt_st[g, :, 0:BLOCK] = jnp.zeros((LANES, BLOCK), BF16)
        vx_st[g, 0:BLOCK, :] = jnp.zeros((BLOCK, 2 * LANES), BF16)
        vx_st[g, :, LANES:] = jnp.ones((BLOCK + seq, LANES), BF16)

    def scores(g, slot, base, first):
        q = q_st[g, base:base + BLOCK, :]
        zero = jnp.zeros_like(q)
        qz = jnp.concatenate([jnp.where(lo, q, zero), jnp.where(lo, zero, q)], axis=0)
        s = jnp.dot(qz, kt_st[g, :, base - BLOCK:base + BLOCK], preferred_element_type=F32)
        s_ref[slot] = s + bias_ref[2 * g + (1 if first else 0)]

    def softmax_group(slot, rg):
        rows = slice(rg * BF16_ROWS, (rg + 1) * BF16_ROWS)
        m = jnp.max(s_ref[slot, rows, :], axis=-1, keepdims=True)
        p_ref[slot, rows, :] = jnp.exp(s_ref[slot, rows, :] - m).astype(BF16)
        mrow_ref[slot, rows, :] = jnp.broadcast_to(m, (BF16_ROWS, LANES))

    def finish(g, slot, base):
        acc = jnp.dot(p_ref[slot], vx_st[g, base - BLOCK:base + BLOCK, :], preferred_element_type=F32)
        mrow = mrow_ref[slot]
        return (jnp.where(lo, mrow[:BLOCK], mrow[BLOCK:]),
                jnp.where(lo, acc[:BLOCK, LANES:], acc[BLOCK:, LANES:]),
                jnp.where(lo, acc[:BLOCK, :LANES], acc[BLOCK:, :LANES]))

    def merge(rows, last):
        states = [(m_ref[g, rows, :], l_ref[g, rows, :], acc_ref[g, rows, :]) for g in dilated] + [last]
        mx = functools.reduce(jnp.maximum, [m for m, _, _ in states])
        num = jnp.zeros((BLOCK, LANES), F32)
        den = jnp.zeros((BLOCK, LANES), F32)
        for m, l, acc in states:
            w = jnp.exp(m - mx)
            num = num + w * acc
            den = den + w * l
        o_ref[0, rows, :] = (num / den).astype(o_ref.dtype)

    dilated = [g for g in range(N_GROUPS_A) if DIL_PATTERNS[g][1] > 1]

    n_rg = 2 * BLOCK // BF16_ROWS
    blocks = []
    for g in sorted(range(N_GROUPS_A), key=lambda g: -DIL_PATTERNS[g][1]):
        d = DIL_PATTERNS[g][1]
        sub_len = seq // d
        q_ref, k_ref, v_ref = qkv[g]
        for r in range(d):
            src = pl.ds(r, sub_len, stride=d) if d > 1 else pl.ds(0, seq)
            dst = slice(BLOCK + r * sub_len, BLOCK + (r + 1) * sub_len)
            q_st[g, dst, :] = (q_ref[0, src, :] * SCALE).astype(BF16)
            kt_st[g, :, dst] = lax.dot_general(eye, k_ref[0, src, :].astype(BF16), (((1,), (1,)), ((), ())),
                                               preferred_element_type=F32).astype(BF16)
            vx_st[g, dst, :LANES] = v_ref[0, src, :].astype(BF16)
            for n in range(sub_len // BLOCK):
                base = BLOCK + r * sub_len + n * BLOCK
                out_rows = pl.ds(n * BLOCK * d + r, BLOCK, stride=d) if d > 1 else pl.ds(n * BLOCK, BLOCK)
                blocks.append((g, base, n == 0, out_rows))

    nslot = s_ref.shape[0]
    for t in range(len(blocks) + 2):
        if t < len(blocks):
            g, base, first, _ = blocks[t]
            scores(g, t % nslot, base, first)
        for rg in range(n_rg):
            if 1 <= t <= len(blocks):
                softmax_group((t - 1) % nslot, rg)
            if rg == n_rg // 2 and 2 <= t:
                g, base, _, out_rows = blocks[t - 2]
                state = finish(g, (t - 2) % nslot, base)
                if g in dilated:
                    m_ref[g, out_rows, :], l_ref[g, out_rows, :], acc_ref[g, out_rows, :] = state
                else:
                    merge(out_rows, state)


def _mixer_a(proj_a, rev_a, weights):
    B, S, _ = proj_a.shape
    npair = OUT_WIDTH_A // LANES
    grid = (npair, B)

    def col(which, g):
        base = (which * WIDTH_A + g * OUT_WIDTH_A) // LANES
        return pl.BlockSpec((1, S, LANES), lambda hp, b: (b, 0, base + hp))

    cast_in, cast_out, cast_shapes = _cast_specs(weights, grid)
    in_specs = [col(which, g) for g in range(N_GROUPS_A) for which in range(3)]
    in_specs.append(pl.BlockSpec((1, 2 * N_GROUPS_A, 4 * BLOCK), lambda hp, b: (hp, 0, 0)))
    return pl.pallas_call(
        functools.partial(_mixer_a_kernel, seq=S, n_cast=len(weights)),
        grid=grid,
        in_specs=in_specs + cast_in,
        out_specs=[pl.BlockSpec((1, S, LANES), lambda hp, b: (b, 0, hp))] + cast_out,
        out_shape=[jax.ShapeDtypeStruct((B, S, OUT_WIDTH_A), BF16)] + cast_shapes,
        scratch_shapes=[pltpu.VMEM((2 * N_GROUPS_A, 2 * BLOCK, 2 * BLOCK), F32),
                        pltpu.VMEM((N_GROUPS_A, BLOCK + S, LANES), BF16),
                        pltpu.VMEM((N_GROUPS_A, LANES, BLOCK + S), BF16),
                        pltpu.VMEM((N_GROUPS_A, BLOCK + S, 2 * LANES), BF16),
                        pltpu.VMEM((4, 2 * BLOCK, 2 * BLOCK), F32),
                        pltpu.VMEM((4, 2 * BLOCK, 2 * BLOCK), BF16),
                        pltpu.VMEM((4, 2 * BLOCK, LANES), F32),
                        pltpu.VMEM((N_GROUPS_A, S, LANES), F32),
                        pltpu.VMEM((N_GROUPS_A, S, LANES), F32),
                        pltpu.VMEM((N_GROUPS_A, S, LANES), F32)],
        compiler_params=_params(("arbitrary", "arbitrary")),
        name="mixer_a",
    )(*([proj_a] * 9), rev_a, *weights)


def _mixer_b_kernel(*refs, tq, seq, lam_init, n_cast):
    nq = seq // tq
    q_ref, k_ref, v_ref, rev_ref, lam_ref, g_ref = refs[:6]
    o_ref = refs[6 + n_cast]
    toep_ref, vx_ref = refs[7 + 2 * n_cast:9 + 2 * n_cast]
    s_refs = refs[9 + 2 * n_cast:9 + 2 * n_cast + nq]
    p_refs = refs[9 + 2 * n_cast + nq:]
    _cast_blocks(refs[6:6 + n_cast], refs[7 + n_cast:7 + 2 * n_cast])
    rg_rows = BF16_ROWS
    n_rg = 2 * tq // rg_rows

    x = jnp.broadcast_to(rev_ref[0] * LOG2E, (tq, seq + tq))
    rolled = pltpu.roll(x, 0, 1, stride=1, stride_axis=0)
    toep_ref[:, :seq] = rolled[:, :seq]
    row = lax.broadcasted_iota(jnp.int32, (tq, tq), 0)
    col = lax.broadcasted_iota(jnp.int32, (tq, tq), 1)
    toep_ref[:, seq:] = jnp.where(col <= row, rolled[:, seq:], NEG_INF)

    vx_ref[:, :LANES] = v_ref[0]
    vx_ref[:, LANES:] = jnp.ones((seq, LANES), BF16)

    lane = lax.broadcasted_iota(jnp.int32, (tq, LANES), 1)
    lo = lane < HEAD_DIM
    lv = lam_ref[...]
    lam = (jnp.exp(jnp.sum(lv[0:1] * lv[1:2], axis=-1, keepdims=True))
           - jnp.exp(jnp.sum(lv[2:3] * lv[3:4], axis=-1, keepdims=True)) + lam_init)

    def score_chunks(qi):
        q = q_ref[0, qi * tq:(qi + 1) * tq, :]
        zero = jnp.zeros_like(q)
        qz = jnp.concatenate([jnp.where(lo, q, zero), jnp.where(lo, zero, q)], axis=0)

        def chunk(c):
            kc = k_ref[0, c * tq:(c + 1) * tq, :]
            s_refs[qi][:, c * tq:(c + 1) * tq] = lax.dot_general(
                qz, kc, (((1,), (1,)), ((), ())), preferred_element_type=F32)

        return [functools.partial(chunk, c) for c in range(qi + 1)]

    def softmax_group(qi, rg):
        rows = slice(rg * rg_rows, (rg + 1) * rg_rows)
        brow = (rg * rg_rows) % tq
        bias = toep_ref[brow:brow + rg_rows, (nq - qi) * tq:(nq + 1) * tq]
        t = s_refs[qi][rows, :] + bias
        m = jnp.max(t, axis=-1, keepdims=True)
        p_refs[qi][rows, :] = jnp.exp2(t - m).astype(BF16)

    halves = {}

    def value_matmul(qi, half):
        acc = jnp.dot(p_refs[qi][half * tq:(half + 1) * tq, :], vx_ref[:(qi + 1) * tq, :],
                      preferred_element_type=F32)
        halves[qi, half] = acc[:, :LANES] / acc[:, LANES:]

    def finish(qi):
        y = halves[qi, 0] - lam * halves[qi, 1]
        y = _rms(y, g_ref[...]) * (1.0 - lam_init)
        o_ref[0, qi * tq:(qi + 1) * tq, :] = y.astype(o_ref.dtype)

    order = list(range(nq - 1, -1, -1))
    for t in range(nq + 2):
        scores_t = score_chunks(order[t]) if t < nq else []
        softmax_t, tail_t = [], []
        if 1 <= t <= nq:
            qi = order[t - 1]
            groups = [functools.partial(softmax_group, qi, rg) for rg in range(n_rg)]
            softmax_t = groups[:n_rg // 2] + [functools.partial(value_matmul, qi, 0)] + groups[n_rg // 2:]
        if t >= 2:
            qi = order[t - 2]
            tail_t = [functools.partial(value_matmul, qi, 1), functools.partial(finish, qi)]
        for emit in _interleave([scores_t, softmax_t, tail_t]):
            emit()


def _mixer_b(proj, rev_b, lam_vecs, subln_g, lam_init, weights, tq=256):
    B, S, _ = proj.shape
    H = N_HEADS_B
    nq = S // tq
    grid = (B, H)
    kern = functools.partial(_mixer_b_kernel, tq=tq, seq=S, lam_init=lam_init, n_cast=len(weights))

    def col(off):
        return pl.BlockSpec((1, S, LANES), lambda b, h: (b, 0, off // LANES + h))

    cast_in, cast_out, cast_shapes = _cast_specs(weights, grid)
    return pl.pallas_call(
        kern,
        grid=grid,
        in_specs=[col(OFF_QB), col(OFF_KB), col(OFF_VB),
                  pl.BlockSpec((1, 1, S + tq), lambda b, h: (h, 0, 0)),
                  pl.BlockSpec((4, HEAD_DIM), lambda b, h: (0, 0)),
                  pl.BlockSpec((1, 2 * HEAD_DIM), lambda b, h: (0, 0))] + cast_in,
        out_specs=[pl.BlockSpec((1, S, LANES), lambda b, h: (b, 0, h))] + cast_out,
        out_shape=[jax.ShapeDtypeStruct((B, S, WIDTH_B), BF16)] + cast_shapes,
        scratch_shapes=([pltpu.VMEM((tq, S + tq), F32), pltpu.VMEM((S, 2 * LANES), BF16)]
                        + [pltpu.VMEM((2 * tq, (i + 1) * tq), F32) for i in range(nq)]
                        + [pltpu.VMEM((2 * tq, (i + 1) * tq), BF16) for i in range(nq)]),
        compiler_params=_params(("arbitrary", "arbitrary")),
        name="mixer_b",
    )(proj, proj, proj, rev_b, lam_vecs, subln_g, *weights)


def _attn_out_kernel(ya_ref, yb_ref, ga_ref, gb_ref, x_ref, wa_ref, wb_ref, wo_ref, g_ref, x1_ref, h_ref):
    pa = jnp.dot(ya_ref[...], wa_ref[...], preferred_element_type=F32)
    pb = jnp.dot(yb_ref[...], wb_ref[...], preferred_element_type=F32)
    ga = jax.nn.sigmoid(ga_ref[...].astype(F32))
    gb = jax.nn.sigmoid(gb_ref[...].astype(F32))
    merged = (ga * pa + gb * pb).astype(BF16)
    x1_ref[...] = x_ref[...] + jnp.dot(merged, wo_ref[...], preferred_element_type=F32)
    h_ref[...] = _rms(x1_ref[...], g_ref[...]).astype(BF16)


def _attn_out(ya, yb, proj_r, x2, wa, wb, wo, g, tm=512):
    T, D = x2.shape

    def resident(shape):
        return pl.BlockSpec(shape, lambda i: (0, 0), pipeline_mode=pl.Buffered(1))

    return pl.pallas_call(
        _attn_out_kernel,
        grid=(T // tm,),
        in_specs=[pl.BlockSpec((tm, OUT_WIDTH_A), lambda i: (i, 0)),
                  pl.BlockSpec((tm, WIDTH_B), lambda i: (i, 0)),
                  pl.BlockSpec((tm, D), lambda i: (i, OFF_GA // D)),
                  pl.BlockSpec((tm, D), lambda i: (i, OFF_GB // D)),
                  pl.BlockSpec((tm, D), lambda i: (i, 0)),
                  resident((OUT_WIDTH_A, D)), resident((WIDTH_B, D)), resident((D, D)), resident((1, D))],
        out_specs=[pl.BlockSpec((tm, D), lambda i: (i, 0)), pl.BlockSpec((tm, D), lambda i: (i, 0))],
        out_shape=[jax.ShapeDtypeStruct((T, D), F32), jax.ShapeDtypeStruct((T, D), BF16)],
        compiler_params=_params(("parallel",), vmem=VMEM_LIMIT_ATTN_OUT),
        name="attn_out",
    )(ya, yb, proj_r, proj_r, x2, wa, wb, wo, g)


def _ffn_kernel(x1_hbm, h_ref, wg_ref, wu_ref, wd_ref, gf_ref, o_ref, x1_buf, x1_sem, *, tm):
    i = pl.program_id(0)
    f = pl.program_id(1)
    x1_copy = pltpu.make_async_copy(x1_hbm.at[pl.ds(i * tm, tm), :], x1_buf, x1_sem)

    @pl.when(f == 0)
    def _():
        x1_copy.start()
        o_ref[...] = jnp.zeros(o_ref.shape, F32)

    h = h_ref[...]
    a = jnp.dot(h, wg_ref[...], preferred_element_type=F32)
    b = jnp.dot(h, wu_ref[...], preferred_element_type=F32)
    u = (a * jax.nn.sigmoid(a)) * b
    o_ref[...] += jnp.dot(u.astype(BF16), wd_ref[...], preferred_element_type=F32)

    @pl.when(f == pl.num_programs(1) - 1)
    def _():
        x1_copy.wait()
        o_ref[...] = _rms(x1_buf[...] + o_ref[...], gf_ref[...])


def _ffn(x1, h, wg, wu, wd, gf, tm=1024, tf=512):
    T, D = x1.shape
    F = wg.shape[1]
    return pl.pallas_call(
        functools.partial(_ffn_kernel, tm=tm),
        grid=(T // tm, F // tf),
        in_specs=[pl.BlockSpec(memory_space=pl.ANY),
                  pl.BlockSpec((tm, D), lambda i, f: (i, 0)),
                  pl.BlockSpec((D, tf), lambda i, f: (0, f)),
                  pl.BlockSpec((D, tf), lambda i, f: (0, f)),
                  pl.BlockSpec((tf, D), lambda i, f: (f, 0)),
                  pl.BlockSpec((1, D), lambda i, f: (0, 0))],
        out_specs=pl.BlockSpec((tm, D), lambda i, f: (i, 0)),
        out_shape=jax.ShapeDtypeStruct((T, D), F32),
        scratch_shapes=[pltpu.VMEM((tm, D), F32), pltpu.SemaphoreType.DMA(())],
        compiler_params=_params(("parallel", "arbitrary"), vmem=VMEM_LIMIT_FFN),
        name="ffn",
    )(x1, h, wg, wu, wd, gf)


def _rev_a_index():
    u = np.arange(4 * BLOCK)
    rel = np.clip(2 * BLOCK - u, 0, None)
    return np.stack([_rel_bucket_np(rel * d) for _, d in DIL_PATTERNS])


def _rev_b_index(seq, tq):
    c = np.arange(seq + tq)
    return _rel_bucket_np(np.clip(seq - c, 0, seq - 1))


def _lookup(table, idx):
    onehot = jnp.asarray(np.asarray(idx)[None, :] == np.arange(NUM_BUCKETS)[:, None])
    return jnp.sum(jnp.where(onehot[:, None, :], table[:, :, None], 0.0), axis=0)


def kernel(x, norm_attn_g, w_in, w_proj_a, w_proj_b, w_out, rel_bias_table, diff_lambda_q1, diff_lambda_k1, diff_lambda_q2, diff_lambda_k2, diff_subln_g, norm_ffn_g, w_ffn_gate, w_ffn_up, w_ffn_down, norm_final_g):
    B, S, D = x.shape
    T = B * S
    depth = w_in.shape[0]
    assert depth == 1, "the final RMSNorm is fused into the FFN epilogue of a single layer"
    table_a = rel_bias_table[:, :N_HEADS_A].astype(F32)
    table_b = rel_bias_table[:, N_HEADS_A:].astype(F32)
    tq = 256

    idx_a = _rev_a_index()
    rev_a = jnp.stack([_lookup(table_a[:, g * HEADS_PER_GROUP_A:(g + 1) * HEADS_PER_GROUP_A], idx_a[g])
                       for g in range(N_GROUPS_A)])
    npair = OUT_WIDTH_A // LANES
    rev_a = jnp.transpose(rev_a.reshape(N_GROUPS_A, npair, 2, 4 * BLOCK), (1, 0, 2, 3))
    rev_a = rev_a.reshape(npair, 2 * N_GROUPS_A, 4 * BLOCK)
    rev_b = _lookup(table_b, _rev_b_index(S, tq))[:, None, :]

    x2 = x.reshape(T, D)
    l = 0
    lam_init = 0.8 - 0.6 * math.exp(-0.3 * l)
    proj_a, proj_r = _in_proj(x2, norm_attn_g[l][None, :], w_in[l])
    ya, wa, wb, wo = _mixer_a(proj_a.reshape(B, S, PROJ_A), rev_a, [w_proj_a[l], w_proj_b[l], w_out[l]])

    lam_vecs = jnp.stack([diff_lambda_q1[l], diff_lambda_k1[l],
                          diff_lambda_q2[l], diff_lambda_k2[l]]).astype(F32)
    yb, wg, wu, wd = _mixer_b(proj_r.reshape(B, S, PROJ_R), rev_b, lam_vecs, diff_subln_g[l][None, :].astype(F32),
                              lam_init, [w_ffn_gate[l], w_ffn_up[l], w_ffn_down[l]], tq=tq)

    x1, h = _attn_out(ya.reshape(T, OUT_WIDTH_A), yb.reshape(T, WIDTH_B), proj_r, x2, wa, wb, wo,
                      norm_ffn_g[l][None, :])
    out = _ffn(x1, h, wg, wu, wd, norm_final_g[None, :])
    return out.reshape(B, S, D)
```

```python
import functools
import math

import numpy as np
import jax
import jax.numpy as jnp
from jax import lax
from jax.experimental import pallas as pl
from jax.experimental.pallas import tpu as pltpu

D_MODEL = 2048
HEAD_DIM = 64
DIL_PATTERNS = ((128, 1), (512, 4), (2048, 16))
N_GROUPS_A = len(DIL_PATTERNS)
HEADS_PER_GROUP_A = 8
N_HEADS_A = N_GROUPS_A * HEADS_PER_GROUP_A
WIDTH_A = N_HEADS_A * HEAD_DIM
OUT_WIDTH_A = HEADS_PER_GROUP_A * HEAD_DIM
BLOCK = 128
N_HEADS_B = D_MODEL // (2 * HEAD_DIM)
WIDTH_B = N_HEADS_B * 2 * HEAD_DIM
NUM_BUCKETS = 32
MAX_DISTANCE = 2048
PROJ_A = 3 * WIDTH_A
PROJ_R = 3 * WIDTH_B + 2 * D_MODEL
D_IN = PROJ_A + PROJ_R
NORM_EPS = 1e-6
NEG_INF = -1e30
SCALE = HEAD_DIM ** -0.5
LOG2E = math.log2(math.e)

OFF_QB = 0
OFF_KB = WIDTH_B
OFF_VB = 2 * WIDTH_B
OFF_GA = 3 * WIDTH_B
OFF_GB = OFF_GA + D_MODEL

LANES = 128
VMEM_LIMIT = 56 * 1024 * 1024
VMEM_LIMIT_ATTN_OUT = 58 * 1024 * 1024
VMEM_LIMIT_FFN = 60 * 1024 * 1024

BF16 = jnp.bfloat16
F32 = jnp.float32


def _rel_bucket_np(dist):
    n = np.maximum(dist, 0)
    max_exact = NUM_BUCKETS // 2
    nf = np.maximum(n, 1).astype(np.float32)
    large = max_exact + (np.log(nf / np.float32(max_exact)) / np.float32(math.log(MAX_DISTANCE / max_exact))
                         * np.float32(NUM_BUCKETS - max_exact)).astype(np.int32)
    large = np.minimum(large, NUM_BUCKETS - 1)
    return np.where(n < max_exact, n, large).astype(np.int32)


def _rms(x, g):
    ms = jnp.mean(x * x, axis=-1, keepdims=True)
    return x * lax.rsqrt(ms + NORM_EPS) * g


def _params(sem, vmem=VMEM_LIMIT):
    return pltpu.CompilerParams(dimension_semantics=sem, vmem_limit_bytes=vmem)


BF16_ROWS = 16


def _cast_specs(weights, grid):
    nsteps = math.prod(grid)
    in_specs, out_specs, out_shapes = [], [], []
    for w in weights:
        rows, cols = w.shape
        blk = next(r for r in range(BF16_ROWS, rows + 1, BF16_ROWS)
                   if rows % r == 0 and nsteps % (rows // r) == 0 and rows // r <= nsteps)
        per = nsteps // (rows // blk)

        def index(*ids, per=per):
            step = 0
            for i, n in zip(ids, grid):
                step = step * n + i
            return (step // per, 0)

        in_specs.append(pl.BlockSpec((blk, cols), index))
        out_specs.append(pl.BlockSpec((blk, cols), index))
        out_shapes.append(jax.ShapeDtypeStruct((rows, cols), BF16))
    return in_specs, out_specs, out_shapes


def _interleave(lists):
    keyed = [((i + 0.5) / len(items), n, i, item) for n, items in enumerate(lists) for i, item in enumerate(items)]
    return [item for _, _, _, item in sorted(keyed, key=lambda k: k[:3])]


def _cast_blocks(in_refs, out_refs):
    for i_ref, o_ref in zip(in_refs, out_refs):
        o_ref[...] = i_ref[...].astype(BF16)


def _in_proj_kernel(x_hbm, g_ref, w_ref, oa_ref, or_ref, h_ref, x_buf, x_sem, *, na, nqb, tm):
    i = pl.program_id(0)
    j = pl.program_id(1)

    def x_copy(tile):
        return pltpu.make_async_copy(x_hbm.at[pl.ds(tile * tm, tm), :], x_buf, x_sem)

    @pl.when(j == 0)
    def _():
        @pl.when(i == 0)
        def _():
            x_copy(0).start()

        x_copy(i).wait()
        h = _rms(x_buf[...], g_ref[...]).astype(BF16)
        h_ref[...] = h
        oa_ref[...] = jnp.dot(h, w_ref[...].astype(BF16), preferred_element_type=F32)

        @pl.when(i + 1 < pl.num_programs(0))
        def _():
            x_copy(i + 1).start()

    @pl.when((j > 0) & (j < na))
    def _():
        oa_ref[...] = jnp.dot(h_ref[...], w_ref[...].astype(BF16), preferred_element_type=F32)

    @pl.when(j >= na)
    def _():
        scale = jnp.where(j < na + nqb, jnp.float32(SCALE * LOG2E), jnp.float32(1.0))
        res = jnp.dot(h_ref[...], w_ref[...].astype(BF16), preferred_element_type=F32)
        or_ref[...] = (res * scale).astype(BF16)


def _in_proj(x2, g, w, tm=2048, tn=512):
    T, D = x2.shape
    na = PROJ_A // tn
    assert OFF_QB == 0 and WIDTH_B % tn == 0
    return pl.pallas_call(
        functools.partial(_in_proj_kernel, na=na, nqb=WIDTH_B // tn, tm=tm),
        grid=(T // tm, D_IN // tn),
        in_specs=[pl.BlockSpec(memory_space=pl.ANY),
                  pl.BlockSpec((1, D), lambda i, j: (0, 0)),
                  pl.BlockSpec((D, tn), lambda i, j: (0, j))],
        out_specs=[pl.BlockSpec((tm, tn), lambda i, j: (i, jnp.minimum(j, na - 1))),
                   pl.BlockSpec((tm, tn), lambda i, j: (i, jnp.maximum(j - na, 0)))],
        out_shape=[jax.ShapeDtypeStruct((T, PROJ_A), F32), jax.ShapeDtypeStruct((T, PROJ_R), BF16)],
        scratch_shapes=[pltpu.VMEM((tm, D), BF16), pltpu.VMEM((tm, D), F32), pltpu.SemaphoreType.DMA(())],
        compiler_params=_params(("arbitrary", "arbitrary")),
        name="in_proj",
    )(x2, g, w)


def _mixer_a_kernel(*refs, seq, n_cast):
    qkv = (refs[0:3], refs[3:6], refs[6:9])
    rev_ref = refs[9]
    o_ref = refs[10 + n_cast]
    (bias_ref, q_st, kt_st, vx_st, s_ref, p_ref, mrow_ref, m_ref, l_ref, acc_ref) = refs[11 + 2 * n_cast:]
    _cast_blocks(refs[10:10 + n_cast], refs[11 + n_cast:11 + 2 * n_cast])
    lane = lax.broadcasted_iota(jnp.int32, (BLOCK, LANES), 1)
    lo = lane < HEAD_DIM

    row = lax.broadcasted_iota(jnp.int32, (2 * BLOCK, 2 * BLOCK), 0)
    col = lax.broadcasted_iota(jnp.int32, (2 * BLOCK, 2 * BLOCK), 1)
    rel = BLOCK + (row & (BLOCK - 1)) - col
    band = (rel >= 0) & (rel <= BLOCK)
    band_first = band & (col >= BLOCK)
    @pl.when(pl.program_id(1) == 0)
    def _():
        for g in range(N_GROUPS_A):
            halves = []
            for hh in range(2):
                x = jnp.broadcast_to(rev_ref[0, g * 2 + hh:g * 2 + hh + 1, :], (BLOCK, 4 * BLOCK))
                halves.append(pltpu.roll(x, 0, 1, stride=1, stride_axis=0)[:, BLOCK:3 * BLOCK])
            toep = jnp.concatenate(halves, axis=0)
            bias_ref[2 * g] = jnp.where(band, toep, NEG_INF)
            bias_ref[2 * g + 1] = jnp.where(band_first, toep, NEG_INF)

    eye = jnp.where(lax.broadcasted_iota(jnp.int32, (LANES, LANES), 0) == lane, 1.0, 0.0).astype(BF16)
    for g in range(N_GROUPS_A):
        kt_st[g, :, 0:BLOCK] = jnp.zeros((LANES, BLOCK), BF16)
        vx_st[g, 0:BLOCK, :] = jnp.zeros((BLOCK, 2 * LANES), BF16)
        vx_st[g, :, LANES:] = jnp.ones((BLOCK + seq, LANES), BF16)

    def scores(g, slot, base, first):
        q = q_st[g, base:base + BLOCK, :]
        zero = jnp.zeros_like(q)
        qz = jnp.concatenate([jnp.where(lo, q, zero), jnp.where(lo, zero, q)], axis=0)
        s = jnp.dot(qz, kt_st[g, :, base - BLOCK:base + BLOCK], preferred_element_type=F32)
        s_ref[slot] = s + bias_ref[2 * g + (1 if first else 0)]

    def softmax_group(slot, rg):
        rows = slice(rg * BF16_ROWS, (rg + 1) * BF16_ROWS)
        m = jnp.max(s_ref[slot, rows, :], axis=-1, keepdims=True)
        p_ref[slot, rows, :] = jnp.exp(s_ref[slot, rows, :] - m).astype(BF16)
        mrow_ref[slot, rows, :] = jnp.broadcast_to(m, (BF16_ROWS, LANES))

    def finish(g, slot, base):
        acc = jnp.dot(p_ref[slot], vx_st[g, base - BLOCK:base + BLOCK, :], preferred_element_type=F32)
        mrow = mrow_ref[slot]
        return (jnp.where(lo, mrow[:BLOCK], mrow[BLOCK:]),
                jnp.where(lo, acc[:BLOCK, LANES:], acc[BLOCK:, LANES:]),
                jnp.where(lo, acc[:BLOCK, :LANES], acc[BLOCK:, :LANES]))

    def merge(rows, last):
        states = [(m_ref[g, rows, :], l_ref[g, rows, :], acc_ref[g, rows, :]) for g in dilated] + [last]
        mx = functools.reduce(jnp.maximum, [m for m, _, _ in states])
        num = jnp.zeros((BLOCK, LANES), F32)
        den = jnp.zeros((BLOCK, LANES), F32)
        for m, l, acc in states:
            w = jnp.exp(m - mx)
            num = num + w * acc
            den = den + w * l
        o_ref[0, rows, :] = (num / den).astype(o_ref.dtype)

    dilated = [g for g in range(N_GROUPS_A) if DIL_PATTERNS[g][1] > 1]

    n_rg = 2 * BLOCK // BF16_ROWS
    blocks = []
    for g in sorted(range(N_GROUPS_A), key=lambda g: -DIL_PATTERNS[g][1]):
        d = DIL_PATTERNS[g][1]
        sub_len = seq // d
        q_ref, k_ref, v_ref = qkv[g]
        for r in range(d):
            src = pl.ds(r, sub_len, stride=d) if d > 1 else pl.ds(0, seq)
            dst = slice(BLOCK + r * sub_len, BLOCK + (r + 1) * sub_len)
            q_st[g, dst, :] = (q_ref[0, src, :] * SCALE).astype(BF16)
            kt_st[g, :, dst] = lax.dot_general(eye, k_ref[0, src, :].astype(BF16), (((1,), (1,)), ((), ())),
                                               preferred_element_type=F32).astype(BF16)
            vx_st[g, dst, :LANES] = v_ref[0, src, :].astype(BF16)
            for n in range(sub_len // BLOCK):
                base = BLOCK + r * sub_len + n * BLOCK
                out_rows = pl.ds(n * BLOCK * d + r, BLOCK, stride=d) if d > 1 else pl.ds(n * BLOCK, BLOCK)
                blocks.append((g, base, n == 0, out_rows))

    nslot = s_ref.shape[0]
    for t in range(len(blocks) + 2):
        if t < len(blocks):
            g, base, first, _ = blocks[t]
            scores(g, t % nslot, base, first)
        for rg in range(n_rg):
            if 1 <= t <= len(blocks):
                softmax_group((t - 1) % nslot, rg)
            if rg == n_rg // 2 and 2 <= t:
                g, base, _, out_rows = blocks[t - 2]
                state = finish(g, (t - 2) % nslot, base)
                if g in dilated:
                    m_ref[g, out_rows, :], l_ref[g, out_rows, :], acc_ref[g, out_rows, :] = state
                else:
                    merge(out_rows, state)


def _mixer_a(proj_a, rev_a, weights):
    B, S, _ = proj_a.shape
    npair = OUT_WIDTH_A // LANES
    grid = (npair, B)

    def col(which, g):
        base = (which * WIDTH_A + g * OUT_WIDTH_A) // LANES
        return pl.BlockSpec((1, S, LANES), lambda hp, b: (b, 0, base + hp))

    cast_in, cast_out, cast_shapes = _cast_specs(weights, grid)
    in_specs = [col(which, g) for g in range(N_GROUPS_A) for which in range(3)]
    in_specs.append(pl.BlockSpec((1, 2 * N_GROUPS_A, 4 * BLOCK), lambda hp, b: (hp, 0, 0)))
    return pl.pallas_call(
        functools.partial(_mixer_a_kernel, seq=S, n_cast=len(weights)),
        grid=grid,
        in_specs=in_specs + cast_in,
        out_specs=[pl.BlockSpec((1, S, LANES), lambda hp, b: (b, 0, hp))] + cast_out,
        out_shape=[jax.ShapeDtypeStruct((B, S, OUT_WIDTH_A), BF16)] + cast_shapes,
        scratch_shapes=[pltpu.VMEM((2 * N_GROUPS_A, 2 * BLOCK, 2 * BLOCK), F32),
                        pltpu.VMEM((N_GROUPS_A, BLOCK + S, LANES), BF16),
                        pltpu.VMEM((N_GROUPS_A, LANES, BLOCK + S), BF16),
                        pltpu.VMEM((N_GROUPS_A, BLOCK + S, 2 * LANES), BF16),
                        pltpu.VMEM((4, 2 * BLOCK, 2 * BLOCK), F32),
                        pltpu.VMEM((4, 2 * BLOCK, 2 * BLOCK), BF16),
                        pltpu.VMEM((4, 2 * BLOCK, LANES), F32),
                        pltpu.VMEM((N_GROUPS_A, S, LANES), F32),
                        pltpu.VMEM((N_GROUPS_A, S, LANES), F32),
                        pltpu.VMEM((N_GROUPS_A, S, LANES), F32)],
        compiler_params=_params(("arbitrary", "arbitrary")),
        name="mixer_a",
    )(*([proj_a] * 9), rev_a, *weights)


def _mixer_b_kernel(*refs, tq, seq, lam_init, n_cast):
    nq = seq // tq
    q_ref, k_ref, v_ref, rev_ref, lam_ref, g_ref = refs[:6]
    o_ref = refs[6 + n_cast]
    toep_ref, vx_ref = refs[7 + 2 * n_cast:9 + 2 * n_cast]
    s_refs = refs[9 + 2 * n_cast:9 + 2 * n_cast + nq]
    p_refs = refs[9 + 2 * n_cast + nq:]
    _cast_blocks(refs[6:6 + n_cast], refs[7 + n_cast:7 + 2 * n_cast])
    rg_rows = BF16_ROWS
    n_rg = 2 * tq // rg_rows

    x = jnp.broadcast_to(rev_ref[0] * LOG2E, (tq, seq + tq))
    rolled = pltpu.roll(x, 0, 1, stride=1, stride_axis=0)
    toep_ref[:, :seq] = rolled[:, :seq]
    row = lax.broadcasted_iota(jnp.int32, (tq, tq), 0)
    col = lax.broadcasted_iota(jnp.int32, (tq, tq), 1)
    toep_ref[:, seq:] = jnp.where(col <= row, rolled[:, seq:], NEG_INF)

    vx_ref[:, :LANES] = v_ref[0]
    vx_ref[:, LANES:] = jnp.ones((seq, LANES), BF16)

    lane = lax.broadcasted_iota(jnp.int32, (tq, LANES), 1)
    lo = lane < HEAD_DIM
    lv = lam_ref[...]
    lam = (jnp.exp(jnp.sum(lv[0:1] * lv[1:2], axis=-1, keepdims=True))
           - jnp.exp(jnp.sum(lv[2:3] * lv[3:4], axis=-1, keepdims=True)) + lam_init)

    def score_chunks(qi):
        q = q_ref[0, qi * tq:(qi + 1) * tq, :]
        zero = jnp.zeros_like(q)
        qz = jnp.concatenate([jnp.where(lo, q, zero), jnp.where(lo, zero, q)], axis=0)

        def chunk(c):
            kc = k_ref[0, c * tq:(c + 1) * tq, :]
            s_refs[qi][:, c * tq:(c + 1) * tq] = lax.dot_general(
                qz, kc, (((1,), (1,)), ((), ())), preferred_element_type=F32)

        return [functools.partial(chunk, c) for c in range(qi + 1)]

    def softmax_group(qi, rg):
        rows = slice(rg * rg_rows, (rg + 1) * rg_rows)
        brow = (rg * rg_rows) % tq
        bias = toep_ref[brow:brow + rg_rows, (nq - qi) * tq:(nq + 1) * tq]
        t = s_refs[qi][rows, :] + bias
        m = jnp.max(t, axis=-1, keepdims=True)
        p_refs[qi][rows, :] = jnp.exp2(t - m).astype(BF16)

    halves = {}

    def value_matmul(qi, half):
        acc = jnp.dot(p_refs[qi][half * tq:(half + 1) * tq, :], vx_ref[:(qi + 1) * tq, :],
                      preferred_element_type=F32)
        halves[qi, half] = acc[:, :LANES] / acc[:, LANES:]

    def finish(qi):
        y = halves[qi, 0] - lam * halves[qi, 1]
        y = _rms(y, g_ref[...]) * (1.0 - lam_init)
        o_ref[0, qi * tq:(qi + 1) * tq, :] = y.astype(o_ref.dtype)

    order = list(range(nq - 1, -1, -1))
    for t in range(nq + 2):
        scores_t = score_chunks(order[t]) if t < nq else []
        softmax_t, tail_t = [], []
        if 1 <= t <= nq:
            qi = order[t - 1]
            groups = [functools.partial(softmax_group, qi, rg) for rg in range(n_rg)]
            softmax_t = groups[:n_rg // 2] + [functools.partial(value_matmul, qi, 0)] + groups[n_rg // 2:]
        if t >= 2:
            qi = order[t - 2]
            tail_t = [functools.partial(value_matmul, qi, 1), functools.partial(finish, qi)]
        for emit in _interleave([scores_t, softmax_t, tail_t]):
            emit()


def _mixer_b(proj, rev_b, lam_vecs, subln_g, lam_init, weights, tq=256):
    B, S, _ = proj.shape
    H = N_HEADS_B
    nq = S // tq
    grid = (B, H)
    kern = functools.partial(_mixer_b_kernel, tq=tq, seq=S, lam_init=lam_init, n_cast=len(weights))

    def col(off):
        return pl.BlockSpec((1, S, LANES), lambda b, h: (b, 0, off // LANES + h))

    cast_in, cast_out, cast_shapes = _cast_specs(weights, grid)
    return pl.pallas_call(
        kern,
        grid=grid,
        in_specs=[col(OFF_QB), col(OFF_KB), col(OFF_VB),
                  pl.BlockSpec((1, 1, S + tq), lambda b, h: (h, 0, 0)),
                  pl.BlockSpec((4, HEAD_DIM), lambda b, h: (0, 0)),
                  pl.BlockSpec((1, 2 * HEAD_DIM), lambda b, h: (0, 0))] + cast_in,
        out_specs=[pl.BlockSpec((1, S, LANES), lambda b, h: (b, 0, h))] + cast_out,
        out_shape=[jax.ShapeDtypeStruct((B, S, WIDTH_B), BF16)] + cast_shapes,
        scratch_shapes=([pltpu.VMEM((tq, S + tq), F32), pltpu.VMEM((S, 2 * LANES), BF16)]
                        + [pltpu.VMEM((2 * tq, (i + 1) * tq), F32) for i in range(nq)]
                        + [pltpu.VMEM((2 * tq, (i + 1) * tq), BF16) for i in range(nq)]),
        compiler_params=_params(("arbitrary", "arbitrary")),
        name="mixer_b",
    )(proj, proj, proj, rev_b, lam_vecs, subln_g, *weights)


def _attn_out_kernel(ya_ref, yb_ref, ga_ref, gb_ref, x_ref, wa_ref, wb_ref, wo_ref, g_ref, x1_ref, h_ref):
    pa = jnp.dot(ya_ref[...], wa_ref[...], preferred_element_type=F32)
    pb = jnp.dot(yb_ref[...], wb_ref[...], preferred_element_type=F32)
    ga = jax.nn.sigmoid(ga_ref[...].astype(F32))
    gb = jax.nn.sigmoid(gb_ref[...].astype(F32))
    merged = (ga * pa + gb * pb).astype(BF16)
    x1_ref[...] = x_ref[...] + jnp.dot(merged, wo_ref[...], preferred_element_type=F32)
    h_ref[...] = _rms(x1_ref[...], g_ref[...]).astype(BF16)


def _attn_out(ya, yb, proj_r, x2, wa, wb, wo, g, tm=512):
    T, D = x2.shape

    def resident(shape):
        return pl.BlockSpec(shape, lambda i: (0, 0), pipeline_mode=pl.Buffered(1))

    return pl.pallas_call(
        _attn_out_kernel,
        grid=(T // tm,),
        in_specs=[pl.BlockSpec((tm, OUT_WIDTH_A), lambda i: (i, 0)),
                  pl.BlockSpec((tm, WIDTH_B), lambda i: (i, 0)),
                  pl.BlockSpec((tm, D), lambda i: (i, OFF_GA // D)),
                  pl.BlockSpec((tm, D), lambda i: (i, OFF_GB // D)),
                  pl.BlockSpec((tm, D), lambda i: (i, 0)),
                  resident((OUT_WIDTH_A, D)), resident((WIDTH_B, D)), resident((D, D)), resident((1, D))],
        out_specs=[pl.BlockSpec((tm, D), lambda i: (i, 0)), pl.BlockSpec((tm, D), lambda i: (i, 0))],
        out_shape=[jax.ShapeDtypeStruct((T, D), F32), jax.ShapeDtypeStruct((T, D), BF16)],
        compiler_params=_params(("parallel",), vmem=VMEM_LIMIT_ATTN_OUT),
        name="attn_out",
    )(ya, yb, proj_r, proj_r, x2, wa, wb, wo, g)


def _ffn_kernel(x1_hbm, h_ref, wg_ref, wu_ref, wd_ref, gf_ref, o_ref, x1_buf, x1_sem, *, tm):
    i = pl.program_id(0)
    f = pl.program_id(1)
    x1_copy = pltpu.make_async_copy(x1_hbm.at[pl.ds(i * tm, tm), :], x1_buf, x1_sem)

    def down_chunk():
        h = h_ref[...]
        a = jnp.dot(h, wg_ref[...], preferred_element_type=F32)
        b = jnp.dot(h, wu_ref[...], preferred_element_type=F32)
        u = (a * jax.nn.sigmoid(a)) * b
        return jnp.dot(u.astype(BF16), wd_ref[...], preferred_element_type=F32)

    @pl.when(f == 0)
    def _():
        x1_copy.start()
        o_ref[...] = down_chunk()

    @pl.when(f > 0)
    def _():
        o_ref[...] += down_chunk()

    @pl.when(f == pl.num_programs(1) - 1)
    def _():
        x1_copy.wait()
        o_ref[...] = _rms(x1_buf[...] + o_ref[...], gf_ref[...])


def _ffn(x1, h, wg, wu, wd, gf, tm=1024, tf=512):
    T, D = x1.shape
    F = wg.shape[1]
    return pl.pallas_call(
        functools.partial(_ffn_kernel, tm=tm),
        grid=(T // tm, F // tf),
        in_specs=[pl.BlockSpec(memory_space=pl.ANY),
                  pl.BlockSpec((tm, D), lambda i, f: (i, 0)),
                  pl.BlockSpec((D, tf), lambda i, f: (0, f)),
                  pl.BlockSpec((D, tf), lambda i, f: (0, f)),
                  pl.BlockSpec((tf, D), lambda i, f: (f, 0)),
                  pl.BlockSpec((1, D), lambda i, f: (0, 0))],
        out_specs=pl.BlockSpec((tm, D), lambda i, f: (i, 0)),
        out_shape=jax.ShapeDtypeStruct((T, D), F32),
        scratch_shapes=[pltpu.VMEM((tm, D), F32), pltpu.SemaphoreType.DMA(())],
        compiler_params=_params(("parallel", "arbitrary"), vmem=VMEM_LIMIT_FFN),
        name="ffn",
    )(x1, h, wg, wu, wd, gf)


def _rev_a_index():
    u = np.arange(4 * BLOCK)
    rel = np.clip(2 * BLOCK - u, 0, None)
    return np.stack([_rel_bucket_np(rel * d) for _, d in DIL_PATTERNS])


def _rev_b_index(seq, tq):
    c = np.arange(seq + tq)
    return _rel_bucket_np(np.clip(seq - c, 0, seq - 1))


def _lookup(table, idx):
    onehot = jnp.asarray(np.asarray(idx)[None, :] == np.arange(NUM_BUCKETS)[:, None])
    return jnp.sum(jnp.where(onehot[:, None, :], table[:, :, None], 0.0), axis=0)


def kernel(x, norm_attn_g, w_in, w_proj_a, w_proj_b, w_out, rel_bias_table, diff_lambda_q1, diff_lambda_k1, diff_lambda_q2, diff_lambda_k2, diff_subln_g, norm_ffn_g, w_ffn_gate, w_ffn_up, w_ffn_down, norm_final_g):
    B, S, D = x.shape
    T = B * S
    depth = w_in.shape[0]
    assert depth == 1, "the final RMSNorm is fused into the FFN epilogue of a single layer"
    table_a = rel_bias_table[:, :N_HEADS_A].astype(F32)
    table_b = rel_bias_table[:, N_HEADS_A:].astype(F32)
    tq = 256

    idx_a = _rev_a_index()
    rev_a = jnp.stack([_lookup(table_a[:, g * HEADS_PER_GROUP_A:(g + 1) * HEADS_PER_GROUP_A], idx_a[g])
                       for g in range(N_GROUPS_A)])
    npair = OUT_WIDTH_A // LANES
    rev_a = jnp.transpose(rev_a.reshape(N_GROUPS_A, npair, 2, 4 * BLOCK), (1, 0, 2, 3))
    rev_a = rev_a.reshape(npair, 2 * N_GROUPS_A, 4 * BLOCK)
    rev_b = _lookup(table_b, _rev_b_index(S, tq))[:, None, :]

    x2 = x.reshape(T, D)
    l = 0
    lam_init = 0.8 - 0.6 * math.exp(-0.3 * l)
    proj_a, proj_r = _in_proj(x2, norm_attn_g[l][None, :], w_in[l])
    ya, wa, wb, wo = _mixer_a(proj_a.reshape(B, S, PROJ_A), rev_a, [w_proj_a[l], w_proj_b[l], w_out[l]])

    lam_vecs = jnp.stack([diff_lambda_q1[l], diff_lambda_k1[l],
                          diff_lambda_q2[l], diff_lambda_k2[l]]).astype(F32)
    yb, wg, wu, wd = _mixer_b(proj_r.reshape(B, S, PROJ_R), rev_b, lam_vecs, diff_subln_g[l][None, :].astype(F32),
                              lam_init, [w_ffn_gate[l], w_ffn_up[l], w_ffn_down[l]], tq=tq)

    x1, h = _attn_out(ya.reshape(T, OUT_WIDTH_A), yb.reshape(T, WIDTH_B), proj_r, x2, wa, wb, wo,
                      norm_ffn_g[l][None, :])
    out = _ffn(x1, h, wg, wu, wd, norm_final_g[None, :])
    return out.reshape(B, S, D)
```

```python
import functools
import math

import numpy as np
import jax
import jax.numpy as jnp
from jax import lax
from jax.experimental import pallas as pl
from jax.experimental.pallas import tpu as pltpu

D_MODEL = 2048
HEAD_DIM = 64
DIL_PATTERNS = ((128, 1), (512, 4), (2048, 16))
N_GROUPS_A = len(DIL_PATTERNS)
HEADS_PER_GROUP_A = 8
N_HEADS_A = N_GROUPS_A * HEADS_PER_GROUP_A
WIDTH_A = N_HEADS_A * HEAD_DIM
OUT_WIDTH_A = HEADS_PER_GROUP_A * HEAD_DIM
BLOCK = 128
N_HEADS_B = D_MODEL // (2 * HEAD_DIM)
WIDTH_B = N_HEADS_B * 2 * HEAD_DIM
NUM_BUCKETS = 32
MAX_DISTANCE = 2048
PROJ_A = 3 * WIDTH_A
PROJ_R = 3 * WIDTH_B + 2 * D_MODEL
D_IN = PROJ_A + PROJ_R
NORM_EPS = 1e-6
NEG_INF = -1e30
SCALE = HEAD_DIM ** -0.5
LOG2E = math.log2(math.e)

OFF_QB = 0
OFF_KB = WIDTH_B
OFF_VB = 2 * WIDTH_B
OFF_GA = 3 * WIDTH_B
OFF_GB = OFF_GA + D_MODEL

LANES = 128
VMEM_LIMIT = 56 * 1024 * 1024
VMEM_LIMIT_ATTN_OUT = 58 * 1024 * 1024
VMEM_LIMIT_FFN = 60 * 1024 * 1024
FFN_EPILOGUE_SLABS = 4

BF16 = jnp.bfloat16
F32 = jnp.float32


def _rel_bucket_np(dist):
    n = np.maximum(dist, 0)
    max_exact = NUM_BUCKETS // 2
    nf = np.maximum(n, 1).astype(np.float32)
    large = max_exact + (np.log(nf / np.float32(max_exact)) / np.float32(math.log(MAX_DISTANCE / max_exact))
                         * np.float32(NUM_BUCKETS - max_exact)).astype(np.int32)
    large = np.minimum(large, NUM_BUCKETS - 1)
    return np.where(n < max_exact, n, large).astype(np.int32)


def _rms(x, g):
    ms = jnp.mean(x * x, axis=-1, keepdims=True)
    return x * lax.rsqrt(ms + NORM_EPS) * g


def _params(sem, vmem=VMEM_LIMIT):
    return pltpu.CompilerParams(dimension_semantics=sem, vmem_limit_bytes=vmem)


BF16_ROWS = 16


def _cast_specs(weights, grid):
    nsteps = math.prod(grid)
    in_specs, out_specs, out_shapes = [], [], []
    for w in weights:
        rows, cols = w.shape
        blk = next(r for r in range(BF16_ROWS, rows + 1, BF16_ROWS)
                   if rows % r == 0 and nsteps % (rows // r) == 0 and rows // r <= nsteps)
        per = nsteps // (rows // blk)

        def index(*ids, per=per):
            step = 0
            for i, n in zip(ids, grid):
                step = step * n + i
            return (step // per, 0)

        in_specs.append(pl.BlockSpec((blk, cols), index))
        out_specs.append(pl.BlockSpec((blk, cols), index))
        out_shapes.append(jax.ShapeDtypeStruct((rows, cols), BF16))
    return in_specs, out_specs, out_shapes


def _interleave(lists):
    keyed = [((i + 0.5) / len(items), n, i, item) for n, items in enumerate(lists) for i, item in enumerate(items)]
    return [item for _, _, _, item in sorted(keyed, key=lambda k: k[:3])]


def _cast_blocks(in_refs, out_refs):
    for i_ref, o_ref in zip(in_refs, out_refs):
        o_ref[...] = i_ref[...].astype(BF16)


def _in_proj_kernel(x_hbm, g_ref, w_ref, oa_ref, or_ref, h_ref, x_buf, x_sem, *, na, nqb, tm):
    i = pl.program_id(0)
    j = pl.program_id(1)

    def x_copy(tile):
        return pltpu.make_async_copy(x_hbm.at[pl.ds(tile * tm, tm), :], x_buf, x_sem)

    @pl.when(j == 0)
    def _():
        @pl.when(i == 0)
        def _():
            x_copy(0).start()

        x_copy(i).wait()
        h = _rms(x_buf[...], g_ref[...]).astype(BF16)
        h_ref[...] = h
        oa_ref[...] = jnp.dot(h, w_ref[...].astype(BF16), preferred_element_type=F32)

        @pl.when(i + 1 < pl.num_programs(0))
        def _():
            x_copy(i + 1).start()

    @pl.when((j > 0) & (j < na))
    def _():
        oa_ref[...] = jnp.dot(h_ref[...], w_ref[...].astype(BF16), preferred_element_type=F32)

    @pl.when(j >= na)
    def _():
        scale = jnp.where(j < na + nqb, jnp.float32(SCALE * LOG2E), jnp.float32(1.0))
        res = jnp.dot(h_ref[...], w_ref[...].astype(BF16), preferred_element_type=F32)
        or_ref[...] = (res * scale).astype(BF16)


def _in_proj(x2, g, w, tm=2048, tn=512):
    T, D = x2.shape
    na = PROJ_A // tn
    assert OFF_QB == 0 and WIDTH_B % tn == 0
    return pl.pallas_call(
        functools.partial(_in_proj_kernel, na=na, nqb=WIDTH_B // tn, tm=tm),
        grid=(T // tm, D_IN // tn),
        in_specs=[pl.BlockSpec(memory_space=pl.ANY),
                  pl.BlockSpec((1, D), lambda i, j: (0, 0)),
                  pl.BlockSpec((D, tn), lambda i, j: (0, j))],
        out_specs=[pl.BlockSpec((tm, tn), lambda i, j: (i, jnp.minimum(j, na - 1))),
                   pl.BlockSpec((tm, tn), lambda i, j: (i, jnp.maximum(j - na, 0)))],
        out_shape=[jax.ShapeDtypeStruct((T, PROJ_A), F32), jax.ShapeDtypeStruct((T, PROJ_R), BF16)],
        scratch_shapes=[pltpu.VMEM((tm, D), BF16), pltpu.VMEM((tm, D), F32), pltpu.SemaphoreType.DMA(())],
        compiler_params=_params(("arbitrary", "arbitrary")),
        name="in_proj",
    )(x2, g, w)


def _mixer_a_kernel(*refs, seq, n_cast):
    qkv = (refs[0:3], refs[3:6], refs[6:9])
    rev_ref = refs[9]
    o_ref = refs[10 + n_cast]
    (bias_ref, q_st, kt_st, vx_st, s_ref, p_ref, mrow_ref, m_ref, l_ref, acc_ref) = refs[11 + 2 * n_cast:]
    _cast_blocks(refs[10:10 + n_cast], refs[11 + n_cast:11 + 2 * n_cast])
    lane = lax.broadcasted_iota(jnp.int32, (BLOCK, LANES), 1)
    lo = lane < HEAD_DIM

    row = lax.broadcasted_iota(jnp.int32, (2 * BLOCK, 2 * BLOCK), 0)
    col = lax.broadcasted_iota(jnp.int32, (2 * BLOCK, 2 * BLOCK), 1)
    rel = BLOCK + (row & (BLOCK - 1)) - col
    band = (rel >= 0) & (rel <= BLOCK)
    band_first = band & (col >= BLOCK)
    @pl.when(pl.program_id(1) == 0)
    def _():
        for g in range(N_GROUPS_A):
            halves = []
            for hh in range(2):
                x = jnp.broadcast_to(rev_ref[0, g * 2 + hh:g * 2 + hh + 1, :], (BLOCK, 4 * BLOCK))
                halves.append(pltpu.roll(x, 0, 1, stride=1, stride_axis=0)[:, BLOCK:3 * BLOCK])
            toep = jnp.concatenate(halves, axis=0)
            bias_ref[2 * g] = jnp.where(band, toep, NEG_INF)
            bias_ref[2 * g + 1] = jnp.where(band_first, toep, NEG_INF)

    eye = jnp.where(lax.broadcasted_iota(jnp.int32, (LANES, LANES), 0) == lane, 1.0, 0.0).astype(BF16)
    for g in range(N_GROUPS_A):
        kt_st[g, :, 0:BLOCK] = jnp.zeros((LANES, BLOCK), BF16)
        vx_st[g, 0:BLOCK, :] = jnp.zeros((BLOCK, 2 * LANES), BF16)
        vx_st[g, :, LANES:] = jnp.ones((BLOCK + seq, LANES), BF16)

    def scores(g, slot, base, first):
        q = q_st[g, base:base + BLOCK, :]
        zero = jnp.zeros_like(q)
        qz = jnp.concatenate([jnp.where(lo, q, zero), jnp.where(lo, zero, q)], axis=0)
        s = jnp.dot(qz, kt_st[g, :, base - BLOCK:base + BLOCK], preferred_element_type=F32)
        s_ref[slot] = s + bias_ref[2 * g + (1 if first else 0)]

    def softmax_group(slot, rg):
        rows = slice(rg * BF16_ROWS, (rg + 1) * BF16_ROWS)
        m = jnp.max(s_ref[slot, rows, :], axis=-1, keepdims=True)
        p_ref[slot, rows, :] = jnp.exp(s_ref[slot, rows, :] - m).astype(BF16)
        mrow_ref[slot, rows, :] = jnp.broadcast_to(m, (BF16_ROWS, LANES))

    def finish(g, slot, base):
        acc = jnp.dot(p_ref[slot], vx_st[g, base - BLOCK:base + BLOCK, :], preferred_element_type=F32)
        mrow = mrow_ref[slot]
        return (jnp.where(lo, mrow[:BLOCK], mrow[BLOCK:]),
                jnp.where(lo, acc[:BLOCK, LANES:], acc[BLOCK:, LANES:]),
                jnp.where(lo, acc[:BLOCK, :LANES], acc[BLOCK:, :LANES]))

    def merge(rows, last):
        states = [(m_ref[g, rows, :], l_ref[g, rows, :], acc_ref[g, rows, :]) for g in dilated] + [last]
        mx = functools.reduce(jnp.maximum, [m for m, _, _ in states])
        num = jnp.zeros((BLOCK, LANES), F32)
        den = jnp.zeros((BLOCK, LANES), F32)
        for m, l, acc in states:
            w = jnp.exp(m - mx)
            num = num + w * acc
            den = den + w * l
        o_ref[0, rows, :] = (num / den).astype(o_ref.dtype)

    dilated = [g for g in range(N_GROUPS_A) if DIL_PATTERNS[g][1] > 1]

    n_rg = 2 * BLOCK // BF16_ROWS
    blocks = []
    for g in sorted(range(N_GROUPS_A), key=lambda g: -DIL_PATTERNS[g][1]):
        d = DIL_PATTERNS[g][1]
        sub_len = seq // d
        q_ref, k_ref, v_ref = qkv[g]
        for r in range(d):
            src = pl.ds(r, sub_len, stride=d) if d > 1 else pl.ds(0, seq)
            dst = slice(BLOCK + r * sub_len, BLOCK + (r + 1) * sub_len)
            q_st[g, dst, :] = (q_ref[0, src, :] * SCALE).astype(BF16)
            kt_st[g, :, dst] = lax.dot_general(eye, k_ref[0, src, :].astype(BF16), (((1,), (1,)), ((), ())),
                                               preferred_element_type=F32).astype(BF16)
            vx_st[g, dst, :LANES] = v_ref[0, src, :].astype(BF16)
            for n in range(sub_len // BLOCK):
                base = BLOCK + r * sub_len + n * BLOCK
                out_rows = pl.ds(n * BLOCK * d + r, BLOCK, stride=d) if d > 1 else pl.ds(n * BLOCK, BLOCK)
                blocks.append((g, base, n == 0, out_rows))

    nslot = s_ref.shape[0]
    for t in range(len(blocks) + 2):
        if t < len(blocks):
            g, base, first, _ = blocks[t]
            scores(g, t % nslot, base, first)
        for rg in range(n_rg):
            if 1 <= t <= len(blocks):
                softmax_group((t - 1) % nslot, rg)
            if rg == n_rg // 2 and 2 <= t:
                g, base, _, out_rows = blocks[t - 2]
                state = finish(g, (t - 2) % nslot, base)
                if g in dilated:
                    m_ref[g, out_rows, :], l_ref[g, out_rows, :], acc_ref[g, out_rows, :] = state
                else:
                    merge(out_rows, state)


def _mixer_a(proj_a, rev_a, weights):
    B, S, _ = proj_a.shape
    npair = OUT_WIDTH_A // LANES
    grid = (npair, B)

    def col(which, g):
        base = (which * WIDTH_A + g * OUT_WIDTH_A) // LANES
        return pl.BlockSpec((1, S, LANES), lambda hp, b: (b, 0, base + hp))

    cast_in, cast_out, cast_shapes = _cast_specs(weights, grid)
    in_specs = [col(which, g) for g in range(N_GROUPS_A) for which in range(3)]
    in_specs.append(pl.BlockSpec((1, 2 * N_GROUPS_A, 4 * BLOCK), lambda hp, b: (hp, 0, 0)))
    return pl.pallas_call(
        functools.partial(_mixer_a_kernel, seq=S, n_cast=len(weights)),
        grid=grid,
        in_specs=in_specs + cast_in,
        out_specs=[pl.BlockSpec((1, S, LANES), lambda hp, b: (b, 0, hp))] + cast_out,
        out_shape=[jax.ShapeDtypeStruct((B, S, OUT_WIDTH_A), BF16)] + cast_shapes,
        scratch_shapes=[pltpu.VMEM((2 * N_GROUPS_A, 2 * BLOCK, 2 * BLOCK), F32),
                        pltpu.VMEM((N_GROUPS_A, BLOCK + S, LANES), BF16),
                        pltpu.VMEM((N_GROUPS_A, LANES, BLOCK + S), BF16),
                        pltpu.VMEM((N_GROUPS_A, BLOCK + S, 2 * LANES), BF16),
                        pltpu.VMEM((4, 2 * BLOCK, 2 * BLOCK), F32),
                        pltpu.VMEM((4, 2 * BLOCK, 2 * BLOCK), BF16),
                        pltpu.VMEM((4, 2 * BLOCK, LANES), F32),
                        pltpu.VMEM((N_GROUPS_A, S, LANES), F32),
                        pltpu.VMEM((N_GROUPS_A, S, LANES), F32),
                        pltpu.VMEM((N_GROUPS_A, S, LANES), F32)],
        compiler_params=_params(("arbitrary", "arbitrary")),
        name="mixer_a",
    )(*([proj_a] * 9), rev_a, *weights)


def _mixer_b_kernel(*refs, tq, seq, lam_init, n_cast):
    nq = seq // tq
    q_ref, k_ref, v_ref, rev_ref, lam_ref, g_ref = refs[:6]
    o_ref = refs[6 + n_cast]
    toep_ref, vx_ref = refs[7 + 2 * n_cast:9 + 2 * n_cast]
    s_refs = refs[9 + 2 * n_cast:9 + 2 * n_cast + nq]
    p_refs = refs[9 + 2 * n_cast + nq:]
    _cast_blocks(refs[6:6 + n_cast], refs[7 + n_cast:7 + 2 * n_cast])
    rg_rows = BF16_ROWS
    n_rg = 2 * tq // rg_rows

    x = jnp.broadcast_to(rev_ref[0] * LOG2E, (tq, seq + tq))
    rolled = pltpu.roll(x, 0, 1, stride=1, stride_axis=0)
    toep_ref[:, :seq] = rolled[:, :seq]
    row = lax.broadcasted_iota(jnp.int32, (tq, tq), 0)
    col = lax.broadcasted_iota(jnp.int32, (tq, tq), 1)
    toep_ref[:, seq:] = jnp.where(col <= row, rolled[:, seq:], NEG_INF)

    vx_ref[:, :LANES] = v_ref[0]
    vx_ref[:, LANES:] = jnp.ones((seq, LANES), BF16)

    lane = lax.broadcasted_iota(jnp.int32, (tq, LANES), 1)
    lo = lane < HEAD_DIM
    lv = lam_ref[...]
    lam = (jnp.exp(jnp.sum(lv[0:1] * lv[1:2], axis=-1, keepdims=True))
           - jnp.exp(jnp.sum(lv[2:3] * lv[3:4], axis=-1, keepdims=True)) + lam_init)

    def score_chunks(qi):
        q = q_ref[0, qi * tq:(qi + 1) * tq, :]
        zero = jnp.zeros_like(q)
        qz = jnp.concatenate([jnp.where(lo, q, zero), jnp.where(lo, zero, q)], axis=0)

        def chunk(c):
            kc = k_ref[0, c * tq:(c + 1) * tq, :]
            s_refs[qi][:, c * tq:(c + 1) * tq] = lax.dot_general(
                qz, kc, (((1,), (1,)), ((), ())), preferred_element_type=F32)

        return [functools.partial(chunk, c) for c in range(qi + 1)]

    def softmax_group(qi, rg):
        rows = slice(rg * rg_rows, (rg + 1) * rg_rows)
        brow = (rg * rg_rows) % tq
        bias = toep_ref[brow:brow + rg_rows, (nq - qi) * tq:(nq + 1) * tq]
        t = s_refs[qi][rows, :] + bias
        m = jnp.max(t, axis=-1, keepdims=True)
        p_refs[qi][rows, :] = jnp.exp2(t - m).astype(BF16)

    halves = {}

    def value_matmul(qi, half):
        acc = jnp.dot(p_refs[qi][half * tq:(half + 1) * tq, :], vx_ref[:(qi + 1) * tq, :],
                      preferred_element_type=F32)
        halves[qi, half] = acc[:, :LANES] / acc[:, LANES:]

    def finish(qi):
        y = halves[qi, 0] - lam * halves[qi, 1]
        y = _rms(y, g_ref[...]) * (1.0 - lam_init)
        o_ref[0, qi * tq:(qi + 1) * tq, :] = y.astype(o_ref.dtype)

    order = list(range(nq - 1, -1, -1))
    for t in range(nq + 2):
        scores_t = score_chunks(order[t]) if t < nq else []
        softmax_t, tail_t = [], []
        if 1 <= t <= nq:
            qi = order[t - 1]
            groups = [functools.partial(softmax_group, qi, rg) for rg in range(n_rg)]
            softmax_t = groups[:n_rg // 2] + [functools.partial(value_matmul, qi, 0)] + groups[n_rg // 2:]
        if t >= 2:
            qi = order[t - 2]
            tail_t = [functools.partial(value_matmul, qi, 1), functools.partial(finish, qi)]
        for emit in _interleave([scores_t, softmax_t, tail_t]):
            emit()


def _mixer_b(proj, rev_b, lam_vecs, subln_g, lam_init, weights, tq=256):
    B, S, _ = proj.shape
    H = N_HEADS_B
    nq = S // tq
    grid = (B, H)
    kern = functools.partial(_mixer_b_kernel, tq=tq, seq=S, lam_init=lam_init, n_cast=len(weights))

    def col(off):
        return pl.BlockSpec((1, S, LANES), lambda b, h: (b, 0, off // LANES + h))

    cast_in, cast_out, cast_shapes = _cast_specs(weights, grid)
    return pl.pallas_call(
        kern,
        grid=grid,
        in_specs=[col(OFF_QB), col(OFF_KB), col(OFF_VB),
                  pl.BlockSpec((1, 1, S + tq), lambda b, h: (h, 0, 0)),
                  pl.BlockSpec((4, HEAD_DIM), lambda b, h: (0, 0)),
                  pl.BlockSpec((1, 2 * HEAD_DIM), lambda b, h: (0, 0))] + cast_in,
        out_specs=[pl.BlockSpec((1, S, LANES), lambda b, h: (b, 0, h))] + cast_out,
        out_shape=[jax.ShapeDtypeStruct((B, S, WIDTH_B), BF16)] + cast_shapes,
        scratch_shapes=([pltpu.VMEM((tq, S + tq), F32), pltpu.VMEM((S, 2 * LANES), BF16)]
                        + [pltpu.VMEM((2 * tq, (i + 1) * tq), F32) for i in range(nq)]
                        + [pltpu.VMEM((2 * tq, (i + 1) * tq), BF16) for i in range(nq)]),
        compiler_params=_params(("arbitrary", "arbitrary")),
        name="mixer_b",
    )(proj, proj, proj, rev_b, lam_vecs, subln_g, *weights)


def _attn_out_kernel(ya_ref, yb_ref, ga_ref, gb_ref, x_ref, wa_ref, wb_ref, wo_ref, g_ref, x1_ref, h_ref):
    pa = jnp.dot(ya_ref[...], wa_ref[...], preferred_element_type=F32)
    pb = jnp.dot(yb_ref[...], wb_ref[...], preferred_element_type=F32)
    ga = jax.nn.sigmoid(ga_ref[...].astype(F32))
    gb = jax.nn.sigmoid(gb_ref[...].astype(F32))
    merged = (ga * pa + gb * pb).astype(BF16)
    x1_ref[...] = x_ref[...] + jnp.dot(merged, wo_ref[...], preferred_element_type=F32)
    h_ref[...] = _rms(x1_ref[...], g_ref[...]).astype(BF16)


def _attn_out(ya, yb, proj_r, x2, wa, wb, wo, g, tm=512):
    T, D = x2.shape

    def resident(shape):
        return pl.BlockSpec(shape, lambda i: (0, 0), pipeline_mode=pl.Buffered(1))

    return pl.pallas_call(
        _attn_out_kernel,
        grid=(T // tm,),
        in_specs=[pl.BlockSpec((tm, OUT_WIDTH_A), lambda i: (i, 0)),
                  pl.BlockSpec((tm, WIDTH_B), lambda i: (i, 0)),
                  pl.BlockSpec((tm, D), lambda i: (i, OFF_GA // D)),
                  pl.BlockSpec((tm, D), lambda i: (i, OFF_GB // D)),
                  pl.BlockSpec((tm, D), lambda i: (i, 0)),
                  resident((OUT_WIDTH_A, D)), resident((WIDTH_B, D)), resident((D, D)), resident((1, D))],
        out_specs=[pl.BlockSpec((tm, D), lambda i: (i, 0)), pl.BlockSpec((tm, D), lambda i: (i, 0))],
        out_shape=[jax.ShapeDtypeStruct((T, D), F32), jax.ShapeDtypeStruct((T, D), BF16)],
        compiler_params=_params(("parallel",), vmem=VMEM_LIMIT_ATTN_OUT),
        name="attn_out",
    )(ya, yb, proj_r, proj_r, x2, wa, wb, wo, g)


def _ffn_kernel(x1_hbm, h_ref, wg_ref, wu_ref, wd_ref, gf_ref, o_ref, x1_buf, x1_sem, *, tm):
    i = pl.program_id(0)
    f = pl.program_id(1)
    x1_copy = pltpu.make_async_copy(x1_hbm.at[pl.ds(i * tm, tm), :], x1_buf, x1_sem)

    last = pl.num_programs(1) - 1

    def gated():
        h = h_ref[...]
        a = jnp.dot(h, wg_ref[...], preferred_element_type=F32)
        b = jnp.dot(h, wu_ref[...], preferred_element_type=F32)
        return ((a * jax.nn.sigmoid(a)) * b).astype(BF16)

    @pl.when(f == 0)
    def _():
        x1_copy.start()
        o_ref[...] = jnp.dot(gated(), wd_ref[...], preferred_element_type=F32)

    @pl.when((f > 0) & (f < last))
    def _():
        o_ref[...] += jnp.dot(gated(), wd_ref[...], preferred_element_type=F32)

    @pl.when(f == last)
    def _():
        x1_copy.wait()
        u = gated()
        slab = tm // FFN_EPILOGUE_SLABS
        for r in range(FFN_EPILOGUE_SLABS):
            rows = slice(r * slab, (r + 1) * slab)
            y = o_ref[rows, :] + jnp.dot(u[rows], wd_ref[...], preferred_element_type=F32)
            o_ref[rows, :] = _rms(x1_buf[rows, :] + y, gf_ref[...])


def _ffn(x1, h, wg, wu, wd, gf, tm=1024, tf=512):
    T, D = x1.shape
    F = wg.shape[1]
    return pl.pallas_call(
        functools.partial(_ffn_kernel, tm=tm),
        grid=(T // tm, F // tf),
        in_specs=[pl.BlockSpec(memory_space=pl.ANY),
                  pl.BlockSpec((tm, D), lambda i, f: (i, 0)),
                  pl.BlockSpec((D, tf), lambda i, f: (0, f)),
                  pl.BlockSpec((D, tf), lambda i, f: (0, f)),
                  pl.BlockSpec((tf, D), lambda i, f: (f, 0)),
                  pl.BlockSpec((1, D), lambda i, f: (0, 0))],
        out_specs=pl.BlockSpec((tm, D), lambda i, f: (i, 0)),
        out_shape=jax.ShapeDtypeStruct((T, D), F32),
        scratch_shapes=[pltpu.VMEM((tm, D), F32), pltpu.SemaphoreType.DMA(())],
        compiler_params=_params(("parallel", "arbitrary"), vmem=VMEM_LIMIT_FFN),
        name="ffn",
    )(x1, h, wg, wu, wd, gf)


def _rev_a_index():
    u = np.arange(4 * BLOCK)
    rel = np.clip(2 * BLOCK - u, 0, None)
    return np.stack([_rel_bucket_np(rel * d) for _, d in DIL_PATTERNS])


def _rev_b_index(seq, tq):
    c = np.arange(seq + tq)
    return _rel_bucket_np(np.clip(seq - c, 0, seq - 1))


def _lookup(table, idx):
    onehot = jnp.asarray(np.asarray(idx)[None, :] == np.arange(NUM_BUCKETS)[:, None])
    return jnp.sum(jnp.where(onehot[:, None, :], table[:, :, None], 0.0), axis=0)


def kernel(x, norm_attn_g, w_in, w_proj_a, w_proj_b, w_out, rel_bias_table, diff_lambda_q1, diff_lambda_k1, diff_lambda_q2, diff_lambda_k2, diff_subln_g, norm_ffn_g, w_ffn_gate, w_ffn_up, w_ffn_down, norm_final_g):
    B, S, D = x.shape
    T = B * S
    depth = w_in.shape[0]
    assert depth == 1, "the final RMSNorm is fused into the FFN epilogue of a single layer"
    table_a = rel_bias_table[:, :N_HEADS_A].astype(F32)
    table_b = rel_bias_table[:, N_HEADS_A:].astype(F32)
    tq = 256

    idx_a = _rev_a_index()
    rev_a = jnp.stack([_lookup(table_a[:, g * HEADS_PER_GROUP_A:(g + 1) * HEADS_PER_GROUP_A], idx_a[g])
                       for g in range(N_GROUPS_A)])
    npair = OUT_WIDTH_A // LANES
    rev_a = jnp.transpose(rev_a.reshape(N_GROUPS_A, npair, 2, 4 * BLOCK), (1, 0, 2, 3))
    rev_a = rev_a.reshape(npair, 2 * N_GROUPS_A, 4 * BLOCK)
    rev_b = _lookup(table_b, _rev_b_index(S, tq))[:, None, :]

    x2 = x.reshape(T, D)
    l = 0
    lam_init = 0.8 - 0.6 * math.exp(-0.3 * l)
    proj_a, proj_r = _in_proj(x2, norm_attn_g[l][None, :], w_in[l])
    ya, wa, wb, wo = _mixer_a(proj_a.reshape(B, S, PROJ_A), rev_a, [w_proj_a[l], w_proj_b[l], w_out[l]])

    lam_vecs = jnp.stack([diff_lambda_q1[l], diff_lambda_k1[l],
                          diff_lambda_q2[l], diff_lambda_k2[l]]).astype(F32)
    yb, wg, wu, wd = _mixer_b(proj_r.reshape(B, S, PROJ_R), rev_b, lam_vecs, diff_subln_g[l][None, :].astype(F32),
                              lam_init, [w_ffn_gate[l], w_ffn_up[l], w_ffn_down[l]], tq=tq)

    x1, h = _attn_out(ya.reshape(T, OUT_WIDTH_A), yb.reshape(T, WIDTH_B), proj_r, x2, wa, wb, wo,
                      norm_ffn_g[l][None, :])
    out = _ffn(x1, h, wg, wu, wd, norm_final_g[None, :])
    return out.reshape(B, S, D)
```

```python
import functools
import math

import numpy as np
import jax
import jax.numpy as jnp
from jax import lax
from jax.experimental import pallas as pl
from jax.experimental.pallas import tpu as pltpu

D_MODEL = 2048
HEAD_DIM = 64
DIL_PATTERNS = ((128, 1), (512, 4), (2048, 16))
N_GROUPS_A = len(DIL_PATTERNS)
HEADS_PER_GROUP_A = 8
N_HEADS_A = N_GROUPS_A * HEADS_PER_GROUP_A
WIDTH_A = N_HEADS_A * HEAD_DIM
OUT_WIDTH_A = HEADS_PER_GROUP_A * HEAD_DIM
BLOCK = 128
N_HEADS_B = D_MODEL // (2 * HEAD_DIM)
WIDTH_B = N_HEADS_B * 2 * HEAD_DIM
NUM_BUCKETS = 32
MAX_DISTANCE = 2048
PROJ_A = 3 * WIDTH_A
PROJ_R = 3 * WIDTH_B + 2 * D_MODEL
D_IN = PROJ_A + PROJ_R
NORM_EPS = 1e-6
NEG_INF = -1e30
SCALE = HEAD_DIM ** -0.5
LOG2E = math.log2(math.e)

OFF_QB = 0
OFF_KB = WIDTH_B
OFF_VB = 2 * WIDTH_B
OFF_GA = 3 * WIDTH_B
OFF_GB = OFF_GA + D_MODEL

LANES = 128
VMEM_LIMIT = 56 * 1024 * 1024
VMEM_LIMIT_ATTN_OUT = 58 * 1024 * 1024
VMEM_LIMIT_FFN = 60 * 1024 * 1024
FFN_EPILOGUE_SLABS = 4

BF16 = jnp.bfloat16
F32 = jnp.float32


def _rel_bucket_np(dist):
    n = np.maximum(dist, 0)
    max_exact = NUM_BUCKETS // 2
    nf = np.maximum(n, 1).astype(np.float32)
    large = max_exact + (np.log(nf / np.float32(max_exact)) / np.float32(math.log(MAX_DISTANCE / max_exact))
                         * np.float32(NUM_BUCKETS - max_exact)).astype(np.int32)
    large = np.minimum(large, NUM_BUCKETS - 1)
    return np.where(n < max_exact, n, large).astype(np.int32)


def _rms(x, g):
    ms = jnp.mean(x * x, axis=-1, keepdims=True)
    return x * lax.rsqrt(ms + NORM_EPS) * g


def _params(sem, vmem=VMEM_LIMIT):
    return pltpu.CompilerParams(dimension_semantics=sem, vmem_limit_bytes=vmem)


BF16_ROWS = 16


def _cast_specs(weights, grid):
    nsteps = math.prod(grid)
    in_specs, out_specs, out_shapes = [], [], []
    for w in weights:
        rows, cols = w.shape
        blk = next(r for r in range(BF16_ROWS, rows + 1, BF16_ROWS)
                   if rows % r == 0 and nsteps % (rows // r) == 0 and rows // r <= nsteps)
        per = nsteps // (rows // blk)

        def index(*ids, per=per):
            step = 0
            for i, n in zip(ids, grid):
                step = step * n + i
            return (step // per, 0)

        in_specs.append(pl.BlockSpec((blk, cols), index))
        out_specs.append(pl.BlockSpec((blk, cols), index))
        out_shapes.append(jax.ShapeDtypeStruct((rows, cols), BF16))
    return in_specs, out_specs, out_shapes


def _interleave(lists):
    keyed = [((i + 0.5) / len(items), n, i, item) for n, items in enumerate(lists) for i, item in enumerate(items)]
    return [item for _, _, _, item in sorted(keyed, key=lambda k: k[:3])]


def _cast_blocks(in_refs, out_refs):
    for i_ref, o_ref in zip(in_refs, out_refs):
        o_ref[...] = i_ref[...].astype(BF16)


def _in_proj_kernel(x_hbm, g_ref, w_ref, oa_ref, or_ref, h_ref, x_buf, x_sem, *, na, nqb, tm):
    i = pl.program_id(0)
    j = pl.program_id(1)

    def x_copy(tile):
        return pltpu.make_async_copy(x_hbm.at[pl.ds(tile * tm, tm), :], x_buf, x_sem)

    @pl.when(j == 0)
    def _():
        @pl.when(i == 0)
        def _():
            x_copy(0).start()

        x_copy(i).wait()
        h = _rms(x_buf[...], g_ref[...]).astype(BF16)
        h_ref[...] = h
        oa_ref[...] = jnp.dot(h, w_ref[...].astype(BF16), preferred_element_type=F32)

        @pl.when(i + 1 < pl.num_programs(0))
        def _():
            x_copy(i + 1).start()

    @pl.when((j > 0) & (j < na))
    def _():
        oa_ref[...] = jnp.dot(h_ref[...], w_ref[...].astype(BF16), preferred_element_type=F32)

    @pl.when(j >= na)
    def _():
        scale = jnp.where(j < na + nqb, jnp.float32(SCALE * LOG2E), jnp.float32(1.0))
        res = jnp.dot(h_ref[...], w_ref[...].astype(BF16), preferred_element_type=F32)
        or_ref[...] = (res * scale).astype(BF16)


def _in_proj(x2, g, w, tm=2048, tn=512):
    T, D = x2.shape
    na = PROJ_A // tn
    assert OFF_QB == 0 and WIDTH_B % tn == 0
    return pl.pallas_call(
        functools.partial(_in_proj_kernel, na=na, nqb=WIDTH_B // tn, tm=tm),
        grid=(T // tm, D_IN // tn),
        in_specs=[pl.BlockSpec(memory_space=pl.ANY),
                  pl.BlockSpec((1, D), lambda i, j: (0, 0)),
                  pl.BlockSpec((D, tn), lambda i, j: (0, j))],
        out_specs=[pl.BlockSpec((tm, tn), lambda i, j: (i, jnp.minimum(j, na - 1))),
                   pl.BlockSpec((tm, tn), lambda i, j: (i, jnp.maximum(j - na, 0)))],
        out_shape=[jax.ShapeDtypeStruct((T, PROJ_A), F32), jax.ShapeDtypeStruct((T, PROJ_R), BF16)],
        scratch_shapes=[pltpu.VMEM((tm, D), BF16), pltpu.VMEM((tm, D), F32), pltpu.SemaphoreType.DMA(())],
        compiler_params=_params(("arbitrary", "arbitrary")),
        name="in_proj",
    )(x2, g, w)


def _mixer_a_kernel(*refs, seq, n_cast):
    qkv = (refs[0:3], refs[3:6], refs[6:9])
    rev_ref = refs[9]
    o_ref = refs[10 + n_cast]
    (bias_ref, q_st, kt_st, vx_st, s_ref, p_ref, mrow_ref, m_ref, l_ref, acc_ref) = refs[11 + 2 * n_cast:]
    _cast_blocks(refs[10:10 + n_cast], refs[11 + n_cast:11 + 2 * n_cast])
    lane = lax.broadcasted_iota(jnp.int32, (BLOCK, LANES), 1)
    lo = lane < HEAD_DIM

    row = lax.broadcasted_iota(jnp.int32, (2 * BLOCK, 2 * BLOCK), 0)
    col = lax.broadcasted_iota(jnp.int32, (2 * BLOCK, 2 * BLOCK), 1)
    rel = BLOCK + (row & (BLOCK - 1)) - col
    band = (rel >= 0) & (rel <= BLOCK)
    band_first = band & (col >= BLOCK)
    @pl.when(pl.program_id(1) == 0)
    def _():
        for g in range(N_GROUPS_A):
            halves = []
            for hh in range(2):
                x = jnp.broadcast_to(rev_ref[0, g * 2 + hh:g * 2 + hh + 1, :], (BLOCK, 4 * BLOCK))
                halves.append(pltpu.roll(x, 0, 1, stride=1, stride_axis=0)[:, BLOCK:3 * BLOCK])
            toep = jnp.concatenate(halves, axis=0)
            bias_ref[2 * g] = jnp.where(band, toep, NEG_INF)
            bias_ref[2 * g + 1] = jnp.where(band_first, toep, NEG_INF)

    eye = jnp.where(lax.broadcasted_iota(jnp.int32, (LANES, LANES), 0) == lane, 1.0, 0.0).astype(BF16)
    for g in range(N_GROUPS_A):
        kt_st[g, :, 0:BLOCK] = jnp.zeros((LANES, BLOCK), BF16)
        vx_st[g, 0:BLOCK, :] = jnp.zeros((BLOCK, 2 * LANES), BF16)
        vx_st[g, :, LANES:] = jnp.ones((BLOCK + seq, LANES), BF16)

    def scores(g, slot, base, first):
        q = q_st[g, base:base + BLOCK, :]
        zero = jnp.zeros_like(q)
        qz = jnp.concatenate([jnp.where(lo, q, zero), jnp.where(lo, zero, q)], axis=0)
        s = jnp.dot(qz, kt_st[g, :, base - BLOCK:base + BLOCK], preferred_element_type=F32)
        s_ref[slot] = s + bias_ref[2 * g + (1 if first else 0)]

    def softmax_group(slot, rg):
        rows = slice(rg * BF16_ROWS, (rg + 1) * BF16_ROWS)
        m = jnp.max(s_ref[slot, rows, :], axis=-1, keepdims=True)
        p_ref[slot, rows, :] = jnp.exp(s_ref[slot, rows, :] - m).astype(BF16)
        mrow_ref[slot, rows, :] = jnp.broadcast_to(m, (BF16_ROWS, LANES))

    def finish(g, slot, base):
        acc = jnp.dot(p_ref[slot], vx_st[g, base - BLOCK:base + BLOCK, :], preferred_element_type=F32)
        mrow = mrow_ref[slot]
        return (jnp.where(lo, mrow[:BLOCK], mrow[BLOCK:]),
                jnp.where(lo, acc[:BLOCK, LANES:], acc[BLOCK:, LANES:]),
                jnp.where(lo, acc[:BLOCK, :LANES], acc[BLOCK:, :LANES]))

    def merge(rows, last):
        states = [(m_ref[g, rows, :], l_ref[g, rows, :], acc_ref[g, rows, :]) for g in dilated] + [last]
        mx = functools.reduce(jnp.maximum, [m for m, _, _ in states])
        num = jnp.zeros((BLOCK, LANES), F32)
        den = jnp.zeros((BLOCK, LANES), F32)
        for m, l, acc in states:
            w = jnp.exp(m - mx)
            num = num + w * acc
            den = den + w * l
        o_ref[0, rows, :] = (num / den).astype(o_ref.dtype)

    dilated = [g for g in range(N_GROUPS_A) if DIL_PATTERNS[g][1] > 1]

    n_rg = 2 * BLOCK // BF16_ROWS
    blocks = []
    for g in sorted(range(N_GROUPS_A), key=lambda g: -DIL_PATTERNS[g][1]):
        d = DIL_PATTERNS[g][1]
        sub_len = seq // d
        q_ref, k_ref, v_ref = qkv[g]
        for r in range(d):
            src = pl.ds(r, sub_len, stride=d) if d > 1 else pl.ds(0, seq)
            dst = slice(BLOCK + r * sub_len, BLOCK + (r + 1) * sub_len)
            q_st[g, dst, :] = (q_ref[0, src, :] * SCALE).astype(BF16)
            kt_st[g, :, dst] = lax.dot_general(eye, k_ref[0, src, :].astype(BF16), (((1,), (1,)), ((), ())),
                                               preferred_element_type=F32).astype(BF16)
            vx_st[g, dst, :LANES] = v_ref[0, src, :].astype(BF16)
            for n in range(sub_len // BLOCK):
                base = BLOCK + r * sub_len + n * BLOCK
                out_rows = pl.ds(n * BLOCK * d + r, BLOCK, stride=d) if d > 1 else pl.ds(n * BLOCK, BLOCK)
                blocks.append((g, base, n == 0, out_rows))

    nslot = s_ref.shape[0]
    for t in range(len(blocks) + 2):
        if t < len(blocks):
            g, base, first, _ = blocks[t]
            scores(g, t % nslot, base, first)
        for rg in range(n_rg):
            if 1 <= t <= len(blocks):
                softmax_group((t - 1) % nslot, rg)
            if rg == n_rg // 2 and 2 <= t:
                g, base, _, out_rows = blocks[t - 2]
                state = finish(g, (t - 2) % nslot, base)
                if g in dilated:
                    m_ref[g, out_rows, :], l_ref[g, out_rows, :], acc_ref[g, out_rows, :] = state
                else:
                    merge(out_rows, state)


def _mixer_a(proj_a, rev_a, weights):
    B, S, _ = proj_a.shape
    npair = OUT_WIDTH_A // LANES
    grid = (npair, B)

    def col(which, g):
        base = (which * WIDTH_A + g * OUT_WIDTH_A) // LANES
        return pl.BlockSpec((1, S, LANES), lambda hp, b: (b, 0, base + hp))

    cast_in, cast_out, cast_shapes = _cast_specs(weights, grid)
    in_specs = [col(which, g) for g in range(N_GROUPS_A) for which in range(3)]
    in_specs.append(pl.BlockSpec((1, 2 * N_GROUPS_A, 4 * BLOCK), lambda hp, b: (hp, 0, 0)))
    return pl.pallas_call(
        functools.partial(_mixer_a_kernel, seq=S, n_cast=len(weights)),
        grid=grid,
        in_specs=in_specs + cast_in,
        out_specs=[pl.BlockSpec((1, S, LANES), lambda hp, b: (b, 0, hp))] + cast_out,
        out_shape=[jax.ShapeDtypeStruct((B, S, OUT_WIDTH_A), BF16)] + cast_shapes,
        scratch_shapes=[pltpu.VMEM((2 * N_GROUPS_A, 2 * BLOCK, 2 * BLOCK), F32),
                        pltpu.VMEM((N_GROUPS_A, BLOCK + S, LANES), BF16),
                        pltpu.VMEM((N_GROUPS_A, LANES, BLOCK + S), BF16),
                        pltpu.VMEM((N_GROUPS_A, BLOCK + S, 2 * LANES), BF16),
                        pltpu.VMEM((4, 2 * BLOCK, 2 * BLOCK), F32),
                        pltpu.VMEM((4, 2 * BLOCK, 2 * BLOCK), BF16),
                        pltpu.VMEM((4, 2 * BLOCK, LANES), F32),
                        pltpu.VMEM((N_GROUPS_A, S, LANES), F32),
                        pltpu.VMEM((N_GROUPS_A, S, LANES), F32),
                        pltpu.VMEM((N_GROUPS_A, S, LANES), F32)],
        compiler_params=_params(("arbitrary", "arbitrary")),
        name="mixer_a",
    )(*([proj_a] * 9), rev_a, *weights)


def _mixer_b_kernel(*refs, tq, seq, lam_init, n_cast):
    nq = seq // tq
    q_ref, k_ref, v_ref, rev_ref, lam_ref, g_ref = refs[:6]
    o_ref = refs[6 + n_cast]
    toep_ref, vx_ref = refs[7 + 2 * n_cast:9 + 2 * n_cast]
    s_refs = refs[9 + 2 * n_cast:9 + 2 * n_cast + nq]
    p_refs = refs[9 + 2 * n_cast + nq:]
    _cast_blocks(refs[6:6 + n_cast], refs[7 + n_cast:7 + 2 * n_cast])
    rg_rows = BF16_ROWS
    n_rg = 2 * tq // rg_rows

    x = jnp.broadcast_to(rev_ref[0] * LOG2E, (tq, seq + tq))
    rolled = pltpu.roll(x, 0, 1, stride=1, stride_axis=0)
    toep_ref[:, :seq] = rolled[:, :seq]
    row = lax.broadcasted_iota(jnp.int32, (tq, tq), 0)
    col = lax.broadcasted_iota(jnp.int32, (tq, tq), 1)
    toep_ref[:, seq:] = jnp.where(col <= row, rolled[:, seq:], NEG_INF)

    vx_ref[:, :LANES] = v_ref[0]
    vx_ref[:, LANES:] = jnp.ones((seq, LANES), BF16)

    lane = lax.broadcasted_iota(jnp.int32, (tq, LANES), 1)
    lo = lane < HEAD_DIM
    lv = lam_ref[...]
    lam = (jnp.exp(jnp.sum(lv[0:1] * lv[1:2], axis=-1, keepdims=True))
           - jnp.exp(jnp.sum(lv[2:3] * lv[3:4], axis=-1, keepdims=True)) + lam_init)

    def score_chunks(qi):
        q = q_ref[0, qi * tq:(qi + 1) * tq, :]
        zero = jnp.zeros_like(q)
        qz = jnp.concatenate([jnp.where(lo, q, zero), jnp.where(lo, zero, q)], axis=0)

        def chunk(c):
            kc = k_ref[0, c * tq:(c + 1) * tq, :]
            s_refs[qi][:, c * tq:(c + 1) * tq] = lax.dot_general(
                qz, kc, (((1,), (1,)), ((), ())), preferred_element_type=F32)

        return [functools.partial(chunk, c) for c in range(qi + 1)]

    def softmax_group(qi, rg):
        rows = slice(rg * rg_rows, (rg + 1) * rg_rows)
        brow = (rg * rg_rows) % tq
        bias = toep_ref[brow:brow + rg_rows, (nq - qi) * tq:(nq + 1) * tq]
        t = s_refs[qi][rows, :] + bias
        m = jnp.max(t, axis=-1, keepdims=True)
        p_refs[qi][rows, :] = jnp.exp2(t - m).astype(BF16)

    halves = {}

    def value_matmul(qi, half):
        acc = jnp.dot(p_refs[qi][half * tq:(half + 1) * tq, :], vx_ref[:(qi + 1) * tq, :],
                      preferred_element_type=F32)
        halves[qi, half] = acc[:, :LANES] / acc[:, LANES:]

    def finish(qi):
        y = halves[qi, 0] - lam * halves[qi, 1]
        y = _rms(y, g_ref[...]) * (1.0 - lam_init)
        o_ref[0, qi * tq:(qi + 1) * tq, :] = y.astype(o_ref.dtype)

    order = list(range(nq - 1, -1, -1))
    for t in range(nq + 2):
        scores_t = score_chunks(order[t]) if t < nq else []
        softmax_t, tail_t = [], []
        if 1 <= t <= nq:
            qi = order[t - 1]
            groups = [functools.partial(softmax_group, qi, rg) for rg in range(n_rg)]
            softmax_t = groups[:n_rg // 2] + [functools.partial(value_matmul, qi, 0)] + groups[n_rg // 2:]
        if t >= 2:
            qi = order[t - 2]
            tail_t = [functools.partial(value_matmul, qi, 1), functools.partial(finish, qi)]
        for emit in _interleave([scores_t, softmax_t, tail_t]):
            emit()


def _mixer_b(proj, rev_b, lam_vecs, subln_g, lam_init, weights, tq=256):
    B, S, _ = proj.shape
    H = N_HEADS_B
    nq = S // tq
    grid = (B, H)
    kern = functools.partial(_mixer_b_kernel, tq=tq, seq=S, lam_init=lam_init, n_cast=len(weights))

    def col(off):
        return pl.BlockSpec((1, S, LANES), lambda b, h: (b, 0, off // LANES + h))

    cast_in, cast_out, cast_shapes = _cast_specs(weights, grid)
    return pl.pallas_call(
        kern,
        grid=grid,
        in_specs=[col(OFF_QB), col(OFF_KB), col(OFF_VB),
                  pl.BlockSpec((1, 1, S + tq), lambda b, h: (h, 0, 0)),
                  pl.BlockSpec((4, HEAD_DIM), lambda b, h: (0, 0)),
                  pl.BlockSpec((1, 2 * HEAD_DIM), lambda b, h: (0, 0))] + cast_in,
        out_specs=[pl.BlockSpec((1, S, LANES), lambda b, h: (b, 0, h))] + cast_out,
        out_shape=[jax.ShapeDtypeStruct((B, S, WIDTH_B), BF16)] + cast_shapes,
        scratch_shapes=([pltpu.VMEM((tq, S + tq), F32), pltpu.VMEM((S, 2 * LANES), BF16)]
                        + [pltpu.VMEM((2 * tq, (i + 1) * tq), F32) for i in range(nq)]
                        + [pltpu.VMEM((2 * tq, (i + 1) * tq), BF16) for i in range(nq)]),
        compiler_params=_params(("arbitrary", "arbitrary")),
        name="mixer_b",
    )(proj, proj, proj, rev_b, lam_vecs, subln_g, *weights)


def _attn_out_kernel(ya_ref, yb_ref, ga_ref, gb_ref, x_ref, wa_ref, wb_ref, wo_ref, g_ref, x1_ref, h_ref):
    pa = jnp.dot(ya_ref[...], wa_ref[...], preferred_element_type=F32)
    pb = jnp.dot(yb_ref[...], wb_ref[...], preferred_element_type=F32)
    ga = jax.nn.sigmoid(ga_ref[...].astype(F32))
    gb = jax.nn.sigmoid(gb_ref[...].astype(F32))
    merged = (ga * pa + gb * pb).astype(BF16)
    slab = merged.shape[0] // 2
    for r in range(2):
        rows = slice(r * slab, (r + 1) * slab)
        x1 = x_ref[rows, :] + jnp.dot(merged[rows], wo_ref[...], preferred_element_type=F32)
        x1_ref[rows, :] = x1
        h_ref[rows, :] = _rms(x1, g_ref[...]).astype(BF16)


def _attn_out(ya, yb, proj_r, x2, wa, wb, wo, g, tm=512):
    T, D = x2.shape

    def resident(shape):
        return pl.BlockSpec(shape, lambda i: (0, 0), pipeline_mode=pl.Buffered(1))

    return pl.pallas_call(
        _attn_out_kernel,
        grid=(T // tm,),
        in_specs=[pl.BlockSpec((tm, OUT_WIDTH_A), lambda i: (i, 0)),
                  pl.BlockSpec((tm, WIDTH_B), lambda i: (i, 0)),
                  pl.BlockSpec((tm, D), lambda i: (i, OFF_GA // D)),
                  pl.BlockSpec((tm, D), lambda i: (i, OFF_GB // D)),
                  pl.BlockSpec((tm, D), lambda i: (i, 0)),
                  resident((OUT_WIDTH_A, D)), resident((WIDTH_B, D)), resident((D, D)), resident((1, D))],
        out_specs=[pl.BlockSpec((tm, D), lambda i: (i, 0)), pl.BlockSpec((tm, D), lambda i: (i, 0))],
        out_shape=[jax.ShapeDtypeStruct((T, D), F32), jax.ShapeDtypeStruct((T, D), BF16)],
        compiler_params=_params(("parallel",), vmem=VMEM_LIMIT_ATTN_OUT),
        name="attn_out",
    )(ya, yb, proj_r, proj_r, x2, wa, wb, wo, g)


def _ffn_kernel(x1_hbm, h_ref, wg_ref, wu_ref, wd_ref, gf_ref, o_ref, x1_buf, x1_sem, *, tm):
    i = pl.program_id(0)
    f = pl.program_id(1)
    x1_copy = pltpu.make_async_copy(x1_hbm.at[pl.ds(i * tm, tm), :], x1_buf, x1_sem)

    last = pl.num_programs(1) - 1

    def gated():
        h = h_ref[...]
        a = jnp.dot(h, wg_ref[...], preferred_element_type=F32)
        b = jnp.dot(h, wu_ref[...], preferred_element_type=F32)
        return ((a * jax.nn.sigmoid(a)) * b).astype(BF16)

    @pl.when(f == 0)
    def _():
        x1_copy.start()
        o_ref[...] = jnp.dot(gated(), wd_ref[...], preferred_element_type=F32)

    @pl.when((f > 0) & (f < last))
    def _():
        o_ref[...] += jnp.dot(gated(), wd_ref[...], preferred_element_type=F32)

    @pl.when(f == last)
    def _():
        x1_copy.wait()
        u = gated()
        slab = tm // FFN_EPILOGUE_SLABS
        for r in range(FFN_EPILOGUE_SLABS):
            rows = slice(r * slab, (r + 1) * slab)
            y = o_ref[rows, :] + jnp.dot(u[rows], wd_ref[...], preferred_element_type=F32)
            o_ref[rows, :] = _rms(x1_buf[rows, :] + y, gf_ref[...])


def _ffn(x1, h, wg, wu, wd, gf, tm=1024, tf=512):
    T, D = x1.shape
    F = wg.shape[1]
    return pl.pallas_call(
        functools.partial(_ffn_kernel, tm=tm),
        grid=(T // tm, F // tf),
        in_specs=[pl.BlockSpec(memory_space=pl.ANY),
                  pl.BlockSpec((tm, D), lambda i, f: (i, 0)),
                  pl.BlockSpec((D, tf), lambda i, f: (0, f)),
                  pl.BlockSpec((D, tf), lambda i, f: (0, f)),
                  pl.BlockSpec((tf, D), lambda i, f: (f, 0)),
                  pl.BlockSpec((1, D), lambda i, f: (0, 0))],
        out_specs=pl.BlockSpec((tm, D), lambda i, f: (i, 0)),
        out_shape=jax.ShapeDtypeStruct((T, D), F32),
        scratch_shapes=[pltpu.VMEM((tm, D), F32), pltpu.SemaphoreType.DMA(())],
        compiler_params=_params(("parallel", "arbitrary"), vmem=VMEM_LIMIT_FFN),
        name="ffn",
    )(x1, h, wg, wu, wd, gf)


def _rev_a_index():
    u = np.arange(4 * BLOCK)
    rel = np.clip(2 * BLOCK - u, 0, None)
    return np.stack([_rel_bucket_np(rel * d) for _, d in DIL_PATTERNS])


def _rev_b_index(seq, tq):
    c = np.arange(seq + tq)
    return _rel_bucket_np(np.clip(seq - c, 0, seq - 1))


def _lookup(table, idx):
    onehot = jnp.asarray(np.asarray(idx)[None, :] == np.arange(NUM_BUCKETS)[:, None])
    return jnp.sum(jnp.where(onehot[:, None, :], table[:, :, None], 0.0), axis=0)


def kernel(x, norm_attn_g, w_in, w_proj_a, w_proj_b, w_out, rel_bias_table, diff_lambda_q1, diff_lambda_k1, diff_lambda_q2, diff_lambda_k2, diff_subln_g, norm_ffn_g, w_ffn_gate, w_ffn_up, w_ffn_down, norm_final_g):
    B, S, D = x.shape
    T = B * S
    depth = w_in.shape[0]
    assert depth == 1, "the final RMSNorm is fused into the FFN epilogue of a single layer"
    table_a = rel_bias_table[:, :N_HEADS_A].astype(F32)
    table_b = rel_bias_table[:, N_HEADS_A:].astype(F32)
    tq = 256

    idx_a = _rev_a_index()
    rev_a = jnp.stack([_lookup(table_a[:, g * HEADS_PER_GROUP_A:(g + 1) * HEADS_PER_GROUP_A], idx_a[g])
                       for g in range(N_GROUPS_A)])
    npair = OUT_WIDTH_A // LANES
    rev_a = jnp.transpose(rev_a.reshape(N_GROUPS_A, npair, 2, 4 * BLOCK), (1, 0, 2, 3))
    rev_a = rev_a.reshape(npair, 2 * N_GROUPS_A, 4 * BLOCK)
    rev_b = _lookup(table_b, _rev_b_index(S, tq))[:, None, :]

    x2 = x.reshape(T, D)
    l = 0
    lam_init = 0.8 - 0.6 * math.exp(-0.3 * l)
    proj_a, proj_r = _in_proj(x2, norm_attn_g[l][None, :], w_in[l])
    ya, wa, wb, wo = _mixer_a(proj_a.reshape(B, S, PROJ_A), rev_a, [w_proj_a[l], w_proj_b[l], w_out[l]])

    lam_vecs = jnp.stack([diff_lambda_q1[l], diff_lambda_k1[l],
                          diff_lambda_q2[l], diff_lambda_k2[l]]).astype(F32)
    yb, wg, wu, wd = _mixer_b(proj_r.reshape(B, S, PROJ_R), rev_b, lam_vecs, diff_subln_g[l][None, :].astype(F32),
                              lam_init, [w_ffn_gate[l], w_ffn_up[l], w_ffn_down[l]], tq=tq)

    x1, h = _attn_out(ya.reshape(T, OUT_WIDTH_A), yb.reshape(T, WIDTH_B), proj_r, x2, wa, wb, wo,
                      norm_ffn_g[l][None, :])
    out = _ffn(x1, h, wg, wu, wd, norm_final_g[None, :])
    return out.reshape(B, S, D)
```

```python
import functools
import math

import numpy as np
import jax
import jax.numpy as jnp
from jax import lax
from jax.experimental import pallas as pl
from jax.experimental.pallas import tpu as pltpu

D_MODEL = 2048
HEAD_DIM = 64
DIL_PATTERNS = ((128, 1), (512, 4), (2048, 16))
N_GROUPS_A = len(DIL_PATTERNS)
HEADS_PER_GROUP_A = 8
N_HEADS_A = N_GROUPS_A * HEADS_PER_GROUP_A
WIDTH_A = N_HEADS_A * HEAD_DIM
OUT_WIDTH_A = HEADS_PER_GROUP_A * HEAD_DIM
BLOCK = 128
N_HEADS_B = D_MODEL // (2 * HEAD_DIM)
WIDTH_B = N_HEADS_B * 2 * HEAD_DIM
NUM_BUCKETS = 32
MAX_DISTANCE = 2048
PROJ_A = 3 * WIDTH_A
PROJ_R = 3 * WIDTH_B + 2 * D_MODEL
D_IN = PROJ_A + PROJ_R
NORM_EPS = 1e-6
NEG_INF = -1e30
SCALE = HEAD_DIM ** -0.5
LOG2E = math.log2(math.e)

OFF_QB = 0
OFF_KB = WIDTH_B
OFF_VB = 2 * WIDTH_B
OFF_GA = 3 * WIDTH_B
OFF_GB = OFF_GA + D_MODEL

LANES = 128
VMEM_LIMIT = 56 * 1024 * 1024
VMEM_LIMIT_ATTN_OUT = 58 * 1024 * 1024
VMEM_LIMIT_FFN = 60 * 1024 * 1024
FFN_EPILOGUE_SLABS = 4
IN_PROJ_PROLOGUE_SLABS = 4

BF16 = jnp.bfloat16
F32 = jnp.float32


def _rel_bucket_np(dist):
    n = np.maximum(dist, 0)
    max_exact = NUM_BUCKETS // 2
    nf = np.maximum(n, 1).astype(np.float32)
    large = max_exact + (np.log(nf / np.float32(max_exact)) / np.float32(math.log(MAX_DISTANCE / max_exact))
                         * np.float32(NUM_BUCKETS - max_exact)).astype(np.int32)
    large = np.minimum(large, NUM_BUCKETS - 1)
    return np.where(n < max_exact, n, large).astype(np.int32)


def _rms(x, g):
    ms = jnp.mean(x * x, axis=-1, keepdims=True)
    return x * lax.rsqrt(ms + NORM_EPS) * g


def _params(sem, vmem=VMEM_LIMIT):
    return pltpu.CompilerParams(dimension_semantics=sem, vmem_limit_bytes=vmem)


BF16_ROWS = 16


def _cast_specs(weights, grid):
    nsteps = math.prod(grid)
    in_specs, out_specs, out_shapes = [], [], []
    for w in weights:
        rows, cols = w.shape
        blk = next(r for r in range(BF16_ROWS, rows + 1, BF16_ROWS)
                   if rows % r == 0 and nsteps % (rows // r) == 0 and rows // r <= nsteps)
        per = nsteps // (rows // blk)

        def index(*ids, per=per):
            step = 0
            for i, n in zip(ids, grid):
                step = step * n + i
            return (step // per, 0)

        in_specs.append(pl.BlockSpec((blk, cols), index))
        out_specs.append(pl.BlockSpec((blk, cols), index))
        out_shapes.append(jax.ShapeDtypeStruct((rows, cols), BF16))
    return in_specs, out_specs, out_shapes


def _interleave(lists):
    keyed = [((i + 0.5) / len(items), n, i, item) for n, items in enumerate(lists) for i, item in enumerate(items)]
    return [item for _, _, _, item in sorted(keyed, key=lambda k: k[:3])]


def _cast_blocks(in_refs, out_refs):
    for i_ref, o_ref in zip(in_refs, out_refs):
        o_ref[...] = i_ref[...].astype(BF16)


def _in_proj_kernel(x_hbm, g_ref, w_ref, oa_ref, or_ref, h_ref, x_buf, x_sem, *, na, nqb, tm):
    i = pl.program_id(0)
    j = pl.program_id(1)

    def x_copy(tile):
        return pltpu.make_async_copy(x_hbm.at[pl.ds(tile * tm, tm), :], x_buf, x_sem)

    @pl.when(j == 0)
    def _():
        @pl.when(i == 0)
        def _():
            x_copy(0).start()

        x_copy(i).wait()
        w0 = w_ref[...].astype(BF16)
        slab = tm // IN_PROJ_PROLOGUE_SLABS
        for r in range(IN_PROJ_PROLOGUE_SLABS):
            rows = slice(r * slab, (r + 1) * slab)
            h = _rms(x_buf[rows, :], g_ref[...]).astype(BF16)
            h_ref[rows, :] = h
            oa_ref[rows, :] = jnp.dot(h, w0, preferred_element_type=F32)

        @pl.when(i + 1 < pl.num_programs(0))
        def _():
            x_copy(i + 1).start()

    @pl.when((j > 0) & (j < na))
    def _():
        oa_ref[...] = jnp.dot(h_ref[...], w_ref[...].astype(BF16), preferred_element_type=F32)

    @pl.when(j >= na)
    def _():
        scale = jnp.where(j < na + nqb, jnp.float32(SCALE * LOG2E), jnp.float32(1.0))
        res = jnp.dot(h_ref[...], w_ref[...].astype(BF16), preferred_element_type=F32)
        or_ref[...] = (res * scale).astype(BF16)


def _in_proj(x2, g, w, tm=2048, tn=512):
    T, D = x2.shape
    na = PROJ_A // tn
    assert OFF_QB == 0 and WIDTH_B % tn == 0
    return pl.pallas_call(
        functools.partial(_in_proj_kernel, na=na, nqb=WIDTH_B // tn, tm=tm),
        grid=(T // tm, D_IN // tn),
        in_specs=[pl.BlockSpec(memory_space=pl.ANY),
                  pl.BlockSpec((1, D), lambda i, j: (0, 0)),
                  pl.BlockSpec((D, tn), lambda i, j: (0, j))],
        out_specs=[pl.BlockSpec((tm, tn), lambda i, j: (i, jnp.minimum(j, na - 1))),
                   pl.BlockSpec((tm, tn), lambda i, j: (i, jnp.maximum(j - na, 0)))],
        out_shape=[jax.ShapeDtypeStruct((T, PROJ_A), F32), jax.ShapeDtypeStruct((T, PROJ_R), BF16)],
        scratch_shapes=[pltpu.VMEM((tm, D), BF16), pltpu.VMEM((tm, D), F32), pltpu.SemaphoreType.DMA(())],
        compiler_params=_params(("arbitrary", "arbitrary")),
        name="in_proj",
    )(x2, g, w)


def _mixer_a_kernel(*refs, seq, n_cast):
    qkv = (refs[0:3], refs[3:6], refs[6:9])
    rev_ref = refs[9]
    o_ref = refs[10 + n_cast]
    (bias_ref, q_st, kt_st, vx_st, s_ref, p_ref, mrow_ref, m_ref, l_ref, acc_ref) = refs[11 + 2 * n_cast:]
    _cast_blocks(refs[10:10 + n_cast], refs[11 + n_cast:11 + 2 * n_cast])
    lane = lax.broadcasted_iota(jnp.int32, (BLOCK, LANES), 1)
    lo = lane < HEAD_DIM

    row = lax.broadcasted_iota(jnp.int32, (2 * BLOCK, 2 * BLOCK), 0)
    col = lax.broadcasted_iota(jnp.int32, (2 * BLOCK, 2 * BLOCK), 1)
    rel = BLOCK + (row & (BLOCK - 1)) - col
    band = (rel >= 0) & (rel <= BLOCK)
    band_first = band & (col >= BLOCK)
    @pl.when(pl.program_id(1) == 0)
    def _():
        for g in range(N_GROUPS_A):
            halves = []
            for hh in range(2):
                x = jnp.broadcast_to(rev_ref[0, g * 2 + hh:g * 2 + hh + 1, :], (BLOCK, 4 * BLOCK))
                halves.append(pltpu.roll(x, 0, 1, stride=1, stride_axis=0)[:, BLOCK:3 * BLOCK])
            toep = jnp.concatenate(halves, axis=0)
            bias_ref[2 * g] = jnp.where(band, toep, NEG_INF)
            bias_ref[2 * g + 1] = jnp.where(band_first, toep, NEG_INF)

    eye = jnp.where(lax.broadcasted_iota(jnp.int32, (LANES, LANES), 0) == lane, 1.0, 0.0).astype(BF16)
    for g in range(N_GROUPS_A):
        kt_st[g, :, 0:BLOCK] = jnp.zeros((LANES, BLOCK), BF16)
        vx_st[g, 0:BLOCK, :] = jnp.zeros((BLOCK, 2 * LANES), BF16)
        vx_st[g, :, LANES:] = jnp.ones((BLOCK + seq, LANES), BF16)

    def scores(g, slot, base, first):
        q = q_st[g, base:base + BLOCK, :]
        zero = jnp.zeros_like(q)
        qz = jnp.concatenate([jnp.where(lo, q, zero), jnp.where(lo, zero, q)], axis=0)
        s = jnp.dot(qz, kt_st[g, :, base - BLOCK:base + BLOCK], preferred_element_type=F32)
        s_ref[slot] = s + bias_ref[2 * g + (1 if first else 0)]

    def softmax_group(slot, rg):
        rows = slice(rg * BF16_ROWS, (rg + 1) * BF16_ROWS)
        m = jnp.max(s_ref[slot, rows, :], axis=-1, keepdims=True)
        p_ref[slot, rows, :] = jnp.exp(s_ref[slot, rows, :] - m).astype(BF16)
        mrow_ref[slot, rows, :] = jnp.broadcast_to(m, (BF16_ROWS, LANES))

    def finish(g, slot, base):
        acc = jnp.dot(p_ref[slot], vx_st[g, base - BLOCK:base + BLOCK, :], preferred_element_type=F32)
        mrow = mrow_ref[slot]
        return (jnp.where(lo, mrow[:BLOCK], mrow[BLOCK:]),
                jnp.where(lo, acc[:BLOCK, LANES:], acc[BLOCK:, LANES:]),
                jnp.where(lo, acc[:BLOCK, :LANES], acc[BLOCK:, :LANES]))

    def merge(rows, last):
        states = [(m_ref[g, rows, :], l_ref[g, rows, :], acc_ref[g, rows, :]) for g in dilated] + [last]
        mx = functools.reduce(jnp.maximum, [m for m, _, _ in states])
        num = jnp.zeros((BLOCK, LANES), F32)
        den = jnp.zeros((BLOCK, LANES), F32)
        for m, l, acc in states:
            w = jnp.exp(m - mx)
            num = num + w * acc
            den = den + w * l
        o_ref[0, rows, :] = (num / den).astype(o_ref.dtype)

    dilated = [g for g in range(N_GROUPS_A) if DIL_PATTERNS[g][1] > 1]

    n_rg = 2 * BLOCK // BF16_ROWS
    blocks = []
    for g in sorted(range(N_GROUPS_A), key=lambda g: -DIL_PATTERNS[g][1]):
        d = DIL_PATTERNS[g][1]
        sub_len = seq // d
        q_ref, k_ref, v_ref = qkv[g]
        for r in range(d):
            src = pl.ds(r, sub_len, stride=d) if d > 1 else pl.ds(0, seq)
            dst = slice(BLOCK + r * sub_len, BLOCK + (r + 1) * sub_len)
            q_st[g, dst, :] = (q_ref[0, src, :] * SCALE).astype(BF16)
            kt_st[g, :, dst] = lax.dot_general(eye, k_ref[0, src, :].astype(BF16), (((1,), (1,)), ((), ())),
                                               preferred_element_type=F32).astype(BF16)
            vx_st[g, dst, :LANES] = v_ref[0, src, :].astype(BF16)
            for n in range(sub_len // BLOCK):
                base = BLOCK + r * sub_len + n * BLOCK
                out_rows = pl.ds(n * BLOCK * d + r, BLOCK, stride=d) if d > 1 else pl.ds(n * BLOCK, BLOCK)
                blocks.append((g, base, n == 0, out_rows))

    nslot = s_ref.shape[0]
    for t in range(len(blocks) + 2):
        if t < len(blocks):
            g, base, first, _ = blocks[t]
            scores(g, t % nslot, base, first)
        for rg in range(n_rg):
            if 1 <= t <= len(blocks):
                softmax_group((t - 1) % nslot, rg)
            if rg == n_rg // 2 and 2 <= t:
                g, base, _, out_rows = blocks[t - 2]
                state = finish(g, (t - 2) % nslot, base)
                if g in dilated:
                    m_ref[g, out_rows, :], l_ref[g, out_rows, :], acc_ref[g, out_rows, :] = state
                else:
                    merge(out_rows, state)


def _mixer_a(proj_a, rev_a, weights):
    B, S, _ = proj_a.shape
    npair = OUT_WIDTH_A // LANES
    grid = (npair, B)

    def col(which, g):
        base = (which * WIDTH_A + g * OUT_WIDTH_A) // LANES
        return pl.BlockSpec((1, S, LANES), lambda hp, b: (b, 0, base + hp))

    cast_in, cast_out, cast_shapes = _cast_specs(weights, grid)
    in_specs = [col(which, g) for g in range(N_GROUPS_A) for which in range(3)]
    in_specs.append(pl.BlockSpec((1, 2 * N_GROUPS_A, 4 * BLOCK), lambda hp, b: (hp, 0, 0)))
    return pl.pallas_call(
        functools.partial(_mixer_a_kernel, seq=S, n_cast=len(weights)),
        grid=grid,
        in_specs=in_specs + cast_in,
        out_specs=[pl.BlockSpec((1, S, LANES), lambda hp, b: (b, 0, hp))] + cast_out,
        out_shape=[jax.ShapeDtypeStruct((B, S, OUT_WIDTH_A), BF16)] + cast_shapes,
        scratch_shapes=[pltpu.VMEM((2 * N_GROUPS_A, 2 * BLOCK, 2 * BLOCK), F32),
                        pltpu.VMEM((N_GROUPS_A, BLOCK + S, LANES), BF16),
                        pltpu.VMEM((N_GROUPS_A, LANES, BLOCK + S), BF16),
                        pltpu.VMEM((N_GROUPS_A, BLOCK + S, 2 * LANES), BF16),
                        pltpu.VMEM((4, 2 * BLOCK, 2 * BLOCK), F32),
                        pltpu.VMEM((4, 2 * BLOCK, 2 * BLOCK), BF16),
                        pltpu.VMEM((4, 2 * BLOCK, LANES), F32),
                        pltpu.VMEM((N_GROUPS_A, S, LANES), F32),
                        pltpu.VMEM((N_GROUPS_A, S, LANES), F32),
                        pltpu.VMEM((N_GROUPS_A, S, LANES), F32)],
        compiler_params=_params(("arbitrary", "arbitrary")),
        name="mixer_a",
    )(*([proj_a] * 9), rev_a, *weights)


def _mixer_b_kernel(*refs, tq, seq, lam_init, n_cast):
    nq = seq // tq
    q_ref, k_ref, v_ref, rev_ref, lam_ref, g_ref = refs[:6]
    o_ref = refs[6 + n_cast]
    toep_ref, vx_ref = refs[7 + 2 * n_cast:9 + 2 * n_cast]
    s_refs = refs[9 + 2 * n_cast:9 + 2 * n_cast + nq]
    p_refs = refs[9 + 2 * n_cast + nq:]
    _cast_blocks(refs[6:6 + n_cast], refs[7 + n_cast:7 + 2 * n_cast])
    rg_rows = BF16_ROWS
    n_rg = 2 * tq // rg_rows

    x = jnp.broadcast_to(rev_ref[0] * LOG2E, (tq, seq + tq))
    rolled = pltpu.roll(x, 0, 1, stride=1, stride_axis=0)
    toep_ref[:, :seq] = rolled[:, :seq]
    row = lax.broadcasted_iota(jnp.int32, (tq, tq), 0)
    col = lax.broadcasted_iota(jnp.int32, (tq, tq), 1)
    toep_ref[:, seq:] = jnp.where(col <= row, rolled[:, seq:], NEG_INF)

    vx_ref[:, :LANES] = v_ref[0]
    vx_ref[:, LANES:] = jnp.ones((seq, LANES), BF16)

    lane = lax.broadcasted_iota(jnp.int32, (tq, LANES), 1)
    lo = lane < HEAD_DIM
    lv = lam_ref[...]
    lam = (jnp.exp(jnp.sum(lv[0:1] * lv[1:2], axis=-1, keepdims=True))
           - jnp.exp(jnp.sum(lv[2:3] * lv[3:4], axis=-1, keepdims=True)) + lam_init)

    def score_chunks(qi):
        q = q_ref[0, qi * tq:(qi + 1) * tq, :]
        zero = jnp.zeros_like(q)
        qz = jnp.concatenate([jnp.where(lo, q, zero), jnp.where(lo, zero, q)], axis=0)

        def chunk(c):
            kc = k_ref[0, c * tq:(c + 1) * tq, :]
            s_refs[qi][:, c * tq:(c + 1) * tq] = lax.dot_general(
                qz, kc, (((1,), (1,)), ((), ())), preferred_element_type=F32)

        return [functools.partial(chunk, c) for c in range(qi + 1)]

    def softmax_group(qi, rg):
        rows = slice(rg * rg_rows, (rg + 1) * rg_rows)
        brow = (rg * rg_rows) % tq
        bias = toep_ref[brow:brow + rg_rows, (nq - qi) * tq:(nq + 1) * tq]
        t = s_refs[qi][rows, :] + bias
        m = jnp.max(t, axis=-1, keepdims=True)
        p_refs[qi][rows, :] = jnp.exp2(t - m).astype(BF16)

    halves = {}

    def value_matmul(qi, half):
        acc = jnp.dot(p_refs[qi][half * tq:(half + 1) * tq, :], vx_ref[:(qi + 1) * tq, :],
                      preferred_element_type=F32)
        halves[qi, half] = acc[:, :LANES] / acc[:, LANES:]

    def finish(qi):
        y = halves[qi, 0] - lam * halves[qi, 1]
        y = _rms(y, g_ref[...]) * (1.0 - lam_init)
        o_ref[0, qi * tq:(qi + 1) * tq, :] = y.astype(o_ref.dtype)

    order = list(range(nq - 1, -1, -1))
    for t in range(nq + 2):
        scores_t = score_chunks(order[t]) if t < nq else []
        softmax_t, tail_t = [], []
        if 1 <= t <= nq:
            qi = order[t - 1]
            groups = [functools.partial(softmax_group, qi, rg) for rg in range(n_rg)]
            softmax_t = groups[:n_rg // 2] + [functools.partial(value_matmul, qi, 0)] + groups[n_rg // 2:]
        if t >= 2:
            qi = order[t - 2]
            tail_t = [functools.partial(value_matmul, qi, 1), functools.partial(finish, qi)]
        for emit in _interleave([scores_t, softmax_t, tail_t]):
            emit()


def _mixer_b(proj, rev_b, lam_vecs, subln_g, lam_init, weights, tq=256):
    B, S, _ = proj.shape
    H = N_HEADS_B
    nq = S // tq
    grid = (B, H)
    kern = functools.partial(_mixer_b_kernel, tq=tq, seq=S, lam_init=lam_init, n_cast=len(weights))

    def col(off):
        return pl.BlockSpec((1, S, LANES), lambda b, h: (b, 0, off // LANES + h))

    cast_in, cast_out, cast_shapes = _cast_specs(weights, grid)
    return pl.pallas_call(
        kern,
        grid=grid,
        in_specs=[col(OFF_QB), col(OFF_KB), col(OFF_VB),
                  pl.BlockSpec((1, 1, S + tq), lambda b, h: (h, 0, 0)),
                  pl.BlockSpec((4, HEAD_DIM), lambda b, h: (0, 0)),
                  pl.BlockSpec((1, 2 * HEAD_DIM), lambda b, h: (0, 0))] + cast_in,
        out_specs=[pl.BlockSpec((1, S, LANES), lambda b, h: (b, 0, h))] + cast_out,
        out_shape=[jax.ShapeDtypeStruct((B, S, WIDTH_B), BF16)] + cast_shapes,
        scratch_shapes=([pltpu.VMEM((tq, S + tq), F32), pltpu.VMEM((S, 2 * LANES), BF16)]
                        + [pltpu.VMEM((2 * tq, (i + 1) * tq), F32) for i in range(nq)]
                        + [pltpu.VMEM((2 * tq, (i + 1) * tq), BF16) for i in range(nq)]),
        compiler_params=_params(("arbitrary", "arbitrary")),
        name="mixer_b",
    )(proj, proj, proj, rev_b, lam_vecs, subln_g, *weights)


def _attn_out_kernel(ya_ref, yb_ref, ga_ref, gb_ref, x_ref, wa_ref, wb_ref, wo_ref, g_ref, x1_ref, h_ref):
    pa = jnp.dot(ya_ref[...], wa_ref[...], preferred_element_type=F32)
    pb = jnp.dot(yb_ref[...], wb_ref[...], preferred_element_type=F32)
    ga = jax.nn.sigmoid(ga_ref[...].astype(F32))
    gb = jax.nn.sigmoid(gb_ref[...].astype(F32))
    merged = (ga * pa + gb * pb).astype(BF16)
    slab = merged.shape[0] // 2
    for r in range(2):
        rows = slice(r * slab, (r + 1) * slab)
        x1 = x_ref[rows, :] + jnp.dot(merged[rows], wo_ref[...], preferred_element_type=F32)
        x1_ref[rows, :] = x1
        h_ref[rows, :] = _rms(x1, g_ref[...]).astype(BF16)


def _attn_out(ya, yb, proj_r, x2, wa, wb, wo, g, tm=512):
    T, D = x2.shape

    def resident(shape):
        return pl.BlockSpec(shape, lambda i: (0, 0), pipeline_mode=pl.Buffered(1))

    return pl.pallas_call(
        _attn_out_kernel,
        grid=(T // tm,),
        in_specs=[pl.BlockSpec((tm, OUT_WIDTH_A), lambda i: (i, 0)),
                  pl.BlockSpec((tm, WIDTH_B), lambda i: (i, 0)),
                  pl.BlockSpec((tm, D), lambda i: (i, OFF_GA // D)),
                  pl.BlockSpec((tm, D), lambda i: (i, OFF_GB // D)),
                  pl.BlockSpec((tm, D), lambda i: (i, 0)),
                  resident((OUT_WIDTH_A, D)), resident((WIDTH_B, D)), resident((D, D)), resident((1, D))],
        out_specs=[pl.BlockSpec((tm, D), lambda i: (i, 0)), pl.BlockSpec((tm, D), lambda i: (i, 0))],
        out_shape=[jax.ShapeDtypeStruct((T, D), F32), jax.ShapeDtypeStruct((T, D), BF16)],
        compiler_params=_params(("parallel",), vmem=VMEM_LIMIT_ATTN_OUT),
        name="attn_out",
    )(ya, yb, proj_r, proj_r, x2, wa, wb, wo, g)


def _ffn_kernel(x1_hbm, h_ref, wg_ref, wu_ref, wd_ref, gf_ref, o_ref, x1_buf, x1_sem, *, tm):
    i = pl.program_id(0)
    f = pl.program_id(1)
    x1_copy = pltpu.make_async_copy(x1_hbm.at[pl.ds(i * tm, tm), :], x1_buf, x1_sem)

    last = pl.num_programs(1) - 1

    def gated():
        h = h_ref[...]
        a = jnp.dot(h, wg_ref[...], preferred_element_type=F32)
        b = jnp.dot(h, wu_ref[...], preferred_element_type=F32)
        return ((a * jax.nn.sigmoid(a)) * b).astype(BF16)

    @pl.when(f == 0)
    def _():
        x1_copy.start()
        o_ref[...] = jnp.dot(gated(), wd_ref[...], preferred_element_type=F32)

    @pl.when((f > 0) & (f < last))
    def _():
        o_ref[...] += jnp.dot(gated(), wd_ref[...], preferred_element_type=F32)

    @pl.when(f == last)
    def _():
        x1_copy.wait()
        u = gated()
        slab = tm // FFN_EPILOGUE_SLABS
        for r in range(FFN_EPILOGUE_SLABS):
            rows = slice(r * slab, (r + 1) * slab)
            y = o_ref[rows, :] + jnp.dot(u[rows], wd_ref[...], preferred_element_type=F32)
            o_ref[rows, :] = _rms(x1_buf[rows, :] + y, gf_ref[...])


def _ffn(x1, h, wg, wu, wd, gf, tm=1024, tf=512):
    T, D = x1.shape
    F = wg.shape[1]
    return pl.pallas_call(
        functools.partial(_ffn_kernel, tm=tm),
        grid=(T // tm, F // tf),
        in_specs=[pl.BlockSpec(memory_space=pl.ANY),
                  pl.BlockSpec((tm, D), lambda i, f: (i, 0)),
                  pl.BlockSpec((D, tf), lambda i, f: (0, f)),
                  pl.BlockSpec((D, tf), lambda i, f: (0, f)),
                  pl.BlockSpec((tf, D), lambda i, f: (f, 0)),
                  pl.BlockSpec((1, D), lambda i, f: (0, 0))],
        out_specs=pl.BlockSpec((tm, D), lambda i, f: (i, 0)),
        out_shape=jax.ShapeDtypeStruct((T, D), F32),
        scratch_shapes=[pltpu.VMEM((tm, D), F32), pltpu.SemaphoreType.DMA(())],
        compiler_params=_params(("parallel", "arbitrary"), vmem=VMEM_LIMIT_FFN),
        name="ffn",
    )(x1, h, wg, wu, wd, gf)


def _rev_a_index():
    u = np.arange(4 * BLOCK)
    rel = np.clip(2 * BLOCK - u, 0, None)
    return np.stack([_rel_bucket_np(rel * d) for _, d in DIL_PATTERNS])


def _rev_b_index(seq, tq):
    c = np.arange(seq + tq)
    return _rel_bucket_np(np.clip(seq - c, 0, seq - 1))


def _lookup(table, idx):
    onehot = jnp.asarray(np.asarray(idx)[None, :] == np.arange(NUM_BUCKETS)[:, None])
    return jnp.sum(jnp.where(onehot[:, None, :], table[:, :, None], 0.0), axis=0)


def kernel(x, norm_attn_g, w_in, w_proj_a, w_proj_b, w_out, rel_bias_table, diff_lambda_q1, diff_lambda_k1, diff_lambda_q2, diff_lambda_k2, diff_subln_g, norm_ffn_g, w_ffn_gate, w_ffn_up, w_ffn_down, norm_final_g):
    B, S, D = x.shape
    T = B * S
    depth = w_in.shape[0]
    assert depth == 1, "the final RMSNorm is fused into the FFN epilogue of a single layer"
    table_a = rel_bias_table[:, :N_HEADS_A].astype(F32)
    table_b = rel_bias_table[:, N_HEADS_A:].astype(F32)
    tq = 256

    idx_a = _rev_a_index()
    rev_a = jnp.stack([_lookup(table_a[:, g * HEADS_PER_GROUP_A:(g + 1) * HEADS_PER_GROUP_A], idx_a[g])
                       for g in range(N_GROUPS_A)])
    npair = OUT_WIDTH_A // LANES
    rev_a = jnp.transpose(rev_a.reshape(N_GROUPS_A, npair, 2, 4 * BLOCK), (1, 0, 2, 3))
    rev_a = rev_a.reshape(npair, 2 * N_GROUPS_A, 4 * BLOCK)
    rev_b = _lookup(table_b, _rev_b_index(S, tq))[:, None, :]

    x2 = x.reshape(T, D)
    l = 0
    lam_init = 0.8 - 0.6 * math.exp(-0.3 * l)
    proj_a, proj_r = _in_proj(x2, norm_attn_g[l][None, :], w_in[l])
    ya, wa, wb, wo = _mixer_a(proj_a.reshape(B, S, PROJ_A), rev_a, [w_proj_a[l], w_proj_b[l], w_out[l]])

    lam_vecs = jnp.stack([diff_lambda_q1[l], diff_lambda_k1[l],
                          diff_lambda_q2[l], diff_lambda_k2[l]]).astype(F32)
    yb, wg, wu, wd = _mixer_b(proj_r.reshape(B, S, PROJ_R), rev_b, lam_vecs, diff_subln_g[l][None, :].astype(F32),
                              lam_init, [w_ffn_gate[l], w_ffn_up[l], w_ffn_down[l]], tq=tq)

    x1, h = _attn_out(ya.reshape(T, OUT_WIDTH_A), yb.reshape(T, WIDTH_B), proj_r, x2, wa, wb, wo,
                      norm_ffn_g[l][None, :])
    out = _ffn(x1, h, wg, wu, wd, norm_final_g[None, :])
    return out.reshape(B, S, D)
```
